```python
import jax, jax.numpy as jnp
from jax import lax
import numpy as np

D_MODEL = 1024
BATCH = 32
SEQ = 256
DEPTH = 4
DEC_BATCH = 8
DEC_SEQ = 1024
PAST_LEN = 256

GRID_W = 64
N_MIXERS = 3
N_A = len(range(0, DEPTH, N_MIXERS))
N_B = len(range(1, DEPTH, N_MIXERS))
N_C = len(range(2, DEPTH, N_MIXERS))
N_MOD = 6
EPS = 1e-6
D_RNN = D_MODEL
RNN_BLOCKS = 16
RNN_BLOCK = D_RNN // RNN_BLOCKS
CONV_W = 4
CONV_LEFT = 2
LRU_C = 8.0
N_HEADS = 16
N_KV = 4
HEAD_DIM = 64
GQA_G = N_HEADS // N_KV
WINDOW = 128
BLOCK = 128
ROPE_BASE = 10000.0
FOURIER_GROUPS = 4
FOURIER_GW = D_MODEL // FOURIER_GROUPS
N_EXPERTS = 16
EXPERT_FF = 1024
EC_FACTOR = 2
F32 = jnp.float32

kernel_name = 'hybrid_flow_rglru_swa_fnet_ecmoe_step'


def _rmsnorm(x, g):
    x32 = x.astype(F32)
    y = x32 * lax.rsqrt(jnp.mean(x32 * x32, axis=-1, keepdims=True) + EPS)
    return (y * g.astype(F32)).astype(x.dtype)


def _modulate(x, g, shift, scale):
    return _rmsnorm(x, g) * (1.0 + scale[:, None, :]) + shift[:, None, :]


def _centred_dwconv(u, w, b):
    S = u.shape[1]
    up = jnp.pad(u, ((0, 0), (CONV_LEFT, CONV_W - 1 - CONV_LEFT), (0, 0)))
    return sum(up[:, k:k + S] * w[k] for k in range(CONV_W)) + b


def _linear_scan(a, b, h0):
    def comb(x, y):
        return (x[0] * y[0], y[0] * x[1] + y[1])
    a_cum, b_cum = lax.associative_scan(comb, (a, b), axis=1)
    return a_cum * h0[:, None, :] + b_cum


def _rglru_coeffs(u, w_a, b_a, w_x, b_x, lam):
    B, S, _ = u.shape
    ub = u.reshape(B, S, RNN_BLOCKS, RNN_BLOCK)
    r = jax.nn.sigmoid(jnp.einsum('bshi,hij->bshj', ub, w_a.astype(F32)).reshape(B, S, D_RNN) + b_a.astype(F32))
    i = jax.nn.sigmoid(jnp.einsum('bshi,hij->bshj', ub, w_x.astype(F32)).reshape(B, S, D_RNN) + b_x.astype(F32))
    log_a = -LRU_C * r * jax.nn.softplus(-lam.astype(F32))
    return jnp.exp(log_a), jnp.sqrt(-jnp.expm1(2.0 * log_a)) * (i * u)


def _recurrent_mixer(xn, w_in, conv_w, conv_b, w_a, b_a, w_x, b_x, lam, w_out, h0_f, h0_b):
    gate, u = jnp.split(xn @ w_in, 2, axis=-1)
    u = _centred_dwconv(u, conv_w, conv_b).astype(F32)
    a_f, b_f = _rglru_coeffs(u, w_a[0], b_a[0], w_x[0], b_x[0], lam[0])
    h_f = _linear_scan(a_f, b_f, h0_f.astype(F32))
    a_b, b_b = _rglru_coeffs(jnp.flip(u, 1), w_a[1], b_a[1], w_x[1], b_x[1], lam[1])
    h_b = _linear_scan(a_b, b_b, h0_b.astype(F32))
    h = h_f + jnp.flip(h_b, 1)
    y = (h * jax.nn.gelu(gate.astype(F32))).astype(xn.dtype) @ w_out
    final = jnp.stack([h_f[:, -1], h_b[:, -1]], axis=1)
    return y, final


def _qkv(xn, w_qkv, q_g, k_g):
    B, S, _ = xn.shape
    nq, nk = N_HEADS * HEAD_DIM, N_KV * HEAD_DIM
    qkv = xn @ w_qkv
    q = qkv[..., :nq].reshape(B, S, N_HEADS, HEAD_DIM)
    k = qkv[..., nq:nq + nk].reshape(B, S, N_KV, HEAD_DIM)
    v = qkv[..., nq + nk:].reshape(B, S, N_KV, HEAD_DIM)
    return _rmsnorm(q, q_g), _rmsnorm(k, k_g), v


def _rope2d(x):
    S = x.shape[1]
    n_rows = S // GRID_W
    row = jnp.repeat(jnp.arange(n_rows), GRID_W).astype(F32)
    col = jnp.tile(jnp.arange(GRID_W), n_rows).astype(F32)
    n_freq = HEAD_DIM // 4
    inv = ROPE_BASE ** (-jnp.arange(n_freq, dtype=F32) / n_freq)
    ar, ac = row[:, None] * inv, col[:, None] * inv
    ang = jnp.concatenate([ar, ar, ac, ac], axis=-1)[None, :, None, :]
    x32 = x.astype(F32)
    x1, x2, x3, x4 = jnp.split(x32, 4, axis=-1)
    rot = jnp.concatenate([-x2, x1, -x4, x3], axis=-1)
    return (x32 * jnp.cos(ang) + rot * jnp.sin(ang)).astype(x.dtype)


def _attend(q, ks, vs, masks, sink):
    logits = []
    for k, m in zip(ks, masks):
        l = jnp.einsum('bqkgd,bskd->bkgqs', q, k, preferred_element_type=F32)
        if m is not None:
            l = jnp.where(m, l, -jnp.inf)
        logits.append(l)
    s = sink[None, :, :, None, None]
    mx = s
    for l in logits:
        mx = jnp.maximum(mx, jnp.max(l, axis=-1, keepdims=True))
    den = jnp.exp(s - mx)
    out = 0.0
    for l, v in zip(logits, vs):
        p = jnp.exp(l - mx)
        den = den + jnp.sum(p, axis=-1, keepdims=True)
        out = out + jnp.einsum('bkgqs,bskd->bqkgd', p.astype(v.dtype), v, preferred_element_type=F32)
    den = jnp.transpose(den, (0, 3, 1, 2, 4))
    return (out / den).astype(q.dtype)


def _context_attention(q, k, v, sink):
    B, L = q.shape[:2]
    nb = L // BLOCK
    qb = jnp.moveaxis(q.reshape(B, nb, BLOCK, N_KV, GQA_G, HEAD_DIM), 1, 0)
    ob = lax.map(lambda qq: _attend(qq, [k], [v], [None], sink), qb)
    return jnp.moveaxis(ob, 0, 1).reshape(B, L, N_HEADS * HEAD_DIM)


def _latent_attention(q, k, v, k_ctx, v_ctx, sink):
    B, S = q.shape[:2]
    nb = S // BLOCK
    span = BLOCK + 2 * WINDOW
    kp = jnp.pad(k, ((0, 0), (WINDOW, WINDOW), (0, 0), (0, 0)))
    vp = jnp.pad(v, ((0, 0), (WINDOW, WINDOW), (0, 0), (0, 0)))
    qb = jnp.moveaxis(q.reshape(B, nb, BLOCK, N_KV, GQA_G, HEAD_DIM), 1, 0)

    def blk(args):
        j, qq = args
        start = j * BLOCK
        kk = lax.dynamic_slice_in_dim(kp, start, span, axis=1)
        vv = lax.dynamic_slice_in_dim(vp, start, span, axis=1)
        qpos = start + jnp.arange(BLOCK)
        kpos = start - WINDOW + jnp.arange(span)
        mask = (kpos[None, :] >= 0) & (kpos[None, :] < S) & (jnp.abs(qpos[:, None] - kpos[None, :]) <= WINDOW)
        return _attend(qq, [kk, k_ctx], [vv, v_ctx], [mask, None], sink)

    ob = lax.map(blk, (jnp.arange(nb), qb))
    return jnp.moveaxis(ob, 0, 1).reshape(B, S, N_HEADS * HEAD_DIM)


def _fourier_mixer(xn, w_f):
    B, S, D = xn.shape
    xg = xn.astype(F32).reshape(B, S, FOURIER_GROUPS, FOURIER_GW)
    f = jnp.fft.fft2(xg, axes=(1, 3), norm='ortho').real
    return f.reshape(B, S, D).astype(xn.dtype) @ w_f


def _expert_choice_moe(xn, w_router, w_gate, w_up, w_down):
    B, S, D = xn.shape
    n_tok = B * S
    cap = EC_FACTOR * n_tok // N_EXPERTS
    xt = xn.reshape(n_tok, D)
    aff = jax.nn.softmax(jnp.dot(xt, w_router, preferred_element_type=F32), axis=-1)
    g, idx = lax.top_k(aff.T, cap)
    xe = xt[idx]
    h = jax.nn.silu(jnp.einsum('ecd,edf->ecf', xe, w_gate)) * jnp.einsum('ecd,edf->ecf', xe, w_up)
    ye = jnp.einsum('ecf,efd->ecd', h, w_down) * g[..., None].astype(xn.dtype)
    out = jnp.zeros_like(xt).at[idx.reshape(-1)].add(ye.reshape(-1, D))
    return out.reshape(B, S, D)


def setup_inputs(seed: int = 0) -> dict:
    key = jax.random.key(seed)
    keys = iter(jax.random.split(key, 40))

    def nrm(shape, scale):
        return jax.random.normal(next(keys), shape, F32) * scale

    D = D_MODEL
    u = jax.random.uniform(next(keys), (N_A, 2, D_RNN), F32, 0.9, 0.999)
    a_base = u ** (1.0 / LRU_C)
    return {
        'x_prompt': nrm((BATCH, SEQ, D), 1.0),
        'x_sample': nrm((DEC_BATCH, DEC_SEQ, D), 1.0),
        'state_rglru': nrm((DEC_BATCH, N_A, 2, D_RNN), 0.5),
        'cache_k': nrm((DEC_BATCH, N_B, PAST_LEN, N_KV, HEAD_DIM), 1.0),
        'cache_v': nrm((DEC_BATCH, N_B, PAST_LEN, N_KV, HEAD_DIM), 1.0),
        'c': nrm((DEC_BATCH, D), 1.0),
        'c_ctx': nrm((D,), 1.0),
        'mod_w': nrm((DEPTH, D, N_MOD * D), 0.5 * D ** -0.5),
        'mod_b': nrm((DEPTH, N_MOD * D), 0.01),
        'norm1_g': 1.0 + nrm((DEPTH, D), 0.05),
        'norm2_g': 1.0 + nrm((DEPTH, D), 0.05),
        'rg_w_in': nrm((N_A, D, 2 * D_RNN), D ** -0.5),
        'rg_conv_w': nrm((N_A, CONV_W, D_RNN), CONV_W ** -0.5),
        'rg_conv_b': nrm((N_A, D_RNN), 0.01),
        'rg_w_a': nrm((N_A, 2, RNN_BLOCKS, RNN_BLOCK, RNN_BLOCK), RNN_BLOCK ** -0.5),
        'rg_b_a': nrm((N_A, 2, D_RNN), 0.01),
        'rg_w_x': nrm((N_A, 2, RNN_BLOCKS, RNN_BLOCK, RNN_BLOCK), RNN_BLOCK ** -0.5),
        'rg_b_x': nrm((N_A, 2, D_RNN), 0.01),
        'rg_lambda': jnp.log(a_base) - jnp.log1p(-a_base),
        'rg_w_out': nrm((N_A, D_RNN, D), D_RNN ** -0.5),
        'at_w_qkv': nrm((N_B, D, (N_HEADS + 2 * N_KV) * HEAD_DIM), D ** -0.5),
        'at_q_norm': 1.0 + nrm((N_B, HEAD_DIM), 0.05),
        'at_k_norm': 1.0 + nrm((N_B, HEAD_DIM), 0.05),
        'at_sink': nrm((N_B, N_HEADS), 0.5),
        'at_w_o': nrm((N_B, N_HEADS * HEAD_DIM, D), (N_HEADS * HEAD_DIM) ** -0.5),
        'ft_w': nrm((N_C, D, D), D ** -0.5),
        'moe_router': nrm((DEPTH, D, N_EXPERTS), D ** -0.5),
        'moe_w_gate': nrm((DEPTH, N_EXPERTS, D, EXPERT_FF), D ** -0.5),
        'moe_w_up': nrm((DEPTH, N_EXPERTS, D, EXPERT_FF), D ** -0.5),
        'moe_w_down': nrm((DEPTH, N_EXPERTS, EXPERT_FF, D), EXPERT_FF ** -0.5),
    }


def reference(x_prompt, x_sample, state_rglru, cache_k, cache_v, c, c_ctx,
              mod_w, mod_b, norm1_g, norm2_g,
              rg_w_in, rg_conv_w, rg_conv_b, rg_w_a, rg_b_a, rg_w_x, rg_b_x, rg_lambda, rg_w_out,
              at_w_qkv, at_q_norm, at_k_norm, at_sink, at_w_o,
              ft_w, moe_router, moe_w_gate, moe_w_up, moe_w_down):
    q_scale = HEAD_DIM ** -0.5
    cond_ctx = jax.nn.silu(c_ctx)[None, :]
    cond_lat = jax.nn.silu(c)
    xp, xs = x_prompt, x_sample
    new_rg, new_k, new_v = [], [], []
    for layer in range(DEPTH):
        kind, j = layer % N_MIXERS, layer // N_MIXERS
        mp = jnp.split(cond_ctx @ mod_w[layer] + mod_b[layer], N_MOD, axis=-1)
        ms = jnp.split(cond_lat @ mod_w[layer] + mod_b[layer], N_MOD, axis=-1)
        hp = _modulate(xp, norm1_g[layer], mp[0], mp[1])
        hs = _modulate(xs, norm1_g[layer], ms[0], ms[1])
        if kind == 0:
            rg = (rg_w_in[j], rg_conv_w[j], rg_conv_b[j], rg_w_a[j], rg_b_a[j],
                  rg_w_x[j], rg_b_x[j], rg_lambda[j], rg_w_out[j])
            h0 = jnp.zeros((xp.shape[0], D_RNN), F32)
            yp, st = _recurrent_mixer(hp, *rg, h0, h0)
            ys, _ = _recurrent_mixer(hs, *rg, state_rglru[:, j, 0], state_rglru[:, j, 1])
            new_rg.append(st)
        elif kind == 1:
            sink = at_sink[j].astype(F32).reshape(N_KV, GQA_G)
            qp, kp, vp = _qkv(hp, at_w_qkv[j], at_q_norm[j], at_k_norm[j])
            yp = _context_attention(qp * q_scale, kp, vp, sink) @ at_w_o[j]
            qs, ks, vs = _qkv(hs, at_w_qkv[j], at_q_norm[j], at_k_norm[j])
            ys = _latent_attention(_rope2d(qs) * q_scale, _rope2d(ks), vs,
                                   cache_k[:, j], cache_v[:, j], sink) @ at_w_o[j]
            new_k.append(kp)
            new_v.append(vp)
        else:
            yp = _fourier_mixer(hp, ft_w[j])
            ys = _fourier_mixer(hs, ft_w[j])
        xp = xp + mp[2][:, None, :] * yp
        xs = xs + ms[2][:, None, :] * ys
        moe = (moe_router[layer], moe_w_gate[layer], moe_w_up[layer], moe_w_down[layer])
        xp = xp + mp[5][:, None, :] * _expert_choice_moe(_modulate(xp, norm2_g[layer], mp[3], mp[4]), *moe)
        xs = xs + ms[5][:, None, :] * _expert_choice_moe(_modulate(xs, norm2_g[layer], ms[3], ms[4]), *moe)
    return (xp, xs, jnp.stack(new_rg, axis=1), jnp.stack(new_k, axis=1), jnp.stack(new_v, axis=1))
```

```python
import functools
import math

import numpy as np
import jax
import jax.numpy as jnp
from jax import lax
from jax.experimental import pallas as pl
from jax.experimental.pallas import tpu as pltpu

F32 = jnp.float32
BF16 = jnp.bfloat16
I32 = jnp.int32
U32 = jnp.uint32

D = 1024
N_MOD = 6
EPS = 1e-6
GRID_W = 64
CONV_W = 4
CONV_LEFT = 2
LRU_C = 8.0
RNN_BLOCKS = 16
RNN_BLOCK = D // RNN_BLOCKS
N_HEADS = 16
N_KV = 4
HEAD_DIM = 64
GQA_G = N_HEADS // N_KV
WINDOW = 128
ROPE_BASE = 10000.0
FOURIER_GROUPS = 4
FOURIER_GW = D // FOURIER_GROUPS
N_EXPERTS = 16
EC_FACTOR = 2

LANES = 128
SUBLANES = 8
VMEM_LIMIT_BYTES = 56 * 1024 * 1024

TOK_TILE = 256
MOD_ROWS = 16
SCAN_CC = 128
SCAN_TC = 32
FF_CHUNK = 512
XPK_W = D // 2 + LANES


def _cparams(sem, vmem=None):
    return pltpu.CompilerParams(dimension_semantics=sem, vmem_limit_bytes=vmem)


def _split2(a):
    hi = a.astype(BF16)
    lo = (a - hi.astype(F32)).astype(BF16)
    return hi, lo


def _dot(a, b):
    return jnp.dot(a, b, preferred_element_type=F32)


def _dot3(a, b):
    a_hi, a_lo = _split2(a)
    b_hi, b_lo = _split2(b)
    return _dot(a_hi, b_hi) + _dot(a_hi, b_lo) + _dot(a_lo, b_hi)


def _norm_mod(x, g, shift, scale):
    ms = jnp.mean(x * x, axis=-1, keepdims=True)
    y = x * lax.rsqrt(ms + EPS) * g
    return y * (1.0 + scale) + shift


def _mod_slice(m, k):
    return m[:, k * D:(k + 1) * D]


class _Stream:
    def __init__(self, batch, seq, shared_cond):
        self.batch, self.seq, self.shared = batch, seq, shared_cond
        self.n_tok = batch * seq
        self.tiles = self.n_tok // TOK_TILE
        self.tiles_per_seq = seq // TOK_TILE

    def mod_row(self, i):
        return 0 if self.shared else 1 + i // self.tiles_per_seq

    def mod_spec(self, layer):
        return pl.BlockSpec((None, None, 1, N_MOD * D), lambda i: (layer, self.mod_row(i), 0, 0))

    def seq_major_spec(self):
        tps = self.tiles_per_seq
        return pl.BlockSpec((TOK_TILE, D), lambda i: (i % tps, i // tps))


def _tok_spec(width=D):
    return pl.BlockSpec((TOK_TILE, width), lambda i: (i, 0))


def _const_spec(shape):
    nd = len(shape)
    return pl.BlockSpec(shape, lambda i: (0,) * nd)


def _mod_kernel(c_ref, w_ref, b_ref, o_ref):
    c = c_ref[...]
    c = c * jax.nn.sigmoid(c)
    o_ref[...] = _dot3(c, w_ref[...]) + b_ref[...]


def _modulation(cond, mod_w, mod_b):
    depth = mod_w.shape[0]
    tn = N_MOD * D // 4
    return pl.pallas_call(
        _mod_kernel,
        grid=(depth, N_MOD * D // tn),
        in_specs=[pl.BlockSpec((MOD_ROWS, D), lambda l, n: (0, 0)),
                  pl.BlockSpec((None, D, tn), lambda l, n: (l, 0, n)),
                  pl.BlockSpec((None, 1, tn), lambda l, n: (l, 0, n))],
        out_specs=pl.BlockSpec((None, MOD_ROWS, tn), lambda l, n: (l, 0, n)),
        out_shape=jax.ShapeDtypeStruct((depth, MOD_ROWS, N_MOD * D), F32),
        compiler_params=_cparams(("arbitrary", "arbitrary"), VMEM_LIMIT_BYTES),
        name="modulation",
    )(cond, mod_w, mod_b.reshape(depth, 1, N_MOD * D))


def _rg_in_kernel(x_ref, mod_ref, g_ref, w_ref, gate_ref, u_ref):
    m = mod_ref[...]
    h = _norm_mod(x_ref[...], g_ref[...], _mod_slice(m, 0), _mod_slice(m, 1))
    gu = _dot(h.astype(BF16), w_ref[...])
    gate_ref[...] = gu[:, :D]
    u_ref[...] = gu[:, D:]


def _rg_in(st, x, mods, layer, g1, w_in):
    out = jax.ShapeDtypeStruct((st.seq, st.batch * D), F32)
    return pl.pallas_call(
        _rg_in_kernel,
        grid=(st.tiles,),
        in_specs=[_tok_spec(), st.mod_spec(layer), _const_spec((1, D)), _const_spec((D, 2 * D))],
        out_specs=[st.seq_major_spec(), st.seq_major_spec()],
        out_shape=[out, out],
        compiler_params=_cparams(("arbitrary",), VMEM_LIMIT_BYTES),
        name="rg_in",
    )(x, mods, g1, w_in)


def _softplus(z):
    return jnp.maximum(z, 0.0) + jnp.log1p(jnp.exp(-jnp.abs(z)))


def _rg_scan_kernel(u_ref, h0_ref, cw_ref, cb_ref, wg_ref, bg_ref, lam_ref, h_ref, fin_ref,
                    upad, a_f, b_f, a_b, b_b, *, seq):
    cc = u_ref.shape[-1]
    pad_hi = CONV_W - 1 - CONV_LEFT
    upad[0:CONV_LEFT] = jnp.zeros((CONV_LEFT, SUBLANES, cc), F32)
    upad[CONV_LEFT:CONV_LEFT + seq] = u_ref[...]
    upad[CONV_LEFT + seq:CONV_LEFT + seq + pad_hi] = jnp.zeros((pad_hi, SUBLANES, cc), F32)
    sp = _softplus(-lam_ref[...])
    cw = cw_ref[...]
    rows = SCAN_TC * SUBLANES

    def coef(c, carry):
        t0 = pl.multiple_of(c * SCAN_TC, SCAN_TC)
        uc = cb_ref[...] + cw[0:1] * upad[pl.ds(t0, SCAN_TC)]
        for k in range(1, CONV_W):
            uc = uc + cw[k:k + 1] * upad[pl.ds(t0 + k, SCAN_TC)]
        u2 = uc.reshape(rows, cc)
        gts = jax.nn.sigmoid(_dot(u2.astype(BF16), wg_ref[...]) + bg_ref[...])
        for d, (a_s, b_s) in enumerate(((a_f, b_f), (a_b, b_b))):
            r = gts[:, (2 * d) * cc:(2 * d + 1) * cc]
            i = gts[:, (2 * d + 1) * cc:(2 * d + 2) * cc]
            log_a = (-LRU_C) * r * sp[d]
            th = jnp.tanh(log_a)
            one_minus_a2 = (-2.0) * th / (1.0 - th)
            a_s[pl.ds(t0, SCAN_TC)] = jnp.exp(log_a).reshape(SCAN_TC, SUBLANES, cc)
            b_s[pl.ds(t0, SCAN_TC)] = (jnp.sqrt(one_minus_a2) * (i * u2)).reshape(SCAN_TC, SUBLANES, cc)
        return carry

    lax.fori_loop(0, seq // SCAN_TC, coef, 0)

    def step(t, carry):
        hf, hb = carry
        hf = a_f[t] * hf + b_f[t]
        b_f[t] = hf
        tb = seq - 1 - t
        hb = a_b[tb] * hb + b_b[tb]
        b_b[tb] = hb
        return hf, hb

    hf, hb = lax.fori_loop(0, seq, step, (h0_ref[0], h0_ref[1]), unroll=8)
    fin_ref[0] = hf
    fin_ref[1] = hb
    h_ref[...] = b_f[...] + b_b[...]


def _blockdiag_pairs(w):
    per = SCAN_CC // RNN_BLOCK
    w4 = w.reshape(D // SCAN_CC, per, RNN_BLOCK, RNN_BLOCK)
    eye = jnp.eye(per, dtype=w.dtype)
    return jnp.einsum('cipq,ij->cipjq', w4, eye).reshape(D // SCAN_CC, SCAN_CC, SCAN_CC)


def _rg_scan(st, u, h0, conv_w, conv_b, w_a, b_a, w_x, b_x, lam):
    seq, batch = st.seq, st.batch
    n_cc = D // SCAN_CC
    wg = jnp.concatenate([_blockdiag_pairs(w_a[0]), _blockdiag_pairs(w_x[0]),
                          _blockdiag_pairs(w_a[1]), _blockdiag_pairs(w_x[1])], axis=-1).astype(BF16)
    bg = jnp.concatenate([b.reshape(n_cc, 1, SCAN_CC) for b in (b_a[0], b_x[0], b_a[1], b_x[1])], axis=-1)
    blk = (seq, SUBLANES, SCAN_CC)
    scr = pltpu.VMEM(blk, F32)
    h, fin = pl.pallas_call(
        functools.partial(_rg_scan_kernel, seq=seq),
        grid=(batch // SUBLANES, n_cc),
        in_specs=[pl.BlockSpec(blk, lambda b, c: (0, b, c)),
                  pl.BlockSpec((2, SUBLANES, SCAN_CC), lambda b, c: (0, b, c)),
                  pl.BlockSpec((CONV_W, 1, SCAN_CC), lambda b, c: (0, 0, c)),
                  pl.BlockSpec((1, 1, SCAN_CC), lambda b, c: (0, 0, c)),
                  pl.BlockSpec((None, SCAN_CC, 4 * SCAN_CC), lambda b, c: (c, 0, 0)),
                  pl.BlockSpec((None, 1, 4 * SCAN_CC), lambda b, c: (c, 0, 0)),
                  pl.BlockSpec((2, 1, SCAN_CC), lambda b, c: (0, 0, c))],
        out_specs=[pl.BlockSpec(blk, lambda b, c: (0, b, c)),
                   pl.BlockSpec((2, SUBLANES, SCAN_CC), lambda b, c: (0, b, c))],
        out_shape=[jax.ShapeDtypeStruct((seq, batch, D), F32),
                   jax.ShapeDtypeStruct((2, batch, D), F32)],
        scratch_shapes=[pltpu.VMEM((seq + CONV_W - 1, SUBLANES, SCAN_CC), F32), scr, scr, scr, scr],
        compiler_params=_cparams(("arbitrary", "arbitrary"), VMEM_LIMIT_BYTES),
        name="rg_scan",
    )(u.reshape(seq, batch, D), h0, conv_w.reshape(CONV_W, 1, D), conv_b.reshape(1, 1, D),
      wg, bg, lam.reshape(2, 1, D))
    return h.reshape(seq, batch * D), fin


def _route_and_pack(x, m, g2_ref, wr_ref, xo_ref, xpk_ref, aff_ref):
    xo_ref[...] = x
    xn = _norm_mod(x, g2_ref[...], _mod_slice(m, 3), _mod_slice(m, 4))
    x_hi = xn.astype(BF16)
    x_hi32 = x_hi.astype(F32)
    x_lo = (xn - x_hi32).astype(BF16)
    wr = wr_ref[...]
    l1 = _dot(x_hi, wr)
    l2 = _dot(x_lo, wr)
    e = N_EXPERTS
    logit = (l1 + pltpu.roll(l1, LANES - e, 1) + pltpu.roll(l1, LANES - 2 * e, 1)
             + l2 + pltpu.roll(l2, LANES - e, 1))
    lane = lax.broadcasted_iota(I32, logit.shape, 1)
    logit = jnp.where(lane < e, logit, -jnp.inf)
    mx = jnp.max(logit, axis=1, keepdims=True)
    ex = jnp.exp(logit - mx)
    aff = ex / jnp.sum(ex, axis=1, keepdims=True)
    bits = pltpu.bitcast(x_hi32, U32)
    half = D // 2
    xpk_ref[:, 0:half] = (bits[:, :half] >> 16) | (bits[:, half:] & jnp.uint32(0xFFFF0000))
    xpk_ref[:, half:half + LANES] = pltpu.bitcast(aff, U32)
    aff_t = aff.T
    for k in range(TOK_TILE // LANES):
        aff_ref[k] = aff_t[0:e, k * LANES:(k + 1) * LANES]


def _lin_out_kernel(a_ref, w_ref, x_ref, mod_ref, g2_ref, wr_ref, xo_ref, xpk_ref, aff_ref):
    m = mod_ref[...]
    y = _dot(a_ref[...], w_ref[...])
    x = x_ref[...] + _mod_slice(m, 2) * y
    _route_and_pack(x, m, g2_ref, wr_ref, xo_ref, xpk_ref, aff_ref)


def _lin_out_gated_kernel(h_ref, gate_ref, w_ref, x_ref, mod_ref, g2_ref, wr_ref, xo_ref, xpk_ref, aff_ref):
    m = mod_ref[...]
    a = (h_ref[...] * jax.nn.gelu(gate_ref[...])).astype(BF16)
    y = _dot(a, w_ref[...])
    x = x_ref[...] + _mod_slice(m, 2) * y
    _route_and_pack(x, m, g2_ref, wr_ref, xo_ref, xpk_ref, aff_ref)


def _router_pieces(w_router):
    hi = w_router.astype(BF16)
    r1 = w_router - hi.astype(F32)
    mid = r1.astype(BF16)
    lo = (r1 - mid.astype(F32)).astype(BF16)
    pad = jnp.zeros((D, LANES - 3 * N_EXPERTS), BF16)
    return jnp.concatenate([hi, mid, lo, pad], axis=1)


def _lin_out(st, srcs, w, x, mods, layer, g2, wr, gated):
    n_chunks = st.n_tok // LANES
    if gated:
        body, src_specs = _lin_out_gated_kernel, [st.seq_major_spec(), st.seq_major_spec()]
    else:
        body, src_specs = _lin_out_kernel, [_tok_spec()]
    return pl.pallas_call(
        body,
        grid=(st.tiles,),
        in_specs=src_specs + [_const_spec((D, D)), _tok_spec(), st.mod_spec(layer), _const_spec((1, D)),
                              _const_spec((D, LANES))],
        out_specs=[_tok_spec(), _tok_spec(XPK_W),
                   pl.BlockSpec((TOK_TILE // LANES, N_EXPERTS, LANES), lambda i: (i, 0, 0))],
        out_shape=[jax.ShapeDtypeStruct((st.n_tok, D), F32),
                   jax.ShapeDtypeStruct((st.n_tok, XPK_W), U32),
                   jax.ShapeDtypeStruct((n_chunks, N_EXPERTS, LANES), F32)],
        compiler_params=_cparams(("arbitrary",), VMEM_LIMIT_BYTES),
        name="lin_out_gated" if gated else "lin_out",
    )(*srcs, w, x, mods, g2, wr)


def _select_kernel(a_ref, idx_ref, cum_ref, linc_s, cnt_s, crow_s, *, cap):
    n_e, n_ch, _ = a_ref.shape
    assert n_ch & (n_ch - 1) == 0 and n_ch <= LANES
    rows = n_e * n_ch
    bits = pltpu.bitcast(a_ref[...], I32)
    capf = jnp.float32(cap)

    def count(mask3):
        c = jnp.sum(mask3.astype(F32), axis=2, keepdims=True)
        return jnp.sum(c, axis=1, keepdims=True)

    def search(i, thr):
        cand = thr | (jnp.int32(1) << (30 - i))
        return jnp.where(count(bits >= cand) >= capf, cand, thr)

    thr = lax.fori_loop(0, 31, search, jnp.zeros((n_e, 1, 1), I32))
    gt3 = bits > thr
    eq3 = bits == thr
    need = capf - count(gt3)

    li = lax.broadcasted_iota(I32, (LANES, LANES), 0)
    lj = lax.broadcasted_iota(I32, (LANES, LANES), 1)
    upper = (li <= lj).astype(BF16)
    ones = jnp.ones((LANES, LANES), BF16)
    ri = lax.broadcasted_iota(I32, (rows, rows), 0)
    rj = lax.broadcasted_iota(I32, (rows, rows), 1)
    sh = n_ch.bit_length() - 1
    before = (((ri >> sh) == (rj >> sh)) & (rj < ri)).astype(BF16)

    def prefixes(mask3):
        x = mask3.reshape(rows, LANES).astype(BF16)
        local = _dot(x, upper)
        tot = _dot(x, ones)
        return local, tot, _dot(before, tot.astype(BF16))

    gt_l, gt_t, gt_x = prefixes(gt3)
    eq_l, eq_t, eq_x = prefixes(eq3)
    need_r = jnp.broadcast_to(need, (n_e, n_ch, LANES)).reshape(rows, LANES)
    sel_incl = gt_x + gt_l + jnp.minimum(eq_x + eq_l, need_r)
    sel_x = gt_x + jnp.minimum(eq_x, need_r)
    sel_c = gt_x + gt_t + jnp.minimum(eq_x + eq_t, need_r)
    linc_s[...] = sel_incl - sel_x
    cnt_s[...] = sel_c - sel_x
    c3 = sel_c.reshape(n_e, n_ch, LANES)
    pick = lax.broadcasted_iota(I32, (1, n_ch, LANES), 1) == lax.broadcasted_iota(I32, (1, n_ch, LANES), 2)
    crow = jnp.sum(jnp.where(pick, c3, 0.0), axis=1)
    crow = jnp.where(lax.broadcasted_iota(I32, crow.shape, 1) < n_ch, crow, jnp.float32(2 * cap + n_ch * LANES))
    crow_s[...] = crow
    cum_ref[...] = crow.astype(I32)

    slot = lax.broadcasted_iota(I32, (cap, LANES), 0).astype(F32)
    lane = lax.broadcasted_iota(I32, (cap, LANES), 1).astype(F32)
    diag = lax.broadcasted_iota(I32, (LANES, LANES), 0) == lax.broadcasted_iota(I32, (LANES, LANES), 1)
    zpad = jnp.zeros((LANES - n_ch, LANES), BF16)

    def per_expert(e, carry):
        r0 = pl.multiple_of(e * n_ch, n_ch)
        passed = (crow_s[pl.ds(e, 1), :] <= slot).astype(BF16)
        cnt_e = jnp.concatenate([cnt_s[pl.ds(r0, n_ch), :].astype(BF16), zpad], axis=0)
        linc_e = jnp.concatenate([linc_s[pl.ds(r0, n_ch), :].astype(BF16), zpad], axis=0)
        chunk = _dot(passed, ones)
        rank = slot - _dot(passed, cnt_e)
        onehot = (chunk == lane).astype(BF16)
        g = _dot(onehot, linc_e)
        within = _dot((g <= rank).astype(BF16), ones)
        tok = chunk * jnp.float32(LANES) + within
        pieces = []
        for b in range(cap // LANES):
            blk = tok[b * LANES:(b + 1) * LANES, :]
            pieces.append(jnp.sum(jnp.where(diag, blk, 0.0), axis=0, keepdims=True))
        idx_ref[pl.ds(e, 1), :] = jnp.concatenate(pieces, axis=1).astype(I32)
        return carry

    lax.fori_loop(0, n_e, per_expert, 0)


def _select(aff_chunks, cap):
    n_ch = aff_chunks.shape[0]
    a = jnp.transpose(aff_chunks, (1, 0, 2))
    rows = N_EXPERTS * n_ch
    return pl.pallas_call(
        functools.partial(_select_kernel, cap=cap),
        out_shape=[jax.ShapeDtypeStruct((N_EXPERTS, cap), I32),
                   jax.ShapeDtypeStruct((N_EXPERTS, LANES), I32)],
        scratch_shapes=[pltpu.VMEM((rows, LANES), F32), pltpu.VMEM((rows, LANES), F32),
                        pltpu.VMEM((N_EXPERTS, LANES), F32)],
        compiler_params=pltpu.CompilerParams(vmem_limit_bytes=VMEM_LIMIT_BYTES),
        name="select",
    )(a)


def _moe_ffn_kernel(idx_ref, xpk_ref, wg_ref, wu_ref, wd_ref, o_ref, xg_s, xe_s, g_s, *, cap):
    e = pl.program_id(0)
    f = pl.program_id(1)
    half = D // 2

    @pl.when(f == 0)
    def _gather():
        def body(g, carry):
            base = pl.multiple_of(g * SUBLANES, SUBLANES)
            rows = [xpk_ref[pl.ds(idx_ref[e, base + k], 1), :] for k in range(SUBLANES)]
            xg_s[pl.ds(base, SUBLANES), :] = jnp.concatenate(rows, axis=0)
            return carry

        lax.fori_loop(0, cap // SUBLANES, body, 0)
        w = xg_s[:, 0:half]
        xe_s[:, 0:half] = pltpu.bitcast(w << 16, F32).astype(BF16)
        xe_s[:, half:D] = pltpu.bitcast(w & jnp.uint32(0xFFFF0000), F32).astype(BF16)
        aff = pltpu.bitcast(xg_s[:, half:half + LANES], F32)
        lane = lax.broadcasted_iota(I32, aff.shape, 1)
        g_s[...] = jnp.sum(jnp.where(lane == e, aff, 0.0), axis=1, keepdims=True)

    xe = xe_s[...]
    hg = _dot(xe, wg_ref[...].astype(BF16))
    hu = _dot(xe, wu_ref[...].astype(BF16))
    h = (hg * jax.nn.sigmoid(hg) * hu).astype(BF16)
    y = _dot(h, wd_ref[...].astype(BF16))

    @pl.when(f == 0)
    def _first():
        o_ref[...] = y

    @pl.when(f > 0)
    def _rest():
        o_ref[...] += y

    @pl.when(f == pl.num_programs(1) - 1)
    def _scale():
        o_ref[...] = o_ref[...] * g_s[...]


def _moe_ffn(idx, xpk, w_gate, w_up, w_down, cap):
    n_tok = xpk.shape[0]
    ff = w_gate.shape[-1]
    grid_spec = pltpu.PrefetchScalarGridSpec(
        num_scalar_prefetch=1,
        grid=(N_EXPERTS, ff // FF_CHUNK),
        in_specs=[pl.BlockSpec((n_tok, XPK_W), lambda e, f, idx: (0, 0), pipeline_mode=pl.Buffered(1)),
                  pl.BlockSpec((None, D, FF_CHUNK), lambda e, f, idx: (e, 0, f)),
                  pl.BlockSpec((None, D, FF_CHUNK), lambda e, f, idx: (e, 0, f)),
                  pl.BlockSpec((None, FF_CHUNK, D), lambda e, f, idx: (e, f, 0))],
        out_specs=pl.BlockSpec((None, cap, D), lambda e, f, idx: (e, 0, 0)),
        scratch_shapes=[pltpu.VMEM((cap, XPK_W), U32), pltpu.VMEM((cap, D), BF16), pltpu.VMEM((cap, 1), F32)],
    )
    return pl.pallas_call(
        functools.partial(_moe_ffn_kernel, cap=cap),
        grid_spec=grid_spec,
        out_shape=jax.ShapeDtypeStruct((N_EXPERTS, cap, D), F32),
        compiler_params=_cparams(("arbitrary", "arbitrary"), VMEM_LIMIT_BYTES),
        name="moe_ffn",
    )(idx, xpk, w_gate, w_up, w_down)


def _combine_kernel(idx_ref, bnd_ref, ye_ref, o_ref, *, cap, span):
    s = pl.program_id(0)
    e = pl.program_id(1)

    @pl.when(e == 0)
    def _zero():
        o_ref[...] = jnp.zeros(o_ref.shape, F32)

    lo = bnd_ref[e, s]
    hi = bnd_ref[e, s + 1]
    base = s * span

    def body(p, carry):
        n = idx_ref[e, p] - base
        o_ref[pl.ds(n, 1), :] = o_ref[pl.ds(n, 1), :] + ye_ref[pl.ds(p, 1), :]
        return carry

    lax.fori_loop(lo, hi, body, 0)


def _combine(idx, bnd, ye, n_tok, cap):
    n_span = bnd.shape[1] - 1
    span = n_tok // n_span
    grid_spec = pltpu.PrefetchScalarGridSpec(
        num_scalar_prefetch=2,
        grid=(n_span, N_EXPERTS),
        in_specs=[pl.BlockSpec((None, cap, D), lambda s, e, idx, bnd: (e, 0, 0))],
        out_specs=pl.BlockSpec((span, D), lambda s, e, idx, bnd: (s, 0)),
    )
    return pl.pallas_call(
        functools.partial(_combine_kernel, cap=cap, span=span),
        grid_spec=grid_spec,
        out_shape=jax.ShapeDtypeStruct((n_tok, D), F32),
        compiler_params=_cparams(("arbitrary", "arbitrary"), VMEM_LIMIT_BYTES),
        name="moe_combine",
    )(idx, bnd, ye)


def _residual_kernel(x_ref, y_ref, mod_ref, o_ref):
    o_ref[...] = x_ref[...] + _mod_slice(mod_ref[...], 5) * y_ref[...]


def _residual(st, x, y, mods, layer):
    return pl.pallas_call(
        _residual_kernel,
        grid=(st.tiles,),
        in_specs=[_tok_spec(), _tok_spec(), st.mod_spec(layer)],
        out_specs=_tok_spec(),
        out_shape=jax.ShapeDtypeStruct((st.n_tok, D), F32),
        compiler_params=_cparams(("arbitrary",), VMEM_LIMIT_BYTES),
        name="moe_residual",
    )(x, y, mods)


COMBINE_SPANS = 2


def _moe(st, x, xpk, aff_chunks, mods, layer, w_gate, w_up, w_down):
    cap = EC_FACTOR * st.n_tok // N_EXPERTS
    idx, cum = _select(aff_chunks, cap)
    n_ch = st.n_tok // LANES
    edges = [cum[:, (k * n_ch) // COMBINE_SPANS - 1:(k * n_ch) // COMBINE_SPANS] for k in range(1, COMBINE_SPANS)]
    bnd = jnp.concatenate([jnp.zeros((N_EXPERTS, 1), I32)] + edges + [jnp.full((N_EXPERTS, 1), cap, I32)], axis=1)
    ye = _moe_ffn(idx, xpk, w_gate, w_up, w_down, cap)
    y = _combine(idx, bnd, ye, st.n_tok, cap)
    return _residual(st, x, y, mods, layer)


def _head_sumsq(x, bd):
    x2 = x * x
    hi, lo = _split2(x2)
    w = bd.shape[0]
    cols = []
    for c in range(x.shape[1] // w):
        sl = slice(c * w, (c + 1) * w)
        cols.append(_dot(hi[:, sl], bd) + _dot(lo[:, sl], bd))
    return cols[0] if len(cols) == 1 else jnp.concatenate(cols, axis=1)


def _qk_norm(x, gain, bd):
    ms = _head_sumsq(x, bd) * (1.0 / HEAD_DIM)
    return x * lax.rsqrt(ms + EPS) * gain


def _rope(x, cos, sin_dn, sin_up):
    n = x.shape[1]
    q = HEAD_DIM // 4
    return x * cos + pltpu.roll(x, n - q, 1) * sin_dn + pltpu.roll(x, q, 1) * sin_up


def _qkv_ctx_kernel(x_ref, mod_ref, g_ref, w_ref, qg_ref, kg_ref, bd_ref, q_ref, k_ref, v_ref, kc_ref, vc_ref):
    m = mod_ref[...]
    h = _norm_mod(x_ref[...], g_ref[...], _mod_slice(m, 0), _mod_slice(m, 1))
    qkv = _dot(h.astype(BF16), w_ref[...])
    nq, nk = N_HEADS * HEAD_DIM, N_KV * HEAD_DIM
    bd = bd_ref[...]
    q = _qk_norm(qkv[:, :nq], qg_ref[...], bd)
    k = _qk_norm(qkv[:, nq:nq + nk], kg_ref[...], bd)
    v = qkv[:, nq + nk:]
    q_ref[...] = (q * (HEAD_DIM ** -0.5)).astype(BF16)
    k_ref[...] = k.astype(BF16)
    v_ref[...] = v.astype(BF16)
    kc_ref[...] = k
    vc_ref[...] = v


def _qkv_lat_kernel(x_ref, mod_ref, g_ref, w_ref, qg_ref, kg_ref, bd_ref,
                    cq_ref, sdq_ref, suq_ref, ck_ref, sdk_ref, suk_ref, q_ref, k_ref, v_ref):
    m = mod_ref[...]
    h = _norm_mod(x_ref[...], g_ref[...], _mod_slice(m, 0), _mod_slice(m, 1))
    qkv = _dot(h.astype(BF16), w_ref[...])
    nq, nk = N_HEADS * HEAD_DIM, N_KV * HEAD_DIM
    bd = bd_ref[...]
    q = _qk_norm(qkv[:, :nq], qg_ref[...], bd)
    k = _qk_norm(qkv[:, nq:nq + nk], kg_ref[...], bd)
    q = _rope(q, cq_ref[...], sdq_ref[...], suq_ref[...])
    k = _rope(k, ck_ref[...], sdk_ref[...], suk_ref[...])
    q_ref[...] = (q * (HEAD_DIM ** -0.5)).astype(BF16)
    k_ref[...] = k.astype(BF16)
    v_ref[...] = qkv[:, nq + nk:].astype(BF16)


def _rope_tables(seq, n_heads):
    n_rows = seq // GRID_W
    row = jnp.repeat(jnp.arange(n_rows), GRID_W).astype(F32)
    col = jnp.tile(jnp.arange(GRID_W), n_rows).astype(F32)
    n_freq = HEAD_DIM // 4
    inv = ROPE_BASE ** (-jnp.arange(n_freq, dtype=F32) / n_freq)
    ar, ac = row[:, None] * inv, col[:, None] * inv
    ang = jnp.concatenate([ar, ar, ac, ac], axis=-1)
    cos, sin = jnp.cos(ang), jnp.sin(ang)
    even = ((jnp.arange(HEAD_DIM) // n_freq) % 2 == 0).astype(F32)
    sin_dn = -sin * even
    sin_up = sin * (1.0 - even)
    return tuple(jnp.tile(a, (1, n_heads)) for a in (cos, sin_dn, sin_up))


def _qkv(st, x, mods, layer, g1, w_qkv, q_gain, k_gain, rope):
    nq, nk = N_HEADS * HEAD_DIM, N_KV * HEAD_DIM
    bd = jnp.asarray(np.kron(np.eye(4, dtype=np.float32), np.ones((HEAD_DIM, HEAD_DIM), np.float32))).astype(BF16)
    qg = jnp.tile(q_gain.reshape(1, HEAD_DIM), (1, N_HEADS))
    kg = jnp.tile(k_gain.reshape(1, HEAD_DIM), (1, N_KV))
    base_specs = [_tok_spec(), st.mod_spec(layer), _const_spec((1, D)), _const_spec((D, nq + 2 * nk)),
                  _const_spec((1, nq)), _const_spec((1, nk)), _const_spec((4 * HEAD_DIM, 4 * HEAD_DIM))]
    outs = [jax.ShapeDtypeStruct((st.n_tok, nq), BF16), jax.ShapeDtypeStruct((st.n_tok, nk), BF16),
            jax.ShapeDtypeStruct((st.n_tok, nk), BF16)]
    out_specs = [_tok_spec(nq), _tok_spec(nk), _tok_spec(nk)]
    if rope:
        tps = st.tiles_per_seq
        tq, tk = _rope_tables(st.seq, N_HEADS), _rope_tables(st.seq, N_KV)
        tab_specs = ([pl.BlockSpec((TOK_TILE, nq), lambda i: (i % tps, 0))] * 3
                     + [pl.BlockSpec((TOK_TILE, nk), lambda i: (i % tps, 0))] * 3)
        return pl.pallas_call(
            _qkv_lat_kernel, grid=(st.tiles,), in_specs=base_specs + tab_specs, out_specs=out_specs,
            out_shape=outs, compiler_params=_cparams(("arbitrary",), VMEM_LIMIT_BYTES), name="qkv_latent",
        )(x, mods, g1, w_qkv, qg, kg, bd, *tq, *tk)
    outs += [jax.ShapeDtypeStruct((st.n_tok, nk), F32)] * 2
    out_specs += [_tok_spec(nk)] * 2
    return pl.pallas_call(
        _qkv_ctx_kernel, grid=(st.tiles,), in_specs=base_specs, out_specs=out_specs,
        out_shape=outs, compiler_params=_cparams(("arbitrary",), VMEM_LIMIT_BYTES), name="qkv_context",
    )(x, mods, g1, w_qkv, qg, kg, bd)


def _softmax_av(s, v, sink):
    mx = jnp.maximum(jnp.max(s, axis=1, keepdims=True), sink)
    p = jnp.exp(s - mx)
    den = jnp.sum(p, axis=1, keepdims=True) + jnp.exp(sink - mx)
    return _dot(p.astype(BF16), v) / den


def _qk(q, k):
    return lax.dot_general(q, k, (((1,), (1,)), ((), ())), preferred_element_type=F32)


def _ctx_attn_kernel(sink_ref, q_ref, k_ref, v_ref, o_ref):
    q, k, v = q_ref[...], k_ref[...], v_ref[...]
    outs = []
    for h in range(N_HEADS):
        kv = h // GQA_G
        ks = slice(kv * HEAD_DIM, (kv + 1) * HEAD_DIM)
        s = _qk(q[:, h * HEAD_DIM:(h + 1) * HEAD_DIM], k[:, ks])
        outs.append(_softmax_av(s, v[:, ks], sink_ref[h]))
    o_ref[...] = jnp.concatenate(outs, axis=1).astype(BF16)


def _ctx_attn(st, q, k, v, sink):
    nq, nk = N_HEADS * HEAD_DIM, N_KV * HEAD_DIM
    seq = st.seq
    return pl.pallas_call(
        _ctx_attn_kernel,
        grid=(st.batch,),
        in_specs=[pl.BlockSpec(memory_space=pltpu.SMEM),
                  pl.BlockSpec((seq, nq), lambda b: (b, 0)),
                  pl.BlockSpec((seq, nk), lambda b: (b, 0)),
                  pl.BlockSpec((seq, nk), lambda b: (b, 0))],
        out_specs=pl.BlockSpec((seq, nq), lambda b: (b, 0)),
        out_shape=jax.ShapeDtypeStruct((st.n_tok, nq), BF16),
        compiler_params=_cparams(("arbitrary",), VMEM_LIMIT_BYTES),
        name="context_attention",
    )(sink, q, k, v)


def _lat_attn_kernel(sink_ref, q_ref, k_ref, v_ref, kc_ref, vc_ref, o_ref, *, n_blocks):
    j = pl.program_id(1)
    w = WINDOW
    jp = jnp.maximum(j - 1, 0)
    jn = jnp.minimum(j + 1, n_blocks - 1)

    def rows(ref, blk):
        return ref[pl.ds(pl.multiple_of(blk * w, w), w), :]

    kcat = jnp.concatenate([rows(k_ref, jp), rows(k_ref, j), rows(k_ref, jn), kc_ref[...].astype(BF16)], axis=0)
    vcat = jnp.concatenate([rows(v_ref, jp), rows(v_ref, j), rows(v_ref, jn), vc_ref[...].astype(BF16)], axis=0)
    n_keys = kcat.shape[0]
    qi = lax.broadcasted_iota(I32, (w, n_keys), 0)
    ki = lax.broadcasted_iota(I32, (w, n_keys), 1)
    ok = (((ki < w) & (j > 0) & (ki >= qi))
          | ((ki >= w) & (ki < 2 * w))
          | ((ki >= 2 * w) & (ki < 3 * w) & (j < n_blocks - 1) & (ki - 2 * w <= qi))
          | (ki >= 3 * w))
    q = q_ref[...]
    outs = []
    for h in range(N_HEADS):
        kv = h // GQA_G
        ks = slice(kv * HEAD_DIM, (kv + 1) * HEAD_DIM)
        s = _qk(q[:, h * HEAD_DIM:(h + 1) * HEAD_DIM], kcat[:, ks])
        s = jnp.where(ok, s, -jnp.inf)
        outs.append(_softmax_av(s, vcat[:, ks], sink_ref[h]))
    o_ref[...] = jnp.concatenate(outs, axis=1).astype(BF16)


def _lat_attn(st, q, k, v, k_ctx, v_ctx, sink):
    nq, nk = N_HEADS * HEAD_DIM, N_KV * HEAD_DIM
    seq, past = st.seq, k_ctx.shape[0] // st.batch
    n_blocks = seq // WINDOW
    return pl.pallas_call(
        functools.partial(_lat_attn_kernel, n_blocks=n_blocks),
        grid=(st.batch, n_blocks),
        in_specs=[pl.BlockSpec(memory_space=pltpu.SMEM),
                  pl.BlockSpec((WINDOW, nq), lambda b, j: (b * n_blocks + j, 0)),
                  pl.BlockSpec((seq, nk), lambda b, j: (b, 0)),
                  pl.BlockSpec((seq, nk), lambda b, j: (b, 0)),
                  pl.BlockSpec((past, nk), lambda b, j: (b, 0)),
                  pl.BlockSpec((past, nk), lambda b, j: (b, 0))],
        out_specs=pl.BlockSpec((WINDOW, nq), lambda b, j: (b * n_blocks + j, 0)),
        out_shape=jax.ShapeDtypeStruct((st.n_tok, nq), BF16),
        compiler_params=_cparams(("arbitrary", "arbitrary"), VMEM_LIMIT_BYTES),
        name="latent_attention",
    )(sink, q, k, v, k_ctx, v_ctx)


def _fourier_kernel(x_ref, mod_ref, g_ref, cs_ref, ss_ref, cc_ref, sc_ref, o_ref, *, scale):
    m = mod_ref[...]
    h = _norm_mod(x_ref[...], g_ref[...], _mod_slice(m, 0), _mod_slice(m, 1)).astype(BF16)
    p = _dot(cs_ref[...], h).astype(BF16)
    q = _dot(ss_ref[...], h).astype(BF16)
    gw = FOURIER_GW
    cc, sc = cc_ref[...], sc_ref[...]
    outs = []
    for g in range(FOURIER_GROUPS):
        sl = slice(g * gw, (g + 1) * gw)
        outs.append(_dot(p[:, sl], cc) - _dot(q[:, sl], sc))
    o_ref[...] = (jnp.concatenate(outs, axis=1) * scale).astype(BF16)


def _dft_tables(n):
    k = jnp.arange(n, dtype=I32)
    ang = ((k[:, None] * k[None, :]) % n).astype(F32) * (2.0 * math.pi / n)
    return jnp.cos(ang).astype(BF16), jnp.sin(ang).astype(BF16)


def _fourier(st, x, mods, layer, g1):
    seq = st.seq
    cs, ss = _dft_tables(seq)
    cc, sc = _dft_tables(FOURIER_GW)
    mod_spec = pl.BlockSpec((None, None, 1, N_MOD * D), lambda b: (layer, 0 if st.shared else 1 + b, 0, 0))
    return pl.pallas_call(
        functools.partial(_fourier_kernel, scale=1.0 / math.sqrt(seq * FOURIER_GW)),
        grid=(st.batch,),
        in_specs=[pl.BlockSpec((seq, D), lambda b: (b, 0)), mod_spec, _const_spec((1, D)),
                  _const_spec((seq, seq)), _const_spec((seq, seq)),
                  _const_spec((FOURIER_GW, FOURIER_GW)), _const_spec((FOURIER_GW, FOURIER_GW))],
        out_specs=pl.BlockSpec((seq, D), lambda b: (b, 0)),
        out_shape=jax.ShapeDtypeStruct((st.n_tok, D), BF16),
        compiler_params=_cparams(("arbitrary",), VMEM_LIMIT_BYTES),
        name="fourier",
    )(x, mods, g1, cs, ss, cc, sc)


def kernel(x_prompt, x_sample, state_rglru, cache_k, cache_v, c, c_ctx, mod_w, mod_b, norm1_g, norm2_g,
           rg_w_in, rg_conv_w, rg_conv_b, rg_w_a, rg_b_a, rg_w_x, rg_b_x, rg_lambda, rg_w_out,
           at_w_qkv, at_q_norm, at_k_norm, at_sink, at_w_o, ft_w, moe_router, moe_w_gate, moe_w_up, moe_w_down):
    depth = mod_w.shape[0]
    batch, seq, _ = x_prompt.shape
    dec_batch, dec_seq, _ = x_sample.shape
    assert 1 + dec_batch <= MOD_ROWS
    streams = (_Stream(batch, seq, True), _Stream(dec_batch, dec_seq, False))
    cond = jnp.concatenate([c_ctx[None, :], c, jnp.zeros((MOD_ROWS - 1 - dec_batch, D), F32)], axis=0)
    mods = _modulation(cond, mod_w, mod_b).reshape(depth, MOD_ROWS, 1, N_MOD * D)

    xs = [x_prompt.reshape(batch * seq, D), x_sample.reshape(dec_batch * dec_seq, D)]
    new_rg, new_k, new_v = [], [], []
    n_mixers = 3
    for layer in range(depth):
        kind, j = layer % n_mixers, layer // n_mixers
        g1 = norm1_g[layer].reshape(1, D)
        g2 = norm2_g[layer].reshape(1, D)
        wr = _router_pieces(moe_router[layer])
        for si, st in enumerate(streams):
            x = xs[si]
            if kind == 0:
                gate, u = _rg_in(st, x, mods, layer, g1, rg_w_in[j].astype(BF16))
                if st.shared:
                    h0 = jnp.zeros((2, st.batch, D), F32)
                else:
                    h0 = jnp.transpose(state_rglru[:, j], (1, 0, 2))
                h, fin = _rg_scan(st, u, h0, rg_conv_w[j], rg_conv_b[j], rg_w_a[j], rg_b_a[j],
                                  rg_w_x[j], rg_b_x[j], rg_lambda[j])
                if st.shared:
                    new_rg.append(jnp.transpose(fin, (1, 0, 2)))
                x1, xpk, aff = _lin_out(st, (h, gate), rg_w_out[j].astype(BF16), x, mods, layer, g2, wr, True)
            elif kind == 1:
                w_qkv = at_w_qkv[j].astype(BF16)
                if st.shared:
                    q, k, v, kc, vc = _qkv(st, x, mods, layer, g1, w_qkv, at_q_norm[j], at_k_norm[j], False)
                    new_k.append(kc.reshape(st.batch, st.seq, N_KV, HEAD_DIM))
                    new_v.append(vc.reshape(st.batch, st.seq, N_KV, HEAD_DIM))
                    a = _ctx_attn(st, q, k, v, at_sink[j])
                else:
                    q, k, v = _qkv(st, x, mods, layer, g1, w_qkv, at_q_norm[j], at_k_norm[j], True)
                    nk = N_KV * HEAD_DIM
                    a = _lat_attn(st, q, k, v, cache_k[:, j].reshape(-1, nk), cache_v[:, j].reshape(-1, nk),
                                  at_sink[j])
                x1, xpk, aff = _lin_out(st, (a,), at_w_o[j].astype(BF16), x, mods, layer, g2, wr, False)
            else:
                a = _fourier(st, x, mods, layer, g1)
                x1, xpk, aff = _lin_out(st, (a,), ft_w[j].astype(BF16), x, mods, layer, g2, wr, False)
            xs[si] = _moe(st, x1, xpk, aff, mods, layer, moe_w_gate[layer], moe_w_up[layer], moe_w_down[layer])
    return (xs[0].reshape(batch, seq, D), xs[1].reshape(dec_batch, dec_seq, D),
            jnp.stack(new_rg, axis=1), jnp.stack(new_k, axis=1), jnp.stack(new_v, axis=1))
```

```python
import functools
import math

import numpy as np
import jax
import jax.numpy as jnp
from jax import lax
from jax.experimental import pallas as pl
from jax.experimental.pallas import tpu as pltpu

F32 = jnp.float32
BF16 = jnp.bfloat16
I32 = jnp.int32
U32 = jnp.uint32

D = 1024
N_MOD = 6
EPS = 1e-6
GRID_W = 64
CONV_W = 4
CONV_LEFT = 2
LRU_C = 8.0
RNN_BLOCKS = 16
RNN_BLOCK = D // RNN_BLOCKS
N_HEADS = 16
N_KV = 4
HEAD_DIM = 64
GQA_G = N_HEADS // N_KV
WINDOW = 128
ROPE_BASE = 10000.0
FOURIER_GROUPS = 4
FOURIER_GW = D // FOURIER_GROUPS
N_EXPERTS = 16
EC_FACTOR = 2

LANES = 128
SUBLANES = 8
VMEM_LIMIT_BYTES = 56 * 1024 * 1024

TOK_TILE = 256
MOD_ROWS = 16
SCAN_CC = 128
SCAN_TC = 32
FF_CHUNK = 512


def _cparams(sem, vmem=None):
    return pltpu.CompilerParams(dimension_semantics=sem, vmem_limit_bytes=vmem)


def _split2(a):
    hi = a.astype(BF16)
    lo = (a - hi.astype(F32)).astype(BF16)
    return hi, lo


def _dot(a, b):
    return jnp.dot(a, b, preferred_element_type=F32)


def _dot3(a, b):
    a_hi, a_lo = _split2(a)
    b_hi, b_lo = _split2(b)
    return _dot(a_hi, b_hi) + _dot(a_hi, b_lo) + _dot(a_lo, b_hi)


def _norm_mod(x, g, shift, scale):
    ms = jnp.mean(x * x, axis=-1, keepdims=True)
    y = x * lax.rsqrt(ms + EPS) * g
    return y * (1.0 + scale) + shift


def _mod_slice(m, k):
    return m[:, k * D:(k + 1) * D]


class _Stream:
    def __init__(self, batch, seq, shared_cond):
        self.batch, self.seq, self.shared = batch, seq, shared_cond
        self.n_tok = batch * seq
        self.tiles = self.n_tok // TOK_TILE
        self.tiles_per_seq = seq // TOK_TILE

    def mod_row(self, i):
        return 0 if self.shared else 1 + i // self.tiles_per_seq

    def mod_spec(self, layer):
        return pl.BlockSpec((None, None, 1, N_MOD * D), lambda i: (layer, self.mod_row(i), 0, 0))

    def seq_major_spec(self):
        tps = self.tiles_per_seq
        return pl.BlockSpec((TOK_TILE, D), lambda i: (i % tps, i // tps))


def _tok_spec(width=D):
    return pl.BlockSpec((TOK_TILE, width), lambda i: (i, 0))


def _const_spec(shape):
    nd = len(shape)
    return pl.BlockSpec(shape, lambda i: (0,) * nd)


def _mod_kernel(c_ref, w_ref, b_ref, o_ref):
    c = c_ref[...]
    c = c * jax.nn.sigmoid(c)
    o_ref[...] = _dot3(c, w_ref[...]) + b_ref[...]


def _modulation(cond, mod_w, mod_b):
    depth = mod_w.shape[0]
    tn = N_MOD * D // 4
    return pl.pallas_call(
        _mod_kernel,
        grid=(depth, N_MOD * D // tn),
        in_specs=[pl.BlockSpec((MOD_ROWS, D), lambda l, n: (0, 0)),
                  pl.BlockSpec((None, D, tn), lambda l, n: (l, 0, n)),
                  pl.BlockSpec((None, 1, tn), lambda l, n: (l, 0, n))],
        out_specs=pl.BlockSpec((None, MOD_ROWS, tn), lambda l, n: (l, 0, n)),
        out_shape=jax.ShapeDtypeStruct((depth, MOD_ROWS, N_MOD * D), F32),
        compiler_params=_cparams(("arbitrary", "arbitrary"), VMEM_LIMIT_BYTES),
        name="modulation",
    )(cond, mod_w, mod_b.reshape(depth, 1, N_MOD * D))


def _rg_in_kernel(x_ref, mod_ref, g_ref, w_ref, gate_ref, u_ref):
    m = mod_ref[...]
    h = _norm_mod(x_ref[...], g_ref[...], _mod_slice(m, 0), _mod_slice(m, 1))
    gu = _dot(h.astype(BF16), w_ref[...])
    gate_ref[...] = gu[:, :D]
    u_ref[...] = gu[:, D:]


def _rg_in(st, x, mods, layer, g1, w_in):
    out = jax.ShapeDtypeStruct((st.seq, st.batch * D), F32)
    return pl.pallas_call(
        _rg_in_kernel,
        grid=(st.tiles,),
        in_specs=[_tok_spec(), st.mod_spec(layer), _const_spec((1, D)), _const_spec((D, 2 * D))],
        out_specs=[st.seq_major_spec(), st.seq_major_spec()],
        out_shape=[out, out],
        compiler_params=_cparams(("arbitrary",), VMEM_LIMIT_BYTES),
        name="rg_in",
    )(x, mods, g1, w_in)


def _softplus(z):
    return jnp.maximum(z, 0.0) + jnp.log1p(jnp.exp(-jnp.abs(z)))


def _rg_scan_kernel(u_ref, h0_ref, cw_ref, cb_ref, wg_ref, bg_ref, lam_ref, h_ref, fin_ref,
                    upad, a_f, b_f, a_b, b_b, *, seq):
    cc = u_ref.shape[-1]
    pad_hi = CONV_W - 1 - CONV_LEFT
    upad[0:CONV_LEFT] = jnp.zeros((CONV_LEFT, SUBLANES, cc), F32)
    upad[CONV_LEFT:CONV_LEFT + seq] = u_ref[...]
    upad[CONV_LEFT + seq:CONV_LEFT + seq + pad_hi] = jnp.zeros((pad_hi, SUBLANES, cc), F32)
    sp = _softplus(-lam_ref[...])
    cw = cw_ref[...]
    rows = SCAN_TC * SUBLANES

    def coef(c, carry):
        t0 = pl.multiple_of(c * SCAN_TC, SCAN_TC)
        uc = cb_ref[...] + cw[0:1] * upad[pl.ds(t0, SCAN_TC)]
        for k in range(1, CONV_W):
            uc = uc + cw[k:k + 1] * upad[pl.ds(t0 + k, SCAN_TC)]
        u2 = uc.reshape(rows, cc)
        gts = jax.nn.sigmoid(_dot(u2.astype(BF16), wg_ref[...]) + bg_ref[...])
        for d, (a_s, b_s) in enumerate(((a_f, b_f), (a_b, b_b))):
            r = gts[:, (2 * d) * cc:(2 * d + 1) * cc]
            i = gts[:, (2 * d + 1) * cc:(2 * d + 2) * cc]
            log_a = (-LRU_C) * r * sp[d]
            th = jnp.tanh(log_a)
            one_minus_a2 = (-2.0) * th / (1.0 - th)
            a_s[pl.ds(t0, SCAN_TC)] = jnp.exp(log_a).reshape(SCAN_TC, SUBLANES, cc)
            b_s[pl.ds(t0, SCAN_TC)] = (jnp.sqrt(one_minus_a2) * (i * u2)).reshape(SCAN_TC, SUBLANES, cc)
        return carry

    lax.fori_loop(0, seq // SCAN_TC, coef, 0)

    def step(t, carry):
        hf, hb = carry
        hf = a_f[t] * hf + b_f[t]
        b_f[t] = hf
        tb = seq - 1 - t
        hb = a_b[tb] * hb + b_b[tb]
        b_b[tb] = hb
        return hf, hb

    hf, hb = lax.fori_loop(0, seq, step, (h0_ref[0], h0_ref[1]), unroll=8)
    fin_ref[0] = hf
    fin_ref[1] = hb
    h_ref[...] = b_f[...] + b_b[...]


def _blockdiag_pairs(w):
    per = SCAN_CC // RNN_BLOCK
    w4 = w.reshape(D // SCAN_CC, per, RNN_BLOCK, RNN_BLOCK)
    eye = jnp.eye(per, dtype=w.dtype)
    return jnp.einsum('cipq,ij->cipjq', w4, eye).reshape(D // SCAN_CC, SCAN_CC, SCAN_CC)


def _rg_scan(st, u, h0, conv_w, conv_b, w_a, b_a, w_x, b_x, lam):
    seq, batch = st.seq, st.batch
    n_cc = D // SCAN_CC
    wg = jnp.concatenate([_blockdiag_pairs(w_a[0]), _blockdiag_pairs(w_x[0]),
                          _blockdiag_pairs(w_a[1]), _blockdiag_pairs(w_x[1])], axis=-1).astype(BF16)
    bg = jnp.concatenate([b.reshape(n_cc, 1, SCAN_CC) for b in (b_a[0], b_x[0], b_a[1], b_x[1])], axis=-1)
    blk = (seq, SUBLANES, SCAN_CC)
    scr = pltpu.VMEM(blk, F32)
    h, fin = pl.pallas_call(
        functools.partial(_rg_scan_kernel, seq=seq),
        grid=(batch // SUBLANES, n_cc),
        in_specs=[pl.BlockSpec(blk, lambda b, c: (0, b, c)),
                  pl.BlockSpec((2, SUBLANES, SCAN_CC), lambda b, c: (0, b, c)),
                  pl.BlockSpec((CONV_W, 1, SCAN_CC), lambda b, c: (0, 0, c)),
                  pl.BlockSpec((1, 1, SCAN_CC), lambda b, c: (0, 0, c)),
                  pl.BlockSpec((None, SCAN_CC, 4 * SCAN_CC), lambda b, c: (c, 0, 0)),
                  pl.BlockSpec((None, 1, 4 * SCAN_CC), lambda b, c: (c, 0, 0)),
                  pl.BlockSpec((2, 1, SCAN_CC), lambda b, c: (0, 0, c))],
        out_specs=[pl.BlockSpec(blk, lambda b, c: (0, b, c)),
                   pl.BlockSpec((2, SUBLANES, SCAN_CC), lambda b, c: (0, b, c))],
        out_shape=[jax.ShapeDtypeStruct((seq, batch, D), F32),
                   jax.ShapeDtypeStruct((2, batch, D), F32)],
        scratch_shapes=[pltpu.VMEM((seq + CONV_W - 1, SUBLANES, SCAN_CC), F32), scr, scr, scr, scr],
        compiler_params=_cparams(("arbitrary", "arbitrary"), VMEM_LIMIT_BYTES),
        name="rg_scan",
    )(u.reshape(seq, batch, D), h0, conv_w.reshape(CONV_W, 1, D), conv_b.reshape(1, 1, D),
      wg, bg, lam.reshape(2, 1, D))
    return h.reshape(seq, batch * D), fin


def _route_and_pack(x, m, g2_ref, wr_ref, xo_ref, xnb_ref, aff_ref):
    xo_ref[...] = x
    xn = _norm_mod(x, g2_ref[...], _mod_slice(m, 3), _mod_slice(m, 4))
    x_hi = xn.astype(BF16)
    xnb_ref[...] = x_hi
    x_lo = (xn - x_hi.astype(F32)).astype(BF16)
    wr = wr_ref[...]
    l1 = _dot(x_hi, wr)
    l2 = _dot(x_lo, wr)
    e = N_EXPERTS
    logit = (l1 + pltpu.roll(l1, LANES - e, 1) + pltpu.roll(l1, LANES - 2 * e, 1)
             + l2 + pltpu.roll(l2, LANES - e, 1))
    lane = lax.broadcasted_iota(I32, logit.shape, 1)
    logit = jnp.where(lane < e, logit, -jnp.inf)
    mx = jnp.max(logit, axis=1, keepdims=True)
    ex = jnp.exp(logit - mx)
    aff = ex / jnp.sum(ex, axis=1, keepdims=True)
    aff_t = aff.T
    for k in range(TOK_TILE // LANES):
        aff_ref[k] = aff_t[0:e, k * LANES:(k + 1) * LANES]


def _lin_out_kernel(a_ref, w_ref, x_ref, mod_ref, g2_ref, wr_ref, xo_ref, xnb_ref, aff_ref):
    m = mod_ref[...]
    y = _dot(a_ref[...], w_ref[...])
    x = x_ref[...] + _mod_slice(m, 2) * y
    _route_and_pack(x, m, g2_ref, wr_ref, xo_ref, xnb_ref, aff_ref)


def _lin_out_gated_kernel(h_ref, gate_ref, w_ref, x_ref, mod_ref, g2_ref, wr_ref, xo_ref, xnb_ref, aff_ref):
    m = mod_ref[...]
    a = (h_ref[...] * jax.nn.gelu(gate_ref[...])).astype(BF16)
    y = _dot(a, w_ref[...])
    x = x_ref[...] + _mod_slice(m, 2) * y
    _route_and_pack(x, m, g2_ref, wr_ref, xo_ref, xnb_ref, aff_ref)


def _router_pieces(w_router):
    hi = w_router.astype(BF16)
    r1 = w_router - hi.astype(F32)
    mid = r1.astype(BF16)
    lo = (r1 - mid.astype(F32)).astype(BF16)
    pad = jnp.zeros((D, LANES - 3 * N_EXPERTS), BF16)
    return jnp.concatenate([hi, mid, lo, pad], axis=1)


def _lin_out(st, srcs, w, x, mods, layer, g2, wr, gated):
    n_chunks = st.n_tok // LANES
    if gated:
        body, src_specs = _lin_out_gated_kernel, [st.seq_major_spec(), st.seq_major_spec()]
    else:
        body, src_specs = _lin_out_kernel, [_tok_spec()]
    return pl.pallas_call(
        body,
        grid=(st.tiles,),
        in_specs=src_specs + [_const_spec((D, D)), _tok_spec(), st.mod_spec(layer), _const_spec((1, D)),
                              _const_spec((D, LANES))],
        out_specs=[_tok_spec(), _tok_spec(),
                   pl.BlockSpec((TOK_TILE // LANES, N_EXPERTS, LANES), lambda i: (i, 0, 0))],
        out_shape=[jax.ShapeDtypeStruct((st.n_tok, D), F32),
                   jax.ShapeDtypeStruct((st.n_tok, D), BF16),
                   jax.ShapeDtypeStruct((n_chunks, N_EXPERTS, LANES), F32)],
        compiler_params=_cparams(("arbitrary",), VMEM_LIMIT_BYTES),
        name="lin_out_gated" if gated else "lin_out",
    )(*srcs, w, x, mods, g2, wr)


def _select_kernel(a_ref, idx_ref, g_ref, linc_s, cnt_s, crow_s, *, cap):
    n_e, n_ch, _ = a_ref.shape
    assert n_ch & (n_ch - 1) == 0 and n_ch <= LANES
    rows = n_e * n_ch
    bits = pltpu.bitcast(a_ref[...], I32)
    capf = jnp.float32(cap)

    def count(mask3):
        c = jnp.sum(mask3.astype(F32), axis=2, keepdims=True)
        return jnp.sum(c, axis=1, keepdims=True)

    def search(i, thr):
        cand = thr | (jnp.int32(1) << (30 - i))
        return jnp.where(count(bits >= cand) >= capf, cand, thr)

    thr = lax.fori_loop(0, 31, search, jnp.zeros((n_e, 1, 1), I32))
    gt3 = bits > thr
    eq3 = bits == thr
    need = capf - count(gt3)

    li = lax.broadcasted_iota(I32, (LANES, LANES), 0)
    lj = lax.broadcasted_iota(I32, (LANES, LANES), 1)
    upper = (li <= lj).astype(BF16)
    ones = jnp.ones((LANES, LANES), BF16)
    ri = lax.broadcasted_iota(I32, (rows, rows), 0)
    rj = lax.broadcasted_iota(I32, (rows, rows), 1)
    sh = n_ch.bit_length() - 1
    before = (((ri >> sh) == (rj >> sh)) & (rj < ri)).astype(BF16)

    def prefixes(mask3):
        x = mask3.reshape(rows, LANES).astype(BF16)
        local = _dot(x, upper)
        tot = _dot(x, ones)
        return local, tot, _dot(before, tot.astype(BF16))

    gt_l, gt_t, gt_x = prefixes(gt3)
    eq_l, eq_t, eq_x = prefixes(eq3)
    need_r = jnp.broadcast_to(need, (n_e, n_ch, LANES)).reshape(rows, LANES)
    sel_incl = gt_x + gt_l + jnp.minimum(eq_x + eq_l, need_r)
    sel_x = gt_x + jnp.minimum(eq_x, need_r)
    sel_c = gt_x + gt_t + jnp.minimum(eq_x + eq_t, need_r)
    linc_s[...] = sel_incl - sel_x
    cnt_s[...] = sel_c - sel_x
    c3 = sel_c.reshape(n_e, n_ch, LANES)
    pick = lax.broadcasted_iota(I32, (1, n_ch, LANES), 1) == lax.broadcasted_iota(I32, (1, n_ch, LANES), 2)
    crow = jnp.sum(jnp.where(pick, c3, 0.0), axis=1)
    crow = jnp.where(lax.broadcasted_iota(I32, crow.shape, 1) < n_ch, crow, jnp.float32(2 * cap + n_ch * LANES))
    crow_s[...] = crow

    slot = lax.broadcasted_iota(I32, (cap, LANES), 0).astype(F32)
    lane = lax.broadcasted_iota(I32, (cap, LANES), 1).astype(F32)
    diag = lax.broadcasted_iota(I32, (LANES, LANES), 0) == lax.broadcasted_iota(I32, (LANES, LANES), 1)
    zpad = jnp.zeros((LANES - n_ch, LANES), BF16)

    def per_expert(e, carry):
        r0 = pl.multiple_of(e * n_ch, n_ch)
        passed = (crow_s[pl.ds(e, 1), :] <= slot).astype(BF16)
        cnt_e = jnp.concatenate([cnt_s[pl.ds(r0, n_ch), :].astype(BF16), zpad], axis=0)
        linc_e = jnp.concatenate([linc_s[pl.ds(r0, n_ch), :].astype(BF16), zpad], axis=0)
        chunk = _dot(passed, ones)
        rank = slot - _dot(passed, cnt_e)
        onehot = (chunk == lane).astype(BF16)
        g = _dot(onehot, linc_e)
        within = _dot((g <= rank).astype(BF16), ones)
        tok = chunk * jnp.float32(LANES) + within
        a_e = a_ref[e]
        a_hi = a_e.astype(BF16)
        r1 = a_e - a_hi.astype(F32)
        a_mid = r1.astype(BF16)
        a_lo = (r1 - a_mid.astype(F32)).astype(BF16)
        arow = (_dot(onehot, jnp.concatenate([a_hi, zpad], axis=0))
                + _dot(onehot, jnp.concatenate([a_mid, zpad], axis=0))
                + _dot(onehot, jnp.concatenate([a_lo, zpad], axis=0)))
        gate = jnp.sum(jnp.where(lane == within, arow, 0.0), axis=1, keepdims=True)
        gate = jnp.broadcast_to(gate, (cap, LANES))

        def to_row(col):
            pieces = []
            for b in range(cap // LANES):
                blk = col[b * LANES:(b + 1) * LANES, :]
                pieces.append(jnp.sum(jnp.where(diag, blk, 0.0), axis=0, keepdims=True))
            return jnp.concatenate(pieces, axis=1)

        idx_ref[pl.ds(e, 1), :] = to_row(tok).astype(I32)
        g_ref[pl.ds(e, 1), :] = to_row(gate)
        return carry

    lax.fori_loop(0, n_e, per_expert, 0)


def _select(aff_chunks, cap):
    n_ch = aff_chunks.shape[0]
    a = jnp.transpose(aff_chunks, (1, 0, 2))
    rows = N_EXPERTS * n_ch
    return pl.pallas_call(
        functools.partial(_select_kernel, cap=cap),
        out_shape=[jax.ShapeDtypeStruct((N_EXPERTS, cap), I32),
                   jax.ShapeDtypeStruct((N_EXPERTS, cap), F32)],
        scratch_shapes=[pltpu.VMEM((rows, LANES), F32), pltpu.VMEM((rows, LANES), F32),
                        pltpu.VMEM((N_EXPERTS, LANES), F32)],
        compiler_params=pltpu.CompilerParams(vmem_limit_bytes=VMEM_LIMIT_BYTES),
        name="select",
    )(a)


ROW_CH = D // LANES
GATHER_UNROLL = 8


def _cm_stride(m):
    return m + SUBLANES


def _moe_ffn_kernel(idx_ref, src_ref, g_ref, wg_ref, wu_ref, wd_ref, o_ref, tile_s, xe_s, *, cap):
    e = pl.program_id(0)
    f = pl.program_id(1)
    stride = _cm_stride(cap)

    @pl.when(f == 0)
    def _gather():
        def body(gi, carry):
            base = pl.multiple_of(gi * GATHER_UNROLL, GATHER_UNROLL)
            for k in range(GATHER_UNROLL):
                n = idx_ref[e * cap + base + k]
                r = pl.multiple_of((n >> 1) * (2 * ROW_CH), 2 * ROW_CH)
                pair = src_ref[pl.ds(r, 2 * ROW_CH), :].astype(F32)
                slab = jnp.where((n & 1) == 1, pair[ROW_CH:], pair[:ROW_CH])
                tile_s[pl.ds(base + k, ROW_CH, stride=stride), :] = slab
            return carry

        lax.fori_loop(0, cap // GATHER_UNROLL, body, 0)
        for j in range(ROW_CH):
            xe_s[:, j * LANES:(j + 1) * LANES] = tile_s[pl.ds(j * stride, cap), :].astype(BF16)
            o_ref[pl.ds(j * stride + cap, stride - cap), :] = jnp.zeros((stride - cap, LANES), F32)

    xe = xe_s[...]
    hg = _dot(xe, wg_ref[...].astype(BF16))
    hu = _dot(xe, wu_ref[...].astype(BF16))
    h = (hg * jax.nn.sigmoid(hg) * hu).astype(BF16)
    y = _dot(h, wd_ref[...].astype(BF16))
    last = pl.num_programs(1) - 1

    chunks = [(pl.ds(j * stride, cap), slice(j * LANES, (j + 1) * LANES)) for j in range(ROW_CH)]

    @pl.when(f == 0)
    def _first():
        for rows, cols in chunks:
            o_ref[rows, :] = y[:, cols]

    @pl.when(f > 0)
    def _rest():
        for rows, cols in chunks:
            o_ref[rows, :] += y[:, cols]

    @pl.when(f == last)
    def _scale():
        for rows, _ in chunks:
            o_ref[rows, :] = o_ref[rows, :] * g_ref[...]


def _moe_ffn(idx, gates, xnb, w_gate, w_up, w_down, layer, cap):
    n_tok = xnb.shape[0]
    ff = w_gate.shape[-1]
    stride = _cm_stride(cap)
    src = xnb.reshape(n_tok * ROW_CH, LANES)
    grid_spec = pltpu.PrefetchScalarGridSpec(
        num_scalar_prefetch=1,
        grid=(N_EXPERTS, ff // FF_CHUNK),
        in_specs=[pl.BlockSpec((n_tok * ROW_CH, LANES), lambda e, f, idx: (0, 0), pipeline_mode=pl.Buffered(1)),
                  pl.BlockSpec((None, cap, 1), lambda e, f, idx: (e, 0, 0)),
                  pl.BlockSpec((None, None, D, FF_CHUNK), lambda e, f, idx: (layer, e, 0, f)),
                  pl.BlockSpec((None, None, D, FF_CHUNK), lambda e, f, idx: (layer, e, 0, f)),
                  pl.BlockSpec((None, None, FF_CHUNK, D), lambda e, f, idx: (layer, e, f, 0))],
        out_specs=pl.BlockSpec((None, ROW_CH * stride, LANES), lambda e, f, idx: (e, 0, 0)),
        scratch_shapes=[pltpu.VMEM((ROW_CH * stride, LANES), F32), pltpu.VMEM((cap, D), BF16)],
    )
    return pl.pallas_call(
        functools.partial(_moe_ffn_kernel, cap=cap),
        grid_spec=grid_spec,
        out_shape=jax.ShapeDtypeStruct((N_EXPERTS, ROW_CH * stride, LANES), F32),
        compiler_params=_cparams(("arbitrary", "arbitrary"), VMEM_LIMIT_BYTES),
        name="moe_ffn",
    )(idx, src, gates.reshape(N_EXPERTS, cap, 1), w_gate, w_up, w_down)


def _combine_kernel(idx_ref, ye_ref, o_ref, *, cap):
    e = pl.program_id(0)
    stride = _cm_stride(cap)

    @pl.when(e == 0)
    def _zero():
        o_ref[...] = jnp.zeros(o_ref.shape, F32)

    def body(gi, carry):
        base = pl.multiple_of(gi * GATHER_UNROLL, GATHER_UNROLL)
        rows, sums = [], []
        for k in range(GATHER_UNROLL):
            r = pl.multiple_of(idx_ref[e * cap + base + k] * ROW_CH, ROW_CH)
            rows.append(r)
            sums.append(o_ref[pl.ds(r, ROW_CH), :] + ye_ref[pl.ds(base + k, ROW_CH, stride=stride), :])
        for r, s in zip(rows, sums):
            o_ref[pl.ds(r, ROW_CH), :] = s
        return carry

    lax.fori_loop(0, cap // GATHER_UNROLL, body, 0)


def _combine(idx, ye, n_tok, cap):
    stride = _cm_stride(cap)
    grid_spec = pltpu.PrefetchScalarGridSpec(
        num_scalar_prefetch=1,
        grid=(N_EXPERTS,),
        in_specs=[pl.BlockSpec((None, ROW_CH * stride, LANES), lambda e, idx: (e, 0, 0))],
        out_specs=pl.BlockSpec((n_tok * ROW_CH, LANES), lambda e, idx: (0, 0), pipeline_mode=pl.Buffered(1)),
    )
    return pl.pallas_call(
        functools.partial(_combine_kernel, cap=cap),
        grid_spec=grid_spec,
        out_shape=jax.ShapeDtypeStruct((n_tok * ROW_CH, LANES), F32),
        compiler_params=_cparams(("arbitrary",), VMEM_LIMIT_BYTES),
        name="moe_combine",
    )(idx, ye)


def _residual_kernel(x_ref, y_ref, mod_ref, o_ref):
    y = jnp.concatenate([y_ref[pl.ds(j, TOK_TILE, stride=ROW_CH), :] for j in range(ROW_CH)], axis=1)
    o_ref[...] = x_ref[...] + _mod_slice(mod_ref[...], 5) * y


def _residual(st, x, y, mods, layer):
    return pl.pallas_call(
        _residual_kernel,
        grid=(st.tiles,),
        in_specs=[_tok_spec(), pl.BlockSpec((TOK_TILE * ROW_CH, LANES), lambda i: (i, 0)), st.mod_spec(layer)],
        out_specs=_tok_spec(),
        out_shape=jax.ShapeDtypeStruct((st.n_tok, D), F32),
        compiler_params=_cparams(("arbitrary",), VMEM_LIMIT_BYTES),
        name="moe_residual",
    )(x, y, mods)


def _moe(st, x, xnb, aff_chunks, mods, layer, w_gate, w_up, w_down):
    cap = EC_FACTOR * st.n_tok // N_EXPERTS
    idx, gates = _select(aff_chunks, cap)
    idx = idx.reshape(N_EXPERTS * cap)
    ye = _moe_ffn(idx, gates, xnb, w_gate, w_up, w_down, layer, cap)
    y = _combine(idx, ye, st.n_tok, cap)
    return _residual(st, x, y, mods, layer)


def _head_sumsq(x, bd):
    x2 = x * x
    hi, lo = _split2(x2)
    w = bd.shape[0]
    cols = []
    for c in range(x.shape[1] // w):
        sl = slice(c * w, (c + 1) * w)
        cols.append(_dot(hi[:, sl], bd) + _dot(lo[:, sl], bd))
    return cols[0] if len(cols) == 1 else jnp.concatenate(cols, axis=1)


def _qk_norm(x, gain, bd):
    ms = _head_sumsq(x, bd) * (1.0 / HEAD_DIM)
    return x * lax.rsqrt(ms + EPS) * gain


def _rope(x, cos, sin_dn, sin_up):
    n = x.shape[1]
    q = HEAD_DIM // 4
    return x * cos + pltpu.roll(x, n - q, 1) * sin_dn + pltpu.roll(x, q, 1) * sin_up


def _qkv_ctx_kernel(x_ref, mod_ref, g_ref, w_ref, qg_ref, kg_ref, bd_ref, q_ref, k_ref, v_ref, kc_ref, vc_ref):
    m = mod_ref[...]
    h = _norm_mod(x_ref[...], g_ref[...], _mod_slice(m, 0), _mod_slice(m, 1))
    qkv = _dot(h.astype(BF16), w_ref[...])
    nq, nk = N_HEADS * HEAD_DIM, N_KV * HEAD_DIM
    bd = bd_ref[...]
    q = _qk_norm(qkv[:, :nq], qg_ref[...], bd)
    k = _qk_norm(qkv[:, nq:nq + nk], kg_ref[...], bd)
    v = qkv[:, nq + nk:]
    q_ref[...] = (q * (HEAD_DIM ** -0.5)).astype(BF16)
    k_ref[...] = k.astype(BF16)
    v_ref[...] = v.astype(BF16)
    kc_ref[...] = k
    vc_ref[...] = v


def _qkv_lat_kernel(x_ref, mod_ref, g_ref, w_ref, qg_ref, kg_ref, bd_ref,
                    cq_ref, sdq_ref, suq_ref, ck_ref, sdk_ref, suk_ref, q_ref, k_ref, v_ref):
    m = mod_ref[...]
    h = _norm_mod(x_ref[...], g_ref[...], _mod_slice(m, 0), _mod_slice(m, 1))
    qkv = _dot(h.astype(BF16), w_ref[...])
    nq, nk = N_HEADS * HEAD_DIM, N_KV * HEAD_DIM
    bd = bd_ref[...]
    q = _qk_norm(qkv[:, :nq], qg_ref[...], bd)
    k = _qk_norm(qkv[:, nq:nq + nk], kg_ref[...], bd)
    q = _rope(q, cq_ref[...], sdq_ref[...], suq_ref[...])
    k = _rope(k, ck_ref[...], sdk_ref[...], suk_ref[...])
    q_ref[...] = (q * (HEAD_DIM ** -0.5)).astype(BF16)
    k_ref[...] = k.astype(BF16)
    v_ref[...] = qkv[:, nq + nk:].astype(BF16)


def _rope_tables(seq, n_heads):
    n_rows = seq // GRID_W
    row = jnp.repeat(jnp.arange(n_rows), GRID_W).astype(F32)
    col = jnp.tile(jnp.arange(GRID_W), n_rows).astype(F32)
    n_freq = HEAD_DIM // 4
    inv = ROPE_BASE ** (-jnp.arange(n_freq, dtype=F32) / n_freq)
    ar, ac = row[:, None] * inv, col[:, None] * inv
    ang = jnp.concatenate([ar, ar, ac, ac], axis=-1)
    cos, sin = jnp.cos(ang), jnp.sin(ang)
    even = ((jnp.arange(HEAD_DIM) // n_freq) % 2 == 0).astype(F32)
    sin_dn = -sin * even
    sin_up = sin * (1.0 - even)
    return tuple(jnp.tile(a, (1, n_heads)) for a in (cos, sin_dn, sin_up))


def _qkv(st, x, mods, layer, g1, w_qkv, q_gain, k_gain, rope):
    nq, nk = N_HEADS * HEAD_DIM, N_KV * HEAD_DIM
    bd = jnp.asarray(np.kron(np.eye(4, dtype=np.float32), np.ones((HEAD_DIM, HEAD_DIM), np.float32))).astype(BF16)
    qg = jnp.tile(q_gain.reshape(1, HEAD_DIM), (1, N_HEADS))
    kg = jnp.tile(k_gain.reshape(1, HEAD_DIM), (1, N_KV))
    base_specs = [_tok_spec(), st.mod_spec(layer), _const_spec((1, D)), _const_spec((D, nq + 2 * nk)),
                  _const_spec((1, nq)), _const_spec((1, nk)), _const_spec((4 * HEAD_DIM, 4 * HEAD_DIM))]
    outs = [jax.ShapeDtypeStruct((st.n_tok, nq), BF16), jax.ShapeDtypeStruct((st.n_tok, nk), BF16),
            jax.ShapeDtypeStruct((st.n_tok, nk), BF16)]
    out_specs = [_tok_spec(nq), _tok_spec(nk), _tok_spec(nk)]
    if rope:
        tps = st.tiles_per_seq
        tq, tk = _rope_tables(st.seq, N_HEADS), _rope_tables(st.seq, N_KV)
        tab_specs = ([pl.BlockSpec((TOK_TILE, nq), lambda i: (i % tps, 0))] * 3
                     + [pl.BlockSpec((TOK_TILE, nk), lambda i: (i % tps, 0))] * 3)
        return pl.pallas_call(
            _qkv_lat_kernel, grid=(st.tiles,), in_specs=base_specs + tab_specs, out_specs=out_specs,
            out_shape=outs, compiler_params=_cparams(("arbitrary",), VMEM_LIMIT_BYTES), name="qkv_latent",
        )(x, mods, g1, w_qkv, qg, kg, bd, *tq, *tk)
    outs += [jax.ShapeDtypeStruct((st.n_tok, nk), F32)] * 2
    out_specs += [_tok_spec(nk)] * 2
    return pl.pallas_call(
        _qkv_ctx_kernel, grid=(st.tiles,), in_specs=base_specs, out_specs=out_specs,
        out_shape=outs, compiler_params=_cparams(("arbitrary",), VMEM_LIMIT_BYTES), name="qkv_context",
    )(x, mods, g1, w_qkv, qg, kg, bd)


def _softmax_av(s, v, sink):
    mx = jnp.maximum(jnp.max(s, axis=1, keepdims=True), sink)
    p = jnp.exp(s - mx)
    den = jnp.sum(p, axis=1, keepdims=True) + jnp.exp(sink - mx)
    return _dot(p.astype(BF16), v) / den


def _qk(q, k):
    return lax.dot_general(q, k, (((1,), (1,)), ((), ())), preferred_element_type=F32)


def _ctx_attn_kernel(sink_ref, q_ref, k_ref, v_ref, o_ref):
    q, k, v = q_ref[...], k_ref[...], v_ref[...]
    outs = []
    for h in range(N_HEADS):
        kv = h // GQA_G
        ks = slice(kv * HEAD_DIM, (kv + 1) * HEAD_DIM)
        s = _qk(q[:, h * HEAD_DIM:(h + 1) * HEAD_DIM], k[:, ks])
        outs.append(_softmax_av(s, v[:, ks], sink_ref[h]))
    o_ref[...] = jnp.concatenate(outs, axis=1).astype(BF16)


def _ctx_attn(st, q, k, v, sink):
    nq, nk = N_HEADS * HEAD_DIM, N_KV * HEAD_DIM
    seq = st.seq
    return pl.pallas_call(
        _ctx_attn_kernel,
        grid=(st.batch,),
        in_specs=[pl.BlockSpec(memory_space=pltpu.SMEM),
                  pl.BlockSpec((seq, nq), lambda b: (b, 0)),
                  pl.BlockSpec((seq, nk), lambda b: (b, 0)),
                  pl.BlockSpec((seq, nk), lambda b: (b, 0))],
        out_specs=pl.BlockSpec((seq, nq), lambda b: (b, 0)),
        out_shape=jax.ShapeDtypeStruct((st.n_tok, nq), BF16),
        compiler_params=_cparams(("arbitrary",), VMEM_LIMIT_BYTES),
        name="context_attention",
    )(sink, q, k, v)


def _lat_attn_kernel(sink_ref, q_ref, k_ref, v_ref, kc_ref, vc_ref, o_ref, *, n_blocks):
    j = pl.program_id(1)
    w = WINDOW
    jp = jnp.maximum(j - 1, 0)
    jn = jnp.minimum(j + 1, n_blocks - 1)

    def rows(ref, blk):
        return ref[pl.ds(pl.multiple_of(blk * w, w), w), :]

    kcat = jnp.concatenate([rows(k_ref, jp), rows(k_ref, j), rows(k_ref, jn), kc_ref[...].astype(BF16)], axis=0)
    vcat = jnp.concatenate([rows(v_ref, jp), rows(v_ref, j), rows(v_ref, jn), vc_ref[...].astype(BF16)], axis=0)
    n_keys = kcat.shape[0]
    qi = lax.broadcasted_iota(I32, (w, n_keys), 0)
    ki = lax.broadcasted_iota(I32, (w, n_keys), 1)
    ok = (((ki < w) & (j > 0) & (ki >= qi))
          | ((ki >= w) & (ki < 2 * w))
          | ((ki >= 2 * w) & (ki < 3 * w) & (j < n_blocks - 1) & (ki - 2 * w <= qi))
          | (ki >= 3 * w))
    q = q_ref[...]
    outs = []
    for h in range(N_HEADS):
        kv = h // GQA_G
        ks = slice(kv * HEAD_DIM, (kv + 1) * HEAD_DIM)
        s = _qk(q[:, h * HEAD_DIM:(h + 1) * HEAD_DIM], kcat[:, ks])
        s = jnp.where(ok, s, -jnp.inf)
        outs.append(_softmax_av(s, vcat[:, ks], sink_ref[h]))
    o_ref[...] = jnp.concatenate(outs, axis=1).astype(BF16)


def _lat_attn(st, q, k, v, k_ctx, v_ctx, sink):
    nq, nk = N_HEADS * HEAD_DIM, N_KV * HEAD_DIM
    seq, past = st.seq, k_ctx.shape[0] // st.batch
    n_blocks = seq // WINDOW
    return pl.pallas_call(
        functools.partial(_lat_attn_kernel, n_blocks=n_blocks),
        grid=(st.batch, n_blocks),
        in_specs=[pl.BlockSpec(memory_space=pltpu.SMEM),
                  pl.BlockSpec((WINDOW, nq), lambda b, j: (b * n_blocks + j, 0)),
                  pl.BlockSpec((seq, nk), lambda b, j: (b, 0)),
                  pl.BlockSpec((seq, nk), lambda b, j: (b, 0)),
                  pl.BlockSpec((past, nk), lambda b, j: (b, 0)),
                  pl.BlockSpec((past, nk), lambda b, j: (b, 0))],
        out_specs=pl.BlockSpec((WINDOW, nq), lambda b, j: (b * n_blocks + j, 0)),
        out_shape=jax.ShapeDtypeStruct((st.n_tok, nq), BF16),
        compiler_params=_cparams(("arbitrary", "arbitrary"), VMEM_LIMIT_BYTES),
        name="latent_attention",
    )(sink, q, k, v, k_ctx, v_ctx)


def _fourier_kernel(x_ref, mod_ref, g_ref, cs_ref, ss_ref, cc_ref, sc_ref, o_ref, *, scale):
    m = mod_ref[...]
    h = _norm_mod(x_ref[...], g_ref[...], _mod_slice(m, 0), _mod_slice(m, 1)).astype(BF16)
    p = _dot(cs_ref[...], h).astype(BF16)
    q = _dot(ss_ref[...], h).astype(BF16)
    gw = FOURIER_GW
    cc, sc = cc_ref[...], sc_ref[...]
    outs = []
    for g in range(FOURIER_GROUPS):
        sl = slice(g * gw, (g + 1) * gw)
        outs.append(_dot(p[:, sl], cc) - _dot(q[:, sl], sc))
    o_ref[...] = (jnp.concatenate(outs, axis=1) * scale).astype(BF16)


def _dft_tables(n):
    k = jnp.arange(n, dtype=I32)
    ang = ((k[:, None] * k[None, :]) % n).astype(F32) * (2.0 * math.pi / n)
    return jnp.cos(ang).astype(BF16), jnp.sin(ang).astype(BF16)


def _fourier(st, x, mods, layer, g1):
    seq = st.seq
    cs, ss = _dft_tables(seq)
    cc, sc = _dft_tables(FOURIER_GW)
    mod_spec = pl.BlockSpec((None, None, 1, N_MOD * D), lambda b: (layer, 0 if st.shared else 1 + b, 0, 0))
    return pl.pallas_call(
        functools.partial(_fourier_kernel, scale=1.0 / math.sqrt(seq * FOURIER_GW)),
        grid=(st.batch,),
        in_specs=[pl.BlockSpec((seq, D), lambda b: (b, 0)), mod_spec, _const_spec((1, D)),
                  _const_spec((seq, seq)), _const_spec((seq, seq)),
                  _const_spec((FOURIER_GW, FOURIER_GW)), _const_spec((FOURIER_GW, FOURIER_GW))],
        out_specs=pl.BlockSpec((seq, D), lambda b: (b, 0)),
        out_shape=jax.ShapeDtypeStruct((st.n_tok, D), BF16),
        compiler_params=_cparams(("arbitrary",), VMEM_LIMIT_BYTES),
        name="fourier",
    )(x, mods, g1, cs, ss, cc, sc)


def kernel(x_prompt, x_sample, state_rglru, cache_k, cache_v, c, c_ctx, mod_w, mod_b, norm1_g, norm2_g,
           rg_w_in, rg_conv_w, rg_conv_b, rg_w_a, rg_b_a, rg_w_x, rg_b_x, rg_lambda, rg_w_out,
           at_w_qkv, at_q_norm, at_k_norm, at_sink, at_w_o, ft_w, moe_router, moe_w_gate, moe_w_up, moe_w_down):
    depth = mod_w.shape[0]
    batch, seq, _ = x_prompt.shape
    dec_batch, dec_seq, _ = x_sample.shape
    assert 1 + dec_batch <= MOD_ROWS
    streams = (_Stream(batch, seq, True), _Stream(dec_batch, dec_seq, False))
    cond = jnp.concatenate([c_ctx[None, :], c, jnp.zeros((MOD_ROWS - 1 - dec_batch, D), F32)], axis=0)
    mods = _modulation(cond, mod_w, mod_b).reshape(depth, MOD_ROWS, 1, N_MOD * D)

    xs = [x_prompt.reshape(batch * seq, D), x_sample.reshape(dec_batch * dec_seq, D)]
    new_rg, new_k, new_v = [], [], []
    n_mixers = 3
    for layer in range(depth):
        kind, j = layer % n_mixers, layer // n_mixers
        g1 = norm1_g[layer].reshape(1, D)
        g2 = norm2_g[layer].reshape(1, D)
        wr = _router_pieces(moe_router[layer])
        for si, st in enumerate(streams):
            x = xs[si]
            if kind == 0:
                gate, u = _rg_in(st, x, mods, layer, g1, rg_w_in[j].astype(BF16))
                if st.shared:
                    h0 = jnp.zeros((2, st.batch, D), F32)
                else:
                    h0 = jnp.transpose(state_rglru[:, j], (1, 0, 2))
                h, fin = _rg_scan(st, u, h0, rg_conv_w[j], rg_conv_b[j], rg_w_a[j], rg_b_a[j],
                                  rg_w_x[j], rg_b_x[j], rg_lambda[j])
                if st.shared:
                    new_rg.append(jnp.transpose(fin, (1, 0, 2)))
                x1, xnb, aff =_lin_out(st, (h, gate), rg_w_out[j].astype(BF16), x, mods, layer, g2, wr, True)
            elif kind == 1:
                w_qkv = at_w_qkv[j].astype(BF16)
                if st.shared:
                    q, k, v, kc, vc = _qkv(st, x, mods, layer, g1, w_qkv, at_q_norm[j], at_k_norm[j], False)
                    new_k.append(kc.reshape(st.batch, st.seq, N_KV, HEAD_DIM))
                    new_v.append(vc.reshape(st.batch, st.seq, N_KV, HEAD_DIM))
                    a = _ctx_attn(st, q, k, v, at_sink[j])
                else:
                    q, k, v = _qkv(st, x, mods, layer, g1, w_qkv, at_q_norm[j], at_k_norm[j], True)
                    nk = N_KV * HEAD_DIM
                    a = _lat_attn(st, q, k, v, cache_k[:, j].reshape(-1, nk), cache_v[:, j].reshape(-1, nk),
                                  at_sink[j])
                x1, xnb, aff =_lin_out(st, (a,), at_w_o[j].astype(BF16), x, mods, layer, g2, wr, False)
            else:
                a = _fourier(st, x, mods, layer, g1)
                x1, xnb, aff =_lin_out(st, (a,), ft_w[j].astype(BF16), x, mods, layer, g2, wr, False)
            xs[si] = _moe(st, x1, xnb, aff, mods, layer, moe_w_gate, moe_w_up, moe_w_down)
    return (xs[0].reshape(batch, seq, D), xs[1].reshape(dec_batch, dec_seq, D),
            jnp.stack(new_rg, axis=1), jnp.stack(new_k, axis=1), jnp.stack(new_v, axis=1))
```

```python
import functools
import math

import numpy as np
import jax
import jax.numpy as jnp
from jax import lax
from jax.experimental import pallas as pl
from jax.experimental.pallas import tpu as pltpu

F32 = jnp.float32
BF16 = jnp.bfloat16
I32 = jnp.int32
U32 = jnp.uint32

D = 1024
N_MOD = 6
EPS = 1e-6
GRID_W = 64
CONV_W = 4
CONV_LEFT = 2
LRU_C = 8.0
RNN_BLOCKS = 16
RNN_BLOCK = D // RNN_BLOCKS
N_HEADS = 16
N_KV = 4
HEAD_DIM = 64
GQA_G = N_HEADS // N_KV
WINDOW = 128
ROPE_BASE = 10000.0
FOURIER_GROUPS = 4
FOURIER_GW = D // FOURIER_GROUPS
N_EXPERTS = 16
EC_FACTOR = 2

LANES = 128
SUBLANES = 8
VMEM_LIMIT_BYTES = 56 * 1024 * 1024

TOK_TILE = 256
MOD_ROWS = 16
SCAN_CC = 128
SCAN_TC = 32
FF_CHUNK = 512


def _cparams(sem, vmem=None):
    return pltpu.CompilerParams(dimension_semantics=sem, vmem_limit_bytes=vmem)


def _split2(a):
    hi = a.astype(BF16)
    lo = (a - hi.astype(F32)).astype(BF16)
    return hi, lo


def _dot(a, b):
    return jnp.dot(a, b, preferred_element_type=F32)


def _dot3(a, b):
    a_hi, a_lo = _split2(a)
    b_hi, b_lo = _split2(b)
    return _dot(a_hi, b_hi) + _dot(a_hi, b_lo) + _dot(a_lo, b_hi)


def _norm_mod(x, g, shift, scale):
    ms = jnp.mean(x * x, axis=-1, keepdims=True)
    y = x * lax.rsqrt(ms + EPS) * g
    return y * (1.0 + scale) + shift


def _mod_slice(m, k):
    return m[:, k * D:(k + 1) * D]


class _Stream:
    def __init__(self, batch, seq, shared_cond):
        self.batch, self.seq, self.shared = batch, seq, shared_cond
        self.n_tok = batch * seq
        self.tiles = self.n_tok // TOK_TILE
        self.tiles_per_seq = seq // TOK_TILE

    def mod_row(self, i):
        return 0 if self.shared else 1 + i // self.tiles_per_seq

    def mod_spec(self, layer):
        return pl.BlockSpec((None, None, 1, N_MOD * D), lambda i: (layer, self.mod_row(i), 0, 0))

    def seq_major_spec(self):
        tps = self.tiles_per_seq
        return pl.BlockSpec((TOK_TILE, D), lambda i: (i % tps, i // tps))


def _tok_spec(width=D):
    return pl.BlockSpec((TOK_TILE, width), lambda i: (i, 0))


def _const_spec(shape):
    nd = len(shape)
    return pl.BlockSpec(shape, lambda i: (0,) * nd)


def _mod_kernel(c_ref, w_ref, b_ref, o_ref):
    c = c_ref[...]
    c = c * jax.nn.sigmoid(c)
    o_ref[...] = _dot3(c, w_ref[...]) + b_ref[...]


def _modulation(cond, mod_w, mod_b):
    depth = mod_w.shape[0]
    tn = N_MOD * D // 4
    return pl.pallas_call(
        _mod_kernel,
        grid=(depth, N_MOD * D // tn),
        in_specs=[pl.BlockSpec((MOD_ROWS, D), lambda l, n: (0, 0)),
                  pl.BlockSpec((None, D, tn), lambda l, n: (l, 0, n)),
                  pl.BlockSpec((None, 1, tn), lambda l, n: (l, 0, n))],
        out_specs=pl.BlockSpec((None, MOD_ROWS, tn), lambda l, n: (l, 0, n)),
        out_shape=jax.ShapeDtypeStruct((depth, MOD_ROWS, N_MOD * D), F32),
        compiler_params=_cparams(("arbitrary", "arbitrary"), VMEM_LIMIT_BYTES),
        name="modulation",
    )(cond, mod_w, mod_b.reshape(depth, 1, N_MOD * D))


def _rg_in_kernel(x_ref, mod_ref, g_ref, w_ref, gate_ref, u_ref):
    m = mod_ref[...]
    h = _norm_mod(x_ref[...], g_ref[...], _mod_slice(m, 0), _mod_slice(m, 1))
    gu = _dot(h.astype(BF16), w_ref[...])
    gate_ref[...] = gu[:, :D]
    u_ref[...] = gu[:, D:]


def _rg_in(st, x, mods, layer, g1, w_in):
    out = jax.ShapeDtypeStruct((st.seq, st.batch * D), F32)
    return pl.pallas_call(
        _rg_in_kernel,
        grid=(st.tiles,),
        in_specs=[_tok_spec(), st.mod_spec(layer), _const_spec((1, D)), _const_spec((D, 2 * D))],
        out_specs=[st.seq_major_spec(), st.seq_major_spec()],
        out_shape=[out, out],
        compiler_params=_cparams(("arbitrary",), VMEM_LIMIT_BYTES),
        name="rg_in",
    )(x, mods, g1, w_in)


def _softplus(z):
    return jnp.maximum(z, 0.0) + jnp.log1p(jnp.exp(-jnp.abs(z)))


def _rg_scan_kernel(u_ref, h0_ref, cw_ref, cb_ref, wg_ref, bg_ref, lam_ref, h_ref, fin_ref,
                    upad, a_f, b_f, a_b, b_b, *, seq):
    cc = u_ref.shape[-1]
    pad_hi = CONV_W - 1 - CONV_LEFT
    upad[0:CONV_LEFT] = jnp.zeros((CONV_LEFT, SUBLANES, cc), F32)
    upad[CONV_LEFT:CONV_LEFT + seq] = u_ref[...]
    upad[CONV_LEFT + seq:CONV_LEFT + seq + pad_hi] = jnp.zeros((pad_hi, SUBLANES, cc), F32)
    half_c_sp = (-0.5 * LRU_C) * _softplus(-lam_ref[...])
    cw = cw_ref[...]
    rows = SCAN_TC * SUBLANES

    def coef(c, carry):
        t0 = pl.multiple_of(c * SCAN_TC, SCAN_TC)
        uc = cb_ref[...] + cw[0:1] * upad[pl.ds(t0, SCAN_TC)]
        for k in range(1, CONV_W):
            uc = uc + cw[k:k + 1] * upad[pl.ds(t0 + k, SCAN_TC)]
        u2 = uc.reshape(rows, cc)
        gth = jnp.tanh(0.5 * (_dot(u2.astype(BF16), wg_ref[...]) + bg_ref[...]))
        hu2 = 0.5 * u2
        for d, (a_s, b_s) in enumerate(((a_f, b_f), (a_b, b_b))):
            r_th = gth[:, (2 * d) * cc:(2 * d + 1) * cc]
            i_th = gth[:, (2 * d + 1) * cc:(2 * d + 2) * cc]
            log_a = half_c_sp[d] * r_th + half_c_sp[d]
            a = jnp.exp(log_a)
            one_minus_a2 = 1.0 - a * a
            a_s[pl.ds(t0, SCAN_TC)] = a.reshape(SCAN_TC, SUBLANES, cc)
            b_s[pl.ds(t0, SCAN_TC)] = (jnp.sqrt(one_minus_a2) * (i_th * hu2 + hu2)).reshape(SCAN_TC, SUBLANES, cc)
        return carry

    lax.fori_loop(0, seq // SCAN_TC, coef, 0)

    def step(t, carry):
        hf, hb = carry
        hf = a_f[t] * hf + b_f[t]
        b_f[t] = hf
        tb = seq - 1 - t
        hb = a_b[tb] * hb + b_b[tb]
        b_b[tb] = hb
        return hf, hb

    hf, hb = lax.fori_loop(0, seq, step, (h0_ref[0], h0_ref[1]), unroll=8)
    fin_ref[0] = hf
    fin_ref[1] = hb
    h_ref[...] = b_f[...] + b_b[...]


def _blockdiag_pairs(w):
    per = SCAN_CC // RNN_BLOCK
    w4 = w.reshape(D // SCAN_CC, per, RNN_BLOCK, RNN_BLOCK)
    eye = jnp.eye(per, dtype=w.dtype)
    return jnp.einsum('cipq,ij->cipjq', w4, eye).reshape(D // SCAN_CC, SCAN_CC, SCAN_CC)


def _rg_scan(st, u, h0, conv_w, conv_b, w_a, b_a, w_x, b_x, lam):
    seq, batch = st.seq, st.batch
    n_cc = D // SCAN_CC
    wg = jnp.concatenate([_blockdiag_pairs(w_a[0]), _blockdiag_pairs(w_x[0]),
                          _blockdiag_pairs(w_a[1]), _blockdiag_pairs(w_x[1])], axis=-1).astype(BF16)
    bg = jnp.concatenate([b.reshape(n_cc, 1, SCAN_CC) for b in (b_a[0], b_x[0], b_a[1], b_x[1])], axis=-1)
    blk = (seq, SUBLANES, SCAN_CC)
    scr = pltpu.VMEM(blk, F32)
    h, fin = pl.pallas_call(
        functools.partial(_rg_scan_kernel, seq=seq),
        grid=(batch // SUBLANES, n_cc),
        in_specs=[pl.BlockSpec(blk, lambda b, c: (0, b, c)),
                  pl.BlockSpec((2, SUBLANES, SCAN_CC), lambda b, c: (0, b, c)),
                  pl.BlockSpec((CONV_W, 1, SCAN_CC), lambda b, c: (0, 0, c)),
                  pl.BlockSpec((1, 1, SCAN_CC), lambda b, c: (0, 0, c)),
                  pl.BlockSpec((None, SCAN_CC, 4 * SCAN_CC), lambda b, c: (c, 0, 0)),
                  pl.BlockSpec((None, 1, 4 * SCAN_CC), lambda b, c: (c, 0, 0)),
                  pl.BlockSpec((2, 1, SCAN_CC), lambda b, c: (0, 0, c))],
        out_specs=[pl.BlockSpec(blk, lambda b, c: (0, b, c)),
                   pl.BlockSpec((2, SUBLANES, SCAN_CC), lambda b, c: (0, b, c))],
        out_shape=[jax.ShapeDtypeStruct((seq, batch, D), F32),
                   jax.ShapeDtypeStruct((2, batch, D), F32)],
        scratch_shapes=[pltpu.VMEM((seq + CONV_W - 1, SUBLANES, SCAN_CC), F32), scr, scr, scr, scr],
        compiler_params=_cparams(("arbitrary", "arbitrary"), VMEM_LIMIT_BYTES),
        name="rg_scan",
    )(u.reshape(seq, batch, D), h0, conv_w.reshape(CONV_W, 1, D), conv_b.reshape(1, 1, D),
      wg, bg, lam.reshape(2, 1, D))
    return h.reshape(seq, batch * D), fin


def _route_and_pack(x, m, g2_ref, wr_ref, xo_ref, xnb_ref, aff_ref, slab_s):
    xo_ref[...] = x
    xn = _norm_mod(x, g2_ref[...], _mod_slice(m, 3), _mod_slice(m, 4))
    for j in range(D // LANES):
        slab_s[pl.ds(j, TOK_TILE, stride=D // LANES), :] = xn[:, j * LANES:(j + 1) * LANES]
    xnb_ref[...] = slab_s[...].astype(BF16)
    x_hi = xn.astype(BF16)
    x_lo = (xn - x_hi.astype(F32)).astype(BF16)
    wr = wr_ref[...]
    l1 = _dot(x_hi, wr)
    l2 = _dot(x_lo, wr)
    e = N_EXPERTS
    logit = (l1 + pltpu.roll(l1, LANES - e, 1) + pltpu.roll(l1, LANES - 2 * e, 1)
             + l2 + pltpu.roll(l2, LANES - e, 1))
    lane = lax.broadcasted_iota(I32, logit.shape, 1)
    logit = jnp.where(lane < e, logit, -jnp.inf)
    mx = jnp.max(logit, axis=1, keepdims=True)
    ex = jnp.exp(logit - mx)
    aff = ex / jnp.sum(ex, axis=1, keepdims=True)
    aff_t = aff.T
    for k in range(TOK_TILE // LANES):
        aff_ref[k] = aff_t[0:e, k * LANES:(k + 1) * LANES]


def _lin_out_kernel(a_ref, w_ref, x_ref, mod_ref, g2_ref, wr_ref, xo_ref, xnb_ref, aff_ref, slab_s):
    m = mod_ref[...]
    y = _dot(a_ref[...], w_ref[...])
    x = x_ref[...] + _mod_slice(m, 2) * y
    _route_and_pack(x, m, g2_ref, wr_ref, xo_ref, xnb_ref, aff_ref, slab_s)


def _lin_out_gated_kernel(h_ref, gate_ref, w_ref, x_ref, mod_ref, g2_ref, wr_ref, xo_ref, xnb_ref, aff_ref,
                          slab_s):
    m = mod_ref[...]
    a = (h_ref[...] * jax.nn.gelu(gate_ref[...])).astype(BF16)
    y = _dot(a, w_ref[...])
    x = x_ref[...] + _mod_slice(m, 2) * y
    _route_and_pack(x, m, g2_ref, wr_ref, xo_ref, xnb_ref, aff_ref, slab_s)


def _router_pieces(w_router):
    hi = w_router.astype(BF16)
    r1 = w_router - hi.astype(F32)
    mid = r1.astype(BF16)
    lo = (r1 - mid.astype(F32)).astype(BF16)
    pad = jnp.zeros((D, LANES - 3 * N_EXPERTS), BF16)
    return jnp.concatenate([hi, mid, lo, pad], axis=1)


def _lin_out(st, srcs, w, x, mods, layer, g2, wr, gated):
    n_chunks = st.n_tok // LANES
    if gated:
        body, src_specs = _lin_out_gated_kernel, [st.seq_major_spec(), st.seq_major_spec()]
    else:
        body, src_specs = _lin_out_kernel, [_tok_spec()]
    return pl.pallas_call(
        body,
        grid=(st.tiles,),
        in_specs=src_specs + [_const_spec((D, D)), _tok_spec(), st.mod_spec(layer), _const_spec((1, D)),
                              _const_spec((D, LANES))],
        out_specs=[_tok_spec(), pl.BlockSpec((TOK_TILE * (D // LANES), LANES), lambda i: (i, 0)),
                   pl.BlockSpec((TOK_TILE // LANES, N_EXPERTS, LANES), lambda i: (i, 0, 0))],
        out_shape=[jax.ShapeDtypeStruct((st.n_tok, D), F32),
                   jax.ShapeDtypeStruct((st.n_tok * (D // LANES), LANES), BF16),
                   jax.ShapeDtypeStruct((n_chunks, N_EXPERTS, LANES), F32)],
        scratch_shapes=[pltpu.VMEM((TOK_TILE * (D // LANES), LANES), F32)],
        compiler_params=_cparams(("arbitrary",), VMEM_LIMIT_BYTES),
        name="lin_out_gated" if gated else "lin_out",
    )(*srcs, w, x, mods, g2, wr)


def _select_kernel(a_ref, idx_ref, g_ref, linc_s, cnt_s, crow_s, *, cap):
    n_e, n_ch, _ = a_ref.shape
    assert n_ch & (n_ch - 1) == 0 and n_ch <= LANES
    rows = n_e * n_ch
    a3 = a_ref[...]
    capf = jnp.float32(cap)

    def count(mask3):
        c = jnp.sum(mask3.astype(F32), axis=2, keepdims=True)
        return jnp.sum(c, axis=1, keepdims=True)

    def as_f32(bits):
        return pltpu.bitcast(bits, F32)

    def search(i, thr):
        cand = thr | (jnp.int32(1) << (30 - i))
        return jnp.where(count(a3 >= as_f32(cand)) >= capf, cand, thr)

    thr = lax.fori_loop(0, 31, search, jnp.zeros((n_e, 1, LANES), I32))
    gt3 = a3 >= as_f32(thr + 1)
    eq3 = jnp.logical_and(a3 >= as_f32(thr), jnp.logical_not(gt3))
    need = capf - count(gt3)

    li = lax.broadcasted_iota(I32, (LANES, LANES), 0)
    lj = lax.broadcasted_iota(I32, (LANES, LANES), 1)
    upper = (li <= lj).astype(BF16)
    ones = jnp.ones((LANES, LANES), BF16)
    ri = lax.broadcasted_iota(I32, (rows, rows), 0)
    rj = lax.broadcasted_iota(I32, (rows, rows), 1)
    sh = n_ch.bit_length() - 1
    before = (((ri >> sh) == (rj >> sh)) & (rj < ri)).astype(BF16)

    def prefixes(mask3):
        x = mask3.reshape(rows, LANES).astype(BF16)
        local = _dot(x, upper)
        tot = _dot(x, ones)
        return local, tot, _dot(before, tot.astype(BF16))

    gt_l, gt_t, gt_x = prefixes(gt3)
    eq_l, eq_t, eq_x = prefixes(eq3)
    need_r = jnp.broadcast_to(need, (n_e, n_ch, LANES)).reshape(rows, LANES)
    sel_incl = gt_x + gt_l + jnp.minimum(eq_x + eq_l, need_r)
    sel_x = gt_x + jnp.minimum(eq_x, need_r)
    sel_c = gt_x + gt_t + jnp.minimum(eq_x + eq_t, need_r)
    linc_s[...] = sel_incl - sel_x
    cnt_s[...] = sel_c - sel_x
    c3 = sel_c.reshape(n_e, n_ch, LANES)
    pick = lax.broadcasted_iota(I32, (1, n_ch, LANES), 1) == lax.broadcasted_iota(I32, (1, n_ch, LANES), 2)
    crow = jnp.sum(jnp.where(pick, c3, 0.0), axis=1)
    crow = jnp.where(lax.broadcasted_iota(I32, crow.shape, 1) < n_ch, crow, jnp.float32(2 * cap + n_ch * LANES))
    crow_s[...] = crow

    slot = lax.broadcasted_iota(I32, (cap, LANES), 0).astype(F32)
    lane = lax.broadcasted_iota(I32, (cap, LANES), 1).astype(F32)
    diag = lax.broadcasted_iota(I32, (LANES, LANES), 0) == lax.broadcasted_iota(I32, (LANES, LANES), 1)
    zpad = jnp.zeros((LANES - n_ch, LANES), BF16)

    def per_expert(e, carry):
        r0 = pl.multiple_of(e * n_ch, n_ch)
        passed = (crow_s[pl.ds(e, 1), :] <= slot).astype(BF16)
        cnt_e = jnp.concatenate([cnt_s[pl.ds(r0, n_ch), :].astype(BF16), zpad], axis=0)
        linc_e = jnp.concatenate([linc_s[pl.ds(r0, n_ch), :].astype(BF16), zpad], axis=0)
        chunk = _dot(passed, ones)
        rank = slot - _dot(passed, cnt_e)
        onehot = (chunk == lane).astype(BF16)
        g = _dot(onehot, linc_e)
        within = _dot((g <= rank).astype(BF16), ones)
        tok = chunk * jnp.float32(LANES) + within
        a_e = a_ref[e]
        a_hi = a_e.astype(BF16)
        r1 = a_e - a_hi.astype(F32)
        a_mid = r1.astype(BF16)
        a_lo = (r1 - a_mid.astype(F32)).astype(BF16)
        arow = (_dot(onehot, jnp.concatenate([a_hi, zpad], axis=0))
                + _dot(onehot, jnp.concatenate([a_mid, zpad], axis=0))
                + _dot(onehot, jnp.concatenate([a_lo, zpad], axis=0)))
        gate = jnp.sum(jnp.where(lane == within, arow, 0.0), axis=1, keepdims=True)
        gate = jnp.broadcast_to(gate, (cap, LANES))

        def to_row(col):
            pieces = []
            for b in range(cap // LANES):
                blk = col[b * LANES:(b + 1) * LANES, :]
                pieces.append(jnp.sum(jnp.where(diag, blk, 0.0), axis=0, keepdims=True))
            return jnp.concatenate(pieces, axis=1)

        idx_ref[pl.ds(e, 1), :] = to_row(tok).astype(I32)
        g_ref[pl.ds(e, 1), :] = to_row(gate)
        return carry

    lax.fori_loop(0, n_e, per_expert, 0)


def _select(aff_chunks, cap):
    n_ch = aff_chunks.shape[0]
    a = jnp.transpose(aff_chunks, (1, 0, 2))
    rows = N_EXPERTS * n_ch
    return pl.pallas_call(
        functools.partial(_select_kernel, cap=cap),
        out_shape=[jax.ShapeDtypeStruct((N_EXPERTS, cap), I32),
                   jax.ShapeDtypeStruct((N_EXPERTS, cap), F32)],
        scratch_shapes=[pltpu.VMEM((rows, LANES), F32), pltpu.VMEM((rows, LANES), F32),
                        pltpu.VMEM((N_EXPERTS, LANES), F32)],
        compiler_params=pltpu.CompilerParams(vmem_limit_bytes=VMEM_LIMIT_BYTES),
        name="select",
    )(a)


ROW_CH = D // LANES
GATHER_UNROLL = 8
FFN_ROWS = 512


def _cm_stride(m):
    return m + SUBLANES


def _moe_ffn_kernel(idx_ref, src_ref, g_ref, wg_ref, wu_ref, wd_ref, o_ref, tile_s, xe_s, *, cap):
    e = pl.program_id(0)
    f = pl.program_id(1)
    stride = _cm_stride(cap)

    @pl.when(f == 0)
    def _gather():
        def body(gi, carry):
            base = pl.multiple_of(gi * GATHER_UNROLL, GATHER_UNROLL)
            for k in range(GATHER_UNROLL):
                n = idx_ref[e * cap + base + k]
                r = pl.multiple_of((n >> 1) * (2 * ROW_CH), 2 * ROW_CH)
                pair = src_ref[pl.ds(r, 2 * ROW_CH), :].astype(F32)
                slab = jnp.where((n & 1) == 1, pair[ROW_CH:], pair[:ROW_CH])
                tile_s[pl.ds(base + k, ROW_CH, stride=stride), :] = slab
            return carry

        lax.fori_loop(0, cap // GATHER_UNROLL, body, 0)
        for j in range(ROW_CH):
            xe_s[:, j * LANES:(j + 1) * LANES] = tile_s[pl.ds(j * stride, cap), :].astype(BF16)
        o_ref[...] = jnp.zeros(o_ref.shape, F32)

    wg = wg_ref[...].astype(BF16)
    wu = wu_ref[...].astype(BF16)
    wd = wd_ref[...].astype(BF16)
    last = pl.num_programs(1) - 1
    tm = min(FFN_ROWS, cap)
    for t in range(cap // tm):
        r0 = t * tm
        xt = xe_s[r0:r0 + tm, :]
        hg = _dot(xt, wg)
        hu = _dot(xt, wu)
        h = (hg * jax.nn.sigmoid(hg) * hu).astype(BF16)
        y = _dot(h, wd)
        gt = jnp.where(f == last, g_ref[r0:r0 + tm, :], 1.0)
        for j in range(ROW_CH):
            rows = pl.ds(j * stride + r0, tm)
            o_ref[rows, :] = (o_ref[rows, :] + y[:, j * LANES:(j + 1) * LANES]) * gt


def _moe_ffn(idx, gates, xnb, w_gate, w_up, w_down, layer, cap):
    n_tok = xnb.shape[0] // ROW_CH
    ff = w_gate.shape[-1]
    stride = _cm_stride(cap)
    src = xnb
    grid_spec = pltpu.PrefetchScalarGridSpec(
        num_scalar_prefetch=1,
        grid=(N_EXPERTS, ff // FF_CHUNK),
        in_specs=[pl.BlockSpec((n_tok * ROW_CH, LANES), lambda e, f, idx: (0, 0), pipeline_mode=pl.Buffered(1)),
                  pl.BlockSpec((None, cap, 1), lambda e, f, idx: (e, 0, 0)),
                  pl.BlockSpec((None, None, D, FF_CHUNK), lambda e, f, idx: (layer, e, 0, f)),
                  pl.BlockSpec((None, None, D, FF_CHUNK), lambda e, f, idx: (layer, e, 0, f)),
                  pl.BlockSpec((None, None, FF_CHUNK, D), lambda e, f, idx: (layer, e, f, 0))],
        out_specs=pl.BlockSpec((None, ROW_CH * stride, LANES), lambda e, f, idx: (e, 0, 0)),
        scratch_shapes=[pltpu.VMEM((ROW_CH * stride, LANES), F32), pltpu.VMEM((cap, D), BF16)],
    )
    return pl.pallas_call(
        functools.partial(_moe_ffn_kernel, cap=cap),
        grid_spec=grid_spec,
        out_shape=jax.ShapeDtypeStruct((N_EXPERTS, ROW_CH * stride, LANES), F32),
        compiler_params=_cparams(("arbitrary", "arbitrary"), VMEM_LIMIT_BYTES),
        name="moe_ffn",
    )(idx, src, gates.reshape(N_EXPERTS, cap, 1), w_gate, w_up, w_down)


def _combine_kernel(idx_ref, ye_ref, o_ref, *, cap):
    e = pl.program_id(0)
    stride = _cm_stride(cap)

    @pl.when(e == 0)
    def _zero():
        o_ref[...] = jnp.zeros(o_ref.shape, F32)

    def body(gi, carry):
        base = pl.multiple_of(gi * GATHER_UNROLL, GATHER_UNROLL)
        rows, sums = [], []
        for k in range(GATHER_UNROLL):
            r = pl.multiple_of(idx_ref[e * cap + base + k] * ROW_CH, ROW_CH)
            rows.append(r)
            sums.append(o_ref[pl.ds(r, ROW_CH), :] + ye_ref[pl.ds(base + k, ROW_CH, stride=stride), :])
        for r, s in zip(rows, sums):
            o_ref[pl.ds(r, ROW_CH), :] = s
        return carry

    lax.fori_loop(0, cap // GATHER_UNROLL, body, 0)


def _combine(idx, ye, n_tok, cap):
    stride = _cm_stride(cap)
    grid_spec = pltpu.PrefetchScalarGridSpec(
        num_scalar_prefetch=1,
        grid=(N_EXPERTS,),
        in_specs=[pl.BlockSpec((None, ROW_CH * stride, LANES), lambda e, idx: (e, 0, 0))],
        out_specs=pl.BlockSpec((n_tok * ROW_CH, LANES), lambda e, idx: (0, 0), pipeline_mode=pl.Buffered(1)),
    )
    return pl.pallas_call(
        functools.partial(_combine_kernel, cap=cap),
        grid_spec=grid_spec,
        out_shape=jax.ShapeDtypeStruct((n_tok * ROW_CH, LANES), F32),
        compiler_params=_cparams(("arbitrary",), VMEM_LIMIT_BYTES),
        name="moe_combine",
    )(idx, ye)


def _residual_kernel(x_ref, y_ref, mod_ref, o_ref):
    y = jnp.concatenate([y_ref[pl.ds(j, TOK_TILE, stride=ROW_CH), :] for j in range(ROW_CH)], axis=1)
    o_ref[...] = x_ref[...] + _mod_slice(mod_ref[...], 5) * y


def _residual(st, x, y, mods, layer):
    return pl.pallas_call(
        _residual_kernel,
        grid=(st.tiles,),
        in_specs=[_tok_spec(), pl.BlockSpec((TOK_TILE * ROW_CH, LANES), lambda i: (i, 0)), st.mod_spec(layer)],
        out_specs=_tok_spec(),
        out_shape=jax.ShapeDtypeStruct((st.n_tok, D), F32),
        compiler_params=_cparams(("arbitrary",), VMEM_LIMIT_BYTES),
        name="moe_residual",
    )(x, y, mods)


def _moe(st, x, xnb, aff_chunks, mods, layer, w_gate, w_up, w_down):
    cap = EC_FACTOR * st.n_tok // N_EXPERTS
    idx, gates = _select(aff_chunks, cap)
    idx = idx.reshape(N_EXPERTS * cap)
    ye = _moe_ffn(idx, gates, xnb, w_gate, w_up, w_down, layer, cap)
    y = _combine(idx, ye, st.n_tok, cap)
    return _residual(st, x, y, mods, layer)


def _head_sumsq(x, bd):
    x2 = x * x
    hi, lo = _split2(x2)
    w = bd.shape[0]
    cols = []
    for c in range(x.shape[1] // w):
        sl = slice(c * w, (c + 1) * w)
        cols.append(_dot(hi[:, sl], bd) + _dot(lo[:, sl], bd))
    return cols[0] if len(cols) == 1 else jnp.concatenate(cols, axis=1)


def _qk_norm(x, gain, bd):
    ms = _head_sumsq(x, bd) * (1.0 / HEAD_DIM)
    return x * lax.rsqrt(ms + EPS) * gain


def _rope(x, cos, sin_dn, sin_up):
    n = x.shape[1]
    q = HEAD_DIM // 4
    return x * cos + pltpu.roll(x, n - q, 1) * sin_dn + pltpu.roll(x, q, 1) * sin_up


def _qkv_ctx_kernel(x_ref, mod_ref, g_ref, w_ref, qg_ref, kg_ref, bd_ref, q_ref, k_ref, v_ref, kc_ref, vc_ref):
    m = mod_ref[...]
    h = _norm_mod(x_ref[...], g_ref[...], _mod_slice(m, 0), _mod_slice(m, 1))
    qkv = _dot(h.astype(BF16), w_ref[...])
    nq, nk = N_HEADS * HEAD_DIM, N_KV * HEAD_DIM
    bd = bd_ref[...]
    q = _qk_norm(qkv[:, :nq], qg_ref[...], bd)
    k = _qk_norm(qkv[:, nq:nq + nk], kg_ref[...], bd)
    v = qkv[:, nq + nk:]
    q_ref[...] = (q * (HEAD_DIM ** -0.5)).astype(BF16)
    k_ref[...] = k.astype(BF16)
    v_ref[...] = v.astype(BF16)
    kc_ref[...] = k
    vc_ref[...] = v


def _qkv_lat_kernel(x_ref, mod_ref, g_ref, w_ref, qg_ref, kg_ref, bd_ref,
                    cq_ref, sdq_ref, suq_ref, ck_ref, sdk_ref, suk_ref, q_ref, k_ref, v_ref):
    m = mod_ref[...]
    h = _norm_mod(x_ref[...], g_ref[...], _mod_slice(m, 0), _mod_slice(m, 1))
    qkv = _dot(h.astype(BF16), w_ref[...])
    nq, nk = N_HEADS * HEAD_DIM, N_KV * HEAD_DIM
    bd = bd_ref[...]
    q = _qk_norm(qkv[:, :nq], qg_ref[...], bd)
    k = _qk_norm(qkv[:, nq:nq + nk], kg_ref[...], bd)
    q = _rope(q, cq_ref[...], sdq_ref[...], suq_ref[...])
    k = _rope(k, ck_ref[...], sdk_ref[...], suk_ref[...])
    q_ref[...] = (q * (HEAD_DIM ** -0.5)).astype(BF16)
    k_ref[...] = k.astype(BF16)
    v_ref[...] = qkv[:, nq + nk:].astype(BF16)


def _rope_tables(seq, n_heads):
    n_rows = seq // GRID_W
    row = jnp.repeat(jnp.arange(n_rows), GRID_W).astype(F32)
    col = jnp.tile(jnp.arange(GRID_W), n_rows).astype(F32)
    n_freq = HEAD_DIM // 4
    inv = ROPE_BASE ** (-jnp.arange(n_freq, dtype=F32) / n_freq)
    ar, ac = row[:, None] * inv, col[:, None] * inv
    ang = jnp.concatenate([ar, ar, ac, ac], axis=-1)
    cos, sin = jnp.cos(ang), jnp.sin(ang)
    even = ((jnp.arange(HEAD_DIM) // n_freq) % 2 == 0).astype(F32)
    sin_dn = -sin * even
    sin_up = sin * (1.0 - even)
    return tuple(jnp.tile(a, (1, n_heads)) for a in (cos, sin_dn, sin_up))


def _qkv(st, x, mods, layer, g1, w_qkv, q_gain, k_gain, rope):
    nq, nk = N_HEADS * HEAD_DIM, N_KV * HEAD_DIM
    bd = jnp.asarray(np.kron(np.eye(4, dtype=np.float32), np.ones((HEAD_DIM, HEAD_DIM), np.float32))).astype(BF16)
    qg = jnp.tile(q_gain.reshape(1, HEAD_DIM), (1, N_HEADS))
    kg = jnp.tile(k_gain.reshape(1, HEAD_DIM), (1, N_KV))
    base_specs = [_tok_spec(), st.mod_spec(layer), _const_spec((1, D)), _const_spec((D, nq + 2 * nk)),
                  _const_spec((1, nq)), _const_spec((1, nk)), _const_spec((4 * HEAD_DIM, 4 * HEAD_DIM))]
    outs = [jax.ShapeDtypeStruct((st.n_tok, nq), BF16), jax.ShapeDtypeStruct((st.n_tok, nk), BF16),
            jax.ShapeDtypeStruct((st.n_tok, nk), BF16)]
    out_specs = [_tok_spec(nq), _tok_spec(nk), _tok_spec(nk)]
    if rope:
        tps = st.tiles_per_seq
        tq, tk = _rope_tables(st.seq, N_HEADS), _rope_tables(st.seq, N_KV)
        tab_specs = ([pl.BlockSpec((TOK_TILE, nq), lambda i: (i % tps, 0))] * 3
                     + [pl.BlockSpec((TOK_TILE, nk), lambda i: (i % tps, 0))] * 3)
        return pl.pallas_call(
            _qkv_lat_kernel, grid=(st.tiles,), in_specs=base_specs + tab_specs, out_specs=out_specs,
            out_shape=outs, compiler_params=_cparams(("arbitrary",), VMEM_LIMIT_BYTES), name="qkv_latent",
        )(x, mods, g1, w_qkv, qg, kg, bd, *tq, *tk)
    outs += [jax.ShapeDtypeStruct((st.n_tok, nk), F32)] * 2
    out_specs += [_tok_spec(nk)] * 2
    return pl.pallas_call(
        _qkv_ctx_kernel, grid=(st.tiles,), in_specs=base_specs, out_specs=out_specs,
        out_shape=outs, compiler_params=_cparams(("arbitrary",), VMEM_LIMIT_BYTES), name="qkv_context",
    )(x, mods, g1, w_qkv, qg, kg, bd)


def _softmax_av(s, v, sink):
    mx = jnp.maximum(jnp.max(s, axis=1, keepdims=True), sink)
    p = jnp.exp(s - mx)
    den = jnp.sum(p, axis=1, keepdims=True) + jnp.exp(sink - mx)
    return _dot(p.astype(BF16), v) / den


def _qk(q, k):
    return lax.dot_general(q, k, (((1,), (1,)), ((), ())), preferred_element_type=F32)


def _gqa_attend(q, k, v, sink_ref, ok):
    nq = q.shape[0]
    row = lax.broadcasted_iota(I32, (GQA_G * nq, 1), 0)
    outs = []
    for kv in range(N_KV):
        ks = slice(kv * HEAD_DIM, (kv + 1) * HEAD_DIM)
        h0 = kv * GQA_G
        qs = jnp.concatenate([q[:, (h0 + g) * HEAD_DIM:(h0 + g + 1) * HEAD_DIM] for g in range(GQA_G)], axis=0)
        sink = jnp.full((GQA_G * nq, 1), sink_ref[h0], F32)
        for g in range(1, GQA_G):
            sink = jnp.where(row >= g * nq, sink_ref[h0 + g], sink)
        s = _qk(qs, k[:, ks])
        if ok is not None:
            s = jnp.where(ok, s, -jnp.inf)
        o = _softmax_av(s, v[:, ks], sink)
        outs += [o[g * nq:(g + 1) * nq] for g in range(GQA_G)]
    return jnp.concatenate(outs, axis=1)


def _ctx_attn_kernel(sink_ref, q_ref, k_ref, v_ref, o_ref):
    o_ref[...] = _gqa_attend(q_ref[...], k_ref[...], v_ref[...], sink_ref, None).astype(BF16)


def _ctx_attn(st, q, k, v, sink):
    nq, nk = N_HEADS * HEAD_DIM, N_KV * HEAD_DIM
    seq = st.seq
    return pl.pallas_call(
        _ctx_attn_kernel,
        grid=(st.batch,),
        in_specs=[pl.BlockSpec(memory_space=pltpu.SMEM),
                  pl.BlockSpec((seq, nq), lambda b: (b, 0)),
                  pl.BlockSpec((seq, nk), lambda b: (b, 0)),
                  pl.BlockSpec((seq, nk), lambda b: (b, 0))],
        out_specs=pl.BlockSpec((seq, nq), lambda b: (b, 0)),
        out_shape=jax.ShapeDtypeStruct((st.n_tok, nq), BF16),
        compiler_params=_cparams(("arbitrary",), VMEM_LIMIT_BYTES),
        name="context_attention",
    )(sink, q, k, v)


def _lat_attn_kernel(sink_ref, q_ref, k_ref, v_ref, kc_ref, vc_ref, o_ref, *, n_blocks):
    j = pl.program_id(1)
    w = WINDOW
    jp = jnp.maximum(j - 1, 0)
    jn = jnp.minimum(j + 1, n_blocks - 1)

    def rows(ref, blk):
        return ref[pl.ds(pl.multiple_of(blk * w, w), w), :]

    kcat = jnp.concatenate([rows(k_ref, jp), rows(k_ref, j), rows(k_ref, jn), kc_ref[...].astype(BF16)], axis=0)
    vcat = jnp.concatenate([rows(v_ref, jp), rows(v_ref, j), rows(v_ref, jn), vc_ref[...].astype(BF16)], axis=0)
    n_keys = kcat.shape[0]
    qi = lax.broadcasted_iota(I32, (GQA_G * w, n_keys), 0) & (w - 1)
    ki = lax.broadcasted_iota(I32, (GQA_G * w, n_keys), 1)
    ok = (((ki < w) & (j > 0) & (ki >= qi))
          | ((ki >= w) & (ki < 2 * w))
          | ((ki >= 2 * w) & (ki < 3 * w) & (j < n_blocks - 1) & (ki - 2 * w <= qi))
          | (ki >= 3 * w))
    o_ref[...] = _gqa_attend(q_ref[...], kcat, vcat, sink_ref, ok).astype(BF16)


def _lat_attn(st, q, k, v, k_ctx, v_ctx, sink):
    nq, nk = N_HEADS * HEAD_DIM, N_KV * HEAD_DIM
    seq, past = st.seq, k_ctx.shape[0] // st.batch
    n_blocks = seq // WINDOW
    return pl.pallas_call(
        functools.partial(_lat_attn_kernel, n_blocks=n_blocks),
        grid=(st.batch, n_blocks),
        in_specs=[pl.BlockSpec(memory_space=pltpu.SMEM),
                  pl.BlockSpec((WINDOW, nq), lambda b, j: (b * n_blocks + j, 0)),
                  pl.BlockSpec((seq, nk), lambda b, j: (b, 0)),
                  pl.BlockSpec((seq, nk), lambda b, j: (b, 0)),
                  pl.BlockSpec((past, nk), lambda b, j: (b, 0)),
                  pl.BlockSpec((past, nk), lambda b, j: (b, 0))],
        out_specs=pl.BlockSpec((WINDOW, nq), lambda b, j: (b * n_blocks + j, 0)),
        out_shape=jax.ShapeDtypeStruct((st.n_tok, nq), BF16),
        compiler_params=_cparams(("arbitrary", "arbitrary"), VMEM_LIMIT_BYTES),
        name="latent_attention",
    )(sink, q, k, v, k_ctx, v_ctx)


def _fourier_kernel(x_ref, mod_ref, g_ref, cs_ref, ss_ref, cc_ref, sc_ref, o_ref, *, scale):
    m = mod_ref[...]
    h = _norm_mod(x_ref[...], g_ref[...], _mod_slice(m, 0), _mod_slice(m, 1)).astype(BF16)
    p = _dot(cs_ref[...], h).astype(BF16)
    q = _dot(ss_ref[...], h).astype(BF16)
    gw = FOURIER_GW
    cc, sc = cc_ref[...], sc_ref[...]
    outs = []
    for g in range(FOURIER_GROUPS):
        sl = slice(g * gw, (g + 1) * gw)
        outs.append(_dot(p[:, sl], cc) - _dot(q[:, sl], sc))
    o_ref[...] = (jnp.concatenate(outs, axis=1) * scale).astype(BF16)


def _dft_tables(n):
    k = jnp.arange(n, dtype=I32)
    ang = ((k[:, None] * k[None, :]) % n).astype(F32) * (2.0 * math.pi / n)
    return jnp.cos(ang).astype(BF16), jnp.sin(ang).astype(BF16)


def _fourier(st, x, mods, layer, g1):
    seq = st.seq
    cs, ss = _dft_tables(seq)
    cc, sc = _dft_tables(FOURIER_GW)
    mod_spec = pl.BlockSpec((None, None, 1, N_MOD * D), lambda b: (layer, 0 if st.shared else 1 + b, 0, 0))
    return pl.pallas_call(
        functools.partial(_fourier_kernel, scale=1.0 / math.sqrt(seq * FOURIER_GW)),
        grid=(st.batch,),
        in_specs=[pl.BlockSpec((seq, D), lambda b: (b, 0)), mod_spec, _const_spec((1, D)),
                  _const_spec((seq, seq)), _const_spec((seq, seq)),
                  _const_spec((FOURIER_GW, FOURIER_GW)), _const_spec((FOURIER_GW, FOURIER_GW))],
        out_specs=pl.BlockSpec((seq, D), lambda b: (b, 0)),
        out_shape=jax.ShapeDtypeStruct((st.n_tok, D), BF16),
        compiler_params=_cparams(("arbitrary",), VMEM_LIMIT_BYTES),
        name="fourier",
    )(x, mods, g1, cs, ss, cc, sc)


def kernel(x_prompt, x_sample, state_rglru, cache_k, cache_v, c, c_ctx, mod_w, mod_b, norm1_g, norm2_g,
           rg_w_in, rg_conv_w, rg_conv_b, rg_w_a, rg_b_a, rg_w_x, rg_b_x, rg_lambda, rg_w_out,
           at_w_qkv, at_q_norm, at_k_norm, at_sink, at_w_o, ft_w, moe_router, moe_w_gate, moe_w_up, moe_w_down):
    depth = mod_w.shape[0]
    batch, seq, _ = x_prompt.shape
    dec_batch, dec_seq, _ = x_sample.shape
    assert 1 + dec_batch <= MOD_ROWS
    streams = (_Stream(batch, seq, True), _Stream(dec_batch, dec_seq, False))
    cond = jnp.concatenate([c_ctx[None, :], c, jnp.zeros((MOD_ROWS - 1 - dec_batch, D), F32)], axis=0)
    mods = _modulation(cond, mod_w, mod_b).reshape(depth, MOD_ROWS, 1, N_MOD * D)

    xs = [x_prompt.reshape(batch * seq, D), x_sample.reshape(dec_batch * dec_seq, D)]
    new_rg, new_k, new_v = [], [], []
    n_mixers = 3
    for layer in range(depth):
        kind, j = layer % n_mixers, layer // n_mixers
        g1 = norm1_g[layer].reshape(1, D)
        g2 = norm2_g[layer].reshape(1, D)
        wr = _router_pieces(moe_router[layer])
        for si, st in enumerate(streams):
            x = xs[si]
            if kind == 0:
                gate, u = _rg_in(st, x, mods, layer, g1, rg_w_in[j].astype(BF16))
                if st.shared:
                    h0 = jnp.zeros((2, st.batch, D), F32)
                else:
                    h0 = jnp.transpose(state_rglru[:, j], (1, 0, 2))
                h, fin = _rg_scan(st, u, h0, rg_conv_w[j], rg_conv_b[j], rg_w_a[j], rg_b_a[j],
                                  rg_w_x[j], rg_b_x[j], rg_lambda[j])
                if st.shared:
                    new_rg.append(jnp.transpose(fin, (1, 0, 2)))
                x1, xnb, aff =_lin_out(st, (h, gate), rg_w_out[j].astype(BF16), x, mods, layer, g2, wr, True)
            elif kind == 1:
                w_qkv = at_w_qkv[j].astype(BF16)
                if st.shared:
                    q, k, v, kc, vc = _qkv(st, x, mods, layer, g1, w_qkv, at_q_norm[j], at_k_norm[j], False)
                    new_k.append(kc.reshape(st.batch, st.seq, N_KV, HEAD_DIM))
                    new_v.append(vc.reshape(st.batch, st.seq, N_KV, HEAD_DIM))
                    a = _ctx_attn(st, q, k, v, at_sink[j])
                else:
                    q, k, v = _qkv(st, x, mods, layer, g1, w_qkv, at_q_norm[j], at_k_norm[j], True)
                    nk = N_KV * HEAD_DIM
                    a = _lat_attn(st, q, k, v, cache_k[:, j].reshape(-1, nk), cache_v[:, j].reshape(-1, nk),
                                  at_sink[j])
                x1, xnb, aff =_lin_out(st, (a,), at_w_o[j].astype(BF16), x, mods, layer, g2, wr, False)
            else:
                a = _fourier(st, x, mods, layer, g1)
                x1, xnb, aff =_lin_out(st, (a,), ft_w[j].astype(BF16), x, mods, layer, g2, wr, False)
            xs[si] = _moe(st, x1, xnb, aff, mods, layer, moe_w_gate, moe_w_up, moe_w_down)
    return (xs[0].reshape(batch, seq, D), xs[1].reshape(dec_batch, dec_seq, D),
            jnp.stack(new_rg, axis=1), jnp.stack(new_k, axis=1), jnp.stack(new_v, axis=1))
```

```python
import functools
import math

import numpy as np
import jax
import jax.numpy as jnp
from jax import lax
from jax.experimental import pallas as pl
from jax.experimental.pallas import tpu as pltpu

F32 = jnp.float32
BF16 = jnp.bfloat16
I32 = jnp.int32
U32 = jnp.uint32

D = 1024
N_MOD = 6
EPS = 1e-6
GRID_W = 64
CONV_W = 4
CONV_LEFT = 2
LRU_C = 8.0
RNN_BLOCKS = 16
RNN_BLOCK = D // RNN_BLOCKS
N_HEADS = 16
N_KV = 4
HEAD_DIM = 64
GQA_G = N_HEADS // N_KV
WINDOW = 128
ROPE_BASE = 10000.0
FOURIER_GROUPS = 4
FOURIER_GW = D // FOURIER_GROUPS
N_EXPERTS = 16
EC_FACTOR = 2

LANES = 128
SUBLANES = 8
VMEM_LIMIT_BYTES = 56 * 1024 * 1024

TOK_TILE = 256
MOD_ROWS = 16
SCAN_CC = 128
SCAN_TC = 32
FF_CHUNK = 512


def _cparams(sem, vmem=None):
    return pltpu.CompilerParams(dimension_semantics=sem, vmem_limit_bytes=vmem)


def _split2(a):
    hi = a.astype(BF16)
    lo = (a - hi.astype(F32)).astype(BF16)
    return hi, lo


def _dot(a, b):
    return jnp.dot(a, b, preferred_element_type=F32)


def _dot3(a, b):
    a_hi, a_lo = _split2(a)
    b_hi, b_lo = _split2(b)
    return _dot(a_hi, b_hi) + _dot(a_hi, b_lo) + _dot(a_lo, b_hi)


def _norm_mod(x, g, shift, scale):
    ms = jnp.mean(x * x, axis=-1, keepdims=True)
    y = x * lax.rsqrt(ms + EPS) * g
    return y * (1.0 + scale) + shift


def _mod_slice(m, k):
    return m[:, k * D:(k + 1) * D]


class _Stream:
    def __init__(self, batch, seq, shared_cond):
        self.batch, self.seq, self.shared = batch, seq, shared_cond
        self.n_tok = batch * seq
        self.tiles = self.n_tok // TOK_TILE
        self.tiles_per_seq = seq // TOK_TILE

    def mod_row(self, i):
        return 0 if self.shared else 1 + i // self.tiles_per_seq

    def mod_spec(self, layer):
        return pl.BlockSpec((None, None, 1, N_MOD * D), lambda i: (layer, self.mod_row(i), 0, 0))

    def seq_major_spec(self):
        tps = self.tiles_per_seq
        return pl.BlockSpec((TOK_TILE, D), lambda i: (i % tps, i // tps))


def _tok_spec(width=D):
    return pl.BlockSpec((TOK_TILE, width), lambda i: (i, 0))


def _const_spec(shape):
    nd = len(shape)
    return pl.BlockSpec(shape, lambda i: (0,) * nd)


def _mod_kernel(c_ref, w_ref, b_ref, o_ref):
    c = c_ref[...]
    c = c * jax.nn.sigmoid(c)
    o_ref[...] = _dot3(c, w_ref[...]) + b_ref[...]


def _modulation(cond, mod_w, mod_b):
    depth = mod_w.shape[0]
    tn = N_MOD * D // 4
    return pl.pallas_call(
        _mod_kernel,
        grid=(depth, N_MOD * D // tn),
        in_specs=[pl.BlockSpec((MOD_ROWS, D), lambda l, n: (0, 0)),
                  pl.BlockSpec((None, D, tn), lambda l, n: (l, 0, n)),
                  pl.BlockSpec((None, 1, tn), lambda l, n: (l, 0, n))],
        out_specs=pl.BlockSpec((None, MOD_ROWS, tn), lambda l, n: (l, 0, n)),
        out_shape=jax.ShapeDtypeStruct((depth, MOD_ROWS, N_MOD * D), F32),
        compiler_params=_cparams(("arbitrary", "arbitrary"), VMEM_LIMIT_BYTES),
        name="modulation",
    )(cond, mod_w, mod_b.reshape(depth, 1, N_MOD * D))


def _rg_in_kernel(x_ref, mod_ref, g_ref, w_ref, gate_ref, u_ref):
    m = mod_ref[...]
    h = _norm_mod(x_ref[...], g_ref[...], _mod_slice(m, 0), _mod_slice(m, 1))
    gu = _dot(h.astype(BF16), w_ref[...])
    gate_ref[...] = gu[:, :D]
    u_ref[...] = gu[:, D:]


def _rg_in(st, x, mods, layer, g1, w_in):
    out = jax.ShapeDtypeStruct((st.seq, st.batch * D), F32)
    return pl.pallas_call(
        _rg_in_kernel,
        grid=(st.tiles,),
        in_specs=[_tok_spec(), st.mod_spec(layer), _const_spec((1, D)), _const_spec((D, 2 * D))],
        out_specs=[st.seq_major_spec(), st.seq_major_spec()],
        out_shape=[out, out],
        compiler_params=_cparams(("arbitrary",), VMEM_LIMIT_BYTES),
        name="rg_in",
    )(x, mods, g1, w_in)


def _softplus(z):
    return jnp.maximum(z, 0.0) + jnp.log1p(jnp.exp(-jnp.abs(z)))


def _rg_scan_kernel(u_ref, h0_ref, cw_ref, cb_ref, wg_ref, bg_ref, lam_ref, h_ref, fin_ref,
                    upad, a_f, b_f, a_b, b_b, *, seq):
    cc = u_ref.shape[-1]
    pad_hi = CONV_W - 1 - CONV_LEFT
    upad[0:CONV_LEFT] = jnp.zeros((CONV_LEFT, SUBLANES, cc), F32)
    upad[CONV_LEFT:CONV_LEFT + seq] = u_ref[...]
    upad[CONV_LEFT + seq:CONV_LEFT + seq + pad_hi] = jnp.zeros((pad_hi, SUBLANES, cc), F32)
    half_c_sp = (-0.5 * LRU_C) * _softplus(-lam_ref[...])
    cw = cw_ref[...]
    rows = SCAN_TC * SUBLANES

    def coef(c, carry):
        t0 = pl.multiple_of(c * SCAN_TC, SCAN_TC)
        uc = cb_ref[...] + cw[0:1] * upad[pl.ds(t0, SCAN_TC)]
        for k in range(1, CONV_W):
            uc = uc + cw[k:k + 1] * upad[pl.ds(t0 + k, SCAN_TC)]
        u2 = uc.reshape(rows, cc)
        gth = jnp.tanh(_dot(u2.astype(BF16), wg_ref[...]) + bg_ref[...])
        hu2 = 0.5 * u2
        for d, (a_s, b_s) in enumerate(((a_f, b_f), (a_b, b_b))):
            r_th = gth[:, (2 * d) * cc:(2 * d + 1) * cc]
            i_th = gth[:, (2 * d + 1) * cc:(2 * d + 2) * cc]
            log_a = half_c_sp[d] * r_th + half_c_sp[d]
            a = jnp.exp(log_a)
            one_minus_a2 = 1.0 - a * a
            a_s[pl.ds(t0, SCAN_TC)] = a.reshape(SCAN_TC, SUBLANES, cc)
            b_s[pl.ds(t0, SCAN_TC)] = (jnp.sqrt(one_minus_a2) * (i_th * hu2 + hu2)).reshape(SCAN_TC, SUBLANES, cc)
        return carry

    lax.fori_loop(0, seq // SCAN_TC, coef, 0)

    def step(i, carry):
        hf, hb = carry
        t0 = 2 * i
        a0, b0, a1, b1 = a_f[t0], b_f[t0], a_f[t0 + 1], b_f[t0 + 1]
        b_f[t0] = a0 * hf + b0
        hf = (a1 * a0) * hf + (a1 * b0 + b1)
        b_f[t0 + 1] = hf
        s0 = seq - 1 - t0
        c0, d0, c1, d1 = a_b[s0], b_b[s0], a_b[s0 - 1], b_b[s0 - 1]
        b_b[s0] = c0 * hb + d0
        hb = (c1 * c0) * hb + (c1 * d0 + d1)
        b_b[s0 - 1] = hb
        return hf, hb

    hf, hb = lax.fori_loop(0, seq // 2, step, (h0_ref[0], h0_ref[1]), unroll=4)
    fin_ref[0] = hf
    fin_ref[1] = hb
    h_ref[...] = b_f[...] + b_b[...]


def _blockdiag_pairs(w):
    per = SCAN_CC // RNN_BLOCK
    w4 = w.reshape(D // SCAN_CC, per, RNN_BLOCK, RNN_BLOCK)
    eye = jnp.eye(per, dtype=w.dtype)
    return jnp.einsum('cipq,ij->cipjq', w4, eye).reshape(D // SCAN_CC, SCAN_CC, SCAN_CC)


def _rg_scan(st, u, h0, conv_w, conv_b, w_a, b_a, w_x, b_x, lam):
    seq, batch = st.seq, st.batch
    n_cc = D // SCAN_CC
    wg = (0.5 * jnp.concatenate([_blockdiag_pairs(w_a[0]), _blockdiag_pairs(w_x[0]),
                                 _blockdiag_pairs(w_a[1]), _blockdiag_pairs(w_x[1])], axis=-1)).astype(BF16)
    bg = 0.5 * jnp.concatenate([b.reshape(n_cc, 1, SCAN_CC) for b in (b_a[0], b_x[0], b_a[1], b_x[1])], axis=-1)
    blk = (seq, SUBLANES, SCAN_CC)
    scr = pltpu.VMEM(blk, F32)
    h, fin = pl.pallas_call(
        functools.partial(_rg_scan_kernel, seq=seq),
        grid=(batch // SUBLANES, n_cc),
        in_specs=[pl.BlockSpec(blk, lambda b, c: (0, b, c)),
                  pl.BlockSpec((2, SUBLANES, SCAN_CC), lambda b, c: (0, b, c)),
                  pl.BlockSpec((CONV_W, 1, SCAN_CC), lambda b, c: (0, 0, c)),
                  pl.BlockSpec((1, 1, SCAN_CC), lambda b, c: (0, 0, c)),
                  pl.BlockSpec((None, SCAN_CC, 4 * SCAN_CC), lambda b, c: (c, 0, 0)),
                  pl.BlockSpec((None, 1, 4 * SCAN_CC), lambda b, c: (c, 0, 0)),
                  pl.BlockSpec((2, 1, SCAN_CC), lambda b, c: (0, 0, c))],
        out_specs=[pl.BlockSpec(blk, lambda b, c: (0, b, c)),
                   pl.BlockSpec((2, SUBLANES, SCAN_CC), lambda b, c: (0, b, c))],
        out_shape=[jax.ShapeDtypeStruct((seq, batch, D), F32),
                   jax.ShapeDtypeStruct((2, batch, D), F32)],
        scratch_shapes=[pltpu.VMEM((seq + CONV_W - 1, SUBLANES, SCAN_CC), F32), scr, scr, scr, scr],
        compiler_params=_cparams(("arbitrary", "arbitrary"), VMEM_LIMIT_BYTES),
        name="rg_scan",
    )(u.reshape(seq, batch, D), h0, conv_w.reshape(CONV_W, 1, D), conv_b.reshape(1, 1, D),
      wg, bg, lam.reshape(2, 1, D))
    return h.reshape(seq, batch * D), fin


def _route_and_pack(k, x, m, g2_ref, wr_ref, xo_ref, xnb_ref, aff_ref, slab_s):
    n_ch = D // LANES
    xo_ref[k * LANES:(k + 1) * LANES, :] = x
    xn = _norm_mod(x, g2_ref[...], _mod_slice(m, 3), _mod_slice(m, 4))
    for j in range(n_ch):
        slab_s[pl.ds(k * LANES * n_ch + j, LANES, stride=n_ch), :] = xn[:, j * LANES:(j + 1) * LANES]
    rows = slice(k * LANES * n_ch, (k + 1) * LANES * n_ch)
    xnb_ref[rows, :] = slab_s[rows, :].astype(BF16)
    x_hi = xn.astype(BF16)
    x_lo = (xn - x_hi.astype(F32)).astype(BF16)
    wr = wr_ref[...]
    l1 = _qk(wr, x_hi)
    l2 = _qk(wr, x_lo)
    e = N_EXPERTS
    logit = l1[0:e] + l1[e:2 * e] + l1[2 * e:3 * e] + l2[0:e] + l2[e:2 * e]
    mx = jnp.max(logit, axis=0, keepdims=True)
    ex = jnp.exp(logit - mx)
    aff_ref[k] = ex / jnp.sum(ex, axis=0, keepdims=True)


def _lin_out_kernel(a_ref, w_ref, x_ref, mod_ref, g2_ref, wr_ref, xo_ref, xnb_ref, aff_ref, slab_s):
    m = mod_ref[...]
    for k in range(TOK_TILE // LANES):
        rows = slice(k * LANES, (k + 1) * LANES)
        y = _dot(a_ref[rows, :], w_ref[...])
        x = x_ref[rows, :] + _mod_slice(m, 2) * y
        _route_and_pack(k, x, m, g2_ref, wr_ref, xo_ref, xnb_ref, aff_ref, slab_s)


def _lin_out_gated_kernel(h_ref, gate_ref, w_ref, x_ref, mod_ref, g2_ref, wr_ref, xo_ref, xnb_ref, aff_ref,
                          slab_s):
    m = mod_ref[...]
    for k in range(TOK_TILE // LANES):
        rows = slice(k * LANES, (k + 1) * LANES)
        a = (h_ref[rows, :] * jax.nn.gelu(gate_ref[rows, :])).astype(BF16)
        y = _dot(a, w_ref[...])
        x = x_ref[rows, :] + _mod_slice(m, 2) * y
        _route_and_pack(k, x, m, g2_ref, wr_ref, xo_ref, xnb_ref, aff_ref, slab_s)


def _router_pieces(w_router):
    hi = w_router.astype(BF16)
    r1 = w_router - hi.astype(F32)
    mid = r1.astype(BF16)
    lo = (r1 - mid.astype(F32)).astype(BF16)
    return jnp.concatenate([hi, mid, lo], axis=1).T


def _lin_out(st, srcs, w, x, mods, layer, g2, wr, gated):
    n_chunks = st.n_tok // LANES
    if gated:
        body, src_specs = _lin_out_gated_kernel, [st.seq_major_spec(), st.seq_major_spec()]
    else:
        body, src_specs = _lin_out_kernel, [_tok_spec()]
    return pl.pallas_call(
        body,
        grid=(st.tiles,),
        in_specs=src_specs + [_const_spec((D, D)), _tok_spec(), st.mod_spec(layer), _const_spec((1, D)),
                              _const_spec((3 * N_EXPERTS, D))],
        out_specs=[_tok_spec(), pl.BlockSpec((TOK_TILE * (D // LANES), LANES), lambda i: (i, 0)),
                   pl.BlockSpec((TOK_TILE // LANES, N_EXPERTS, LANES), lambda i: (i, 0, 0))],
        out_shape=[jax.ShapeDtypeStruct((st.n_tok, D), F32),
                   jax.ShapeDtypeStruct((st.n_tok * (D // LANES), LANES), BF16),
                   jax.ShapeDtypeStruct((n_chunks, N_EXPERTS, LANES), F32)],
        scratch_shapes=[pltpu.VMEM((TOK_TILE * (D // LANES), LANES), F32)],
        compiler_params=_cparams(("arbitrary",), VMEM_LIMIT_BYTES),
        name="lin_out_gated" if gated else "lin_out",
    )(*srcs, w, x, mods, g2, wr)


def _select_kernel(a_ref, idx_ref, g_ref, linc_s, cnt_s, crow_s, *, cap):
    n_e, n_ch, _ = a_ref.shape
    assert n_ch & (n_ch - 1) == 0 and n_ch <= LANES
    rows = n_e * n_ch
    a3 = a_ref[...]
    capf = jnp.float32(cap)

    def count(mask3):
        c = jnp.sum(mask3.astype(F32), axis=2, keepdims=True)
        return jnp.sum(c, axis=1, keepdims=True)

    def as_f32(bits):
        return pltpu.bitcast(bits, F32)

    def search(i, thr):
        cand = thr | (jnp.int32(1) << (30 - i))
        return jnp.where(count(a3 >= as_f32(cand)) >= capf, cand, thr)

    thr = lax.fori_loop(0, 31, search, jnp.zeros((n_e, 1, LANES), I32))
    gt3 = a3 >= as_f32(thr + 1)
    eq3 = jnp.logical_and(a3 >= as_f32(thr), jnp.logical_not(gt3))
    need = capf - count(gt3)

    li = lax.broadcasted_iota(I32, (LANES, LANES), 0)
    lj = lax.broadcasted_iota(I32, (LANES, LANES), 1)
    upper = (li <= lj).astype(BF16)
    ones = jnp.ones((LANES, LANES), BF16)
    ri = lax.broadcasted_iota(I32, (rows, rows), 0)
    rj = lax.broadcasted_iota(I32, (rows, rows), 1)
    sh = n_ch.bit_length() - 1
    before = (((ri >> sh) == (rj >> sh)) & (rj < ri)).astype(BF16)

    def prefixes(mask3):
        x = mask3.reshape(rows, LANES).astype(BF16)
        local = _dot(x, upper)
        tot = _dot(x, ones)
        return local, tot, _dot(before, tot.astype(BF16))

    gt_l, gt_t, gt_x = prefixes(gt3)
    eq_l, eq_t, eq_x = prefixes(eq3)
    need_r = jnp.broadcast_to(need, (n_e, n_ch, LANES)).reshape(rows, LANES)
    sel_incl = gt_x + gt_l + jnp.minimum(eq_x + eq_l, need_r)
    sel_x = gt_x + jnp.minimum(eq_x, need_r)
    sel_c = gt_x + gt_t + jnp.minimum(eq_x + eq_t, need_r)
    linc_s[...] = sel_incl - sel_x
    cnt_s[...] = sel_c - sel_x
    c3 = sel_c.reshape(n_e, n_ch, LANES)
    pick = lax.broadcasted_iota(I32, (1, n_ch, LANES), 1) == lax.broadcasted_iota(I32, (1, n_ch, LANES), 2)
    crow = jnp.sum(jnp.where(pick, c3, 0.0), axis=1)
    crow = jnp.where(lax.broadcasted_iota(I32, crow.shape, 1) < n_ch, crow, jnp.float32(2 * cap + n_ch * LANES))
    crow_s[...] = crow

    slot = lax.broadcasted_iota(I32, (cap, LANES), 0).astype(F32)
    lane = lax.broadcasted_iota(I32, (cap, LANES), 1).astype(F32)
    diag = lax.broadcasted_iota(I32, (LANES, LANES), 0) == lax.broadcasted_iota(I32, (LANES, LANES), 1)
    zpad = jnp.zeros((LANES - n_ch, LANES), BF16)

    def per_expert(e, carry):
        r0 = pl.multiple_of(e * n_ch, n_ch)
        passed = (crow_s[pl.ds(e, 1), :] <= slot).astype(BF16)
        cnt_e = cnt_s[pl.ds(r0, n_ch), :].astype(BF16)
        linc_e = linc_s[pl.ds(r0, n_ch), :].astype(BF16)
        a_e = a_ref[e]
        a_hi = a_e.astype(BF16)
        r1 = a_e - a_hi.astype(F32)
        a_mid = r1.astype(BF16)
        a_lo = (r1 - a_mid.astype(F32)).astype(BF16)
        zpad_w = jnp.zeros((LANES - n_ch, 4 * LANES), BF16)
        by_passed = jnp.concatenate([jnp.concatenate([ones[:n_ch], cnt_e], axis=1), zpad_w[:, :2 * LANES]], axis=0)
        by_chunk = jnp.concatenate([jnp.concatenate([linc_e, a_hi, a_mid, a_lo], axis=1), zpad_w], axis=0)
        res = _dot(passed, by_passed)
        chunk = res[:, :LANES]
        rank = slot - res[:, LANES:]
        onehot = (chunk == lane).astype(BF16)
        res = _dot(onehot, by_chunk)
        g = res[:, :LANES]
        within = _dot((g <= rank).astype(BF16), ones)
        tok = chunk * jnp.float32(LANES) + within
        arow = res[:, LANES:2 * LANES] + res[:, 2 * LANES:3 * LANES] + res[:, 3 * LANES:]
        gate = jnp.sum(jnp.where(lane == within, arow, 0.0), axis=1, keepdims=True)
        gate = jnp.broadcast_to(gate, (cap, LANES))

        def to_row(col):
            pieces = []
            for b in range(cap // LANES):
                blk = col[b * LANES:(b + 1) * LANES, :]
                pieces.append(jnp.sum(jnp.where(diag, blk, 0.0), axis=0, keepdims=True))
            return jnp.concatenate(pieces, axis=1)

        idx_ref[pl.ds(e, 1), :] = to_row(tok).astype(I32)
        g_ref[pl.ds(e, 1), :] = to_row(gate)
        return carry

    lax.fori_loop(0, n_e, per_expert, 0)


def _select(aff_chunks, cap):
    n_ch = aff_chunks.shape[0]
    a = jnp.transpose(aff_chunks, (1, 0, 2))
    rows = N_EXPERTS * n_ch
    return pl.pallas_call(
        functools.partial(_select_kernel, cap=cap),
        out_shape=[jax.ShapeDtypeStruct((N_EXPERTS, cap), I32),
                   jax.ShapeDtypeStruct((N_EXPERTS, cap), F32)],
        scratch_shapes=[pltpu.VMEM((rows, LANES), F32), pltpu.VMEM((rows, LANES), F32),
                        pltpu.VMEM((N_EXPERTS, LANES), F32)],
        compiler_params=pltpu.CompilerParams(vmem_limit_bytes=VMEM_LIMIT_BYTES),
        name="select",
    )(a)


ROW_CH = D // LANES
GATHER_UNROLL = 16
FFN_ROWS = 512


def _cm_stride(m):
    return m + SUBLANES


def _moe_ffn_kernel(idx_ref, src_ref, g_ref, wg_ref, wu_ref, wd_ref, o_ref, tile_s, xe_s, *, cap):
    e = pl.program_id(0)
    f = pl.program_id(1)
    stride = _cm_stride(cap)

    @pl.when(f == 0)
    def _gather():
        def body(gi, carry):
            base = pl.multiple_of(gi * GATHER_UNROLL, GATHER_UNROLL)
            for k in range(GATHER_UNROLL):
                n = idx_ref[e * cap + base + k]
                pair = src_ref[n >> 1].astype(F32)
                odd = (jnp.full((ROW_CH, LANES), n, I32) & 1) == 1
                slab = jnp.where(odd, pair[ROW_CH:], pair[:ROW_CH])
                tile_s[pl.ds(base + k, ROW_CH, stride=stride), :] = slab
            return carry

        lax.fori_loop(0, cap // GATHER_UNROLL, body, 0)
        for j in range(ROW_CH):
            xe_s[:, j * LANES:(j + 1) * LANES] = tile_s[pl.ds(j * stride, cap), :].astype(BF16)
        o_ref[...] = jnp.zeros(o_ref.shape, F32)

    wg = wg_ref[...].astype(BF16)
    wu = wu_ref[...].astype(BF16)
    wd = wd_ref[...].astype(BF16)
    last = pl.num_programs(1) - 1
    tm = min(FFN_ROWS, cap)
    for t in range(cap // tm):
        r0 = t * tm
        xt = xe_s[r0:r0 + tm, :]
        hg = _dot(xt, wg)
        hu = _dot(xt, wu)
        h = (hg * jax.nn.sigmoid(hg) * hu).astype(BF16)
        y = _dot(h, wd)
        gt = jnp.where(f == last, g_ref[r0:r0 + tm, :], 1.0)
        for j in range(ROW_CH):
            rows = pl.ds(j * stride + r0, tm)
            o_ref[rows, :] = (o_ref[rows, :] + y[:, j * LANES:(j + 1) * LANES]) * gt


def _moe_ffn(idx, gates, xnb, w_gate, w_up, w_down, layer, cap):
    n_tok = xnb.shape[0] // ROW_CH
    ff = w_gate.shape[-1]
    stride = _cm_stride(cap)
    src = xnb.reshape(n_tok // 2, 2 * ROW_CH, LANES)
    grid_spec = pltpu.PrefetchScalarGridSpec(
        num_scalar_prefetch=1,
        grid=(N_EXPERTS, ff // FF_CHUNK),
        in_specs=[pl.BlockSpec((n_tok // 2, 2 * ROW_CH, LANES), lambda e, f, idx: (0, 0, 0),
                               pipeline_mode=pl.Buffered(1)),
                  pl.BlockSpec((None, cap, 1), lambda e, f, idx: (e, 0, 0)),
                  pl.BlockSpec((None, None, D, FF_CHUNK), lambda e, f, idx: (layer, e, 0, f)),
                  pl.BlockSpec((None, None, D, FF_CHUNK), lambda e, f, idx: (layer, e, 0, f)),
                  pl.BlockSpec((None, None, FF_CHUNK, D), lambda e, f, idx: (layer, e, f, 0))],
        out_specs=pl.BlockSpec((None, ROW_CH * stride, LANES), lambda e, f, idx: (e, 0, 0)),
        scratch_shapes=[pltpu.VMEM((ROW_CH * stride, LANES), F32), pltpu.VMEM((cap, D), BF16)],
    )
    return pl.pallas_call(
        functools.partial(_moe_ffn_kernel, cap=cap),
        grid_spec=grid_spec,
        out_shape=jax.ShapeDtypeStruct((N_EXPERTS, ROW_CH * stride, LANES), F32),
        compiler_params=_cparams(("arbitrary", "arbitrary"), VMEM_LIMIT_BYTES),
        name="moe_ffn",
    )(idx, src, gates.reshape(N_EXPERTS, cap, 1), w_gate, w_up, w_down)


COMBINE_VMEM_LIMIT_BYTES = 60 * 1024 * 1024


def _combine_kernel(idx_ref, ye_ref, x_ref, mod_ref, o_ref, acc_s, *, cap):
    s = pl.program_id(0)
    stride = _cm_stride(cap)

    @pl.when(s == 0)
    def _zero():
        acc_s[...] = jnp.zeros(acc_s.shape, F32)

    @pl.when(s < N_EXPERTS)
    def _scatter():
        def body(gi, carry):
            base = pl.multiple_of(gi * GATHER_UNROLL, GATHER_UNROLL)
            rows, sums = [], []
            for k in range(GATHER_UNROLL):
                r = pl.multiple_of(idx_ref[s * cap + base + k] * ROW_CH, ROW_CH)
                rows.append(r)
                sums.append(acc_s[pl.ds(r, ROW_CH), :] + ye_ref[pl.ds(base + k, ROW_CH, stride=stride), :])
            for r, v in zip(rows, sums):
                acc_s[pl.ds(r, ROW_CH), :] = v
            return carry

        lax.fori_loop(0, cap // GATHER_UNROLL, body, 0)

    @pl.when(s >= N_EXPERTS)
    def _residual():
        r0 = pl.multiple_of((s - N_EXPERTS) * (TOK_TILE * ROW_CH), TOK_TILE * ROW_CH)
        y = jnp.concatenate([acc_s[pl.ds(r0 + j, TOK_TILE, stride=ROW_CH), :] for j in range(ROW_CH)], axis=1)
        o_ref[...] = x_ref[...] + _mod_slice(mod_ref[...], 5) * y


def _combine(st, idx, ye, x, mods, layer, cap):
    stride = _cm_stride(cap)
    n_e = N_EXPERTS

    def tile(s):
        return jnp.maximum(s - n_e, 0)

    grid_spec = pltpu.PrefetchScalarGridSpec(
        num_scalar_prefetch=1,
        grid=(n_e + st.tiles,),
        in_specs=[pl.BlockSpec((None, ROW_CH * stride, LANES), lambda s, idx: (jnp.minimum(s, n_e - 1), 0, 0)),
                  pl.BlockSpec((TOK_TILE, D), lambda s, idx: (tile(s), 0)),
                  pl.BlockSpec((None, None, 1, N_MOD * D), lambda s, idx: (layer, st.mod_row(tile(s)), 0, 0))],
        out_specs=pl.BlockSpec((TOK_TILE, D), lambda s, idx: (tile(s), 0)),
        scratch_shapes=[pltpu.VMEM((st.n_tok * ROW_CH, LANES), F32)],
    )
    return pl.pallas_call(
        functools.partial(_combine_kernel, cap=cap),
        grid_spec=grid_spec,
        out_shape=jax.ShapeDtypeStruct((st.n_tok, D), F32),
        compiler_params=_cparams(("arbitrary",), COMBINE_VMEM_LIMIT_BYTES),
        name="moe_combine",
    )(idx, ye, x, mods)


def _moe(st, x, xnb, aff_chunks, mods, layer, w_gate, w_up, w_down):
    cap = EC_FACTOR * st.n_tok // N_EXPERTS
    idx, gates = _select(aff_chunks, cap)
    idx = idx.reshape(N_EXPERTS * cap)
    ye = _moe_ffn(idx, gates, xnb, w_gate, w_up, w_down, layer, cap)
    return _combine(st, idx, ye, x, mods, layer, cap)


def _head_sumsq(x, bd):
    x2 = x * x
    hi, lo = _split2(x2)
    w = bd.shape[0]
    cols = []
    for c in range(x.shape[1] // w):
        sl = slice(c * w, (c + 1) * w)
        cols.append(_dot(hi[:, sl], bd) + _dot(lo[:, sl], bd))
    return cols[0] if len(cols) == 1 else jnp.concatenate(cols, axis=1)


def _qk_norm(x, gain, bd):
    ms = _head_sumsq(x, bd) * (1.0 / HEAD_DIM)
    return x * lax.rsqrt(ms + EPS) * gain


def _rope(x, cos, sin_dn, sin_up):
    n = x.shape[1]
    q = HEAD_DIM // 4
    return x * cos + pltpu.roll(x, n - q, 1) * sin_dn + pltpu.roll(x, q, 1) * sin_up


def _qkv_ctx_kernel(x_ref, mod_ref, g_ref, w_ref, qg_ref, kg_ref, bd_ref, q_ref, k_ref, v_ref, kc_ref, vc_ref):
    m = mod_ref[...]
    h = _norm_mod(x_ref[...], g_ref[...], _mod_slice(m, 0), _mod_slice(m, 1))
    qkv = _dot(h.astype(BF16), w_ref[...])
    nq, nk = N_HEADS * HEAD_DIM, N_KV * HEAD_DIM
    bd = bd_ref[...]
    q = _qk_norm(qkv[:, :nq], qg_ref[...], bd)
    k = _qk_norm(qkv[:, nq:nq + nk], kg_ref[...], bd)
    v = qkv[:, nq + nk:]
    q_ref[...] = (q * (HEAD_DIM ** -0.5)).astype(BF16)
    k_ref[...] = k.astype(BF16)
    v_ref[...] = v.astype(BF16)
    kc_ref[...] = k
    vc_ref[...] = v


def _qkv_lat_kernel(x_ref, mod_ref, g_ref, w_ref, qg_ref, kg_ref, bd_ref,
                    cq_ref, sdq_ref, suq_ref, ck_ref, sdk_ref, suk_ref, q_ref, k_ref, v_ref):
    m = mod_ref[...]
    h = _norm_mod(x_ref[...], g_ref[...], _mod_slice(m, 0), _mod_slice(m, 1))
    qkv = _dot(h.astype(BF16), w_ref[...])
    nq, nk = N_HEADS * HEAD_DIM, N_KV * HEAD_DIM
    bd = bd_ref[...]
    q = _qk_norm(qkv[:, :nq], qg_ref[...], bd)
    k = _qk_norm(qkv[:, nq:nq + nk], kg_ref[...], bd)
    q = _rope(q, cq_ref[...], sdq_ref[...], suq_ref[...])
    k = _rope(k, ck_ref[...], sdk_ref[...], suk_ref[...])
    q_ref[...] = (q * (HEAD_DIM ** -0.5)).astype(BF16)
    k_ref[...] = k.astype(BF16)
    v_ref[...] = qkv[:, nq + nk:].astype(BF16)


def _rope_tables(seq, n_heads):
    n_rows = seq // GRID_W
    row = jnp.repeat(jnp.arange(n_rows), GRID_W).astype(F32)
    col = jnp.tile(jnp.arange(GRID_W), n_rows).astype(F32)
    n_freq = HEAD_DIM // 4
    inv = ROPE_BASE ** (-jnp.arange(n_freq, dtype=F32) / n_freq)
    ar, ac = row[:, None] * inv, col[:, None] * inv
    ang = jnp.concatenate([ar, ar, ac, ac], axis=-1)
    cos, sin = jnp.cos(ang), jnp.sin(ang)
    even = ((jnp.arange(HEAD_DIM) // n_freq) % 2 == 0).astype(F32)
    sin_dn = -sin * even
    sin_up = sin * (1.0 - even)
    return tuple(jnp.tile(a, (1, n_heads)) for a in (cos, sin_dn, sin_up))


def _qkv(st, x, mods, layer, g1, w_qkv, q_gain, k_gain, rope):
    nq, nk = N_HEADS * HEAD_DIM, N_KV * HEAD_DIM
    bd = jnp.asarray(np.kron(np.eye(4, dtype=np.float32), np.ones((HEAD_DIM, HEAD_DIM), np.float32))).astype(BF16)
    qg = jnp.tile(q_gain.reshape(1, HEAD_DIM), (1, N_HEADS))
    kg = jnp.tile(k_gain.reshape(1, HEAD_DIM), (1, N_KV))
    base_specs = [_tok_spec(), st.mod_spec(layer), _const_spec((1, D)), _const_spec((D, nq + 2 * nk)),
                  _const_spec((1, nq)), _const_spec((1, nk)), _const_spec((4 * HEAD_DIM, 4 * HEAD_DIM))]
    outs = [jax.ShapeDtypeStruct((st.n_tok, nq), BF16), jax.ShapeDtypeStruct((st.n_tok, nk), BF16),
            jax.ShapeDtypeStruct((st.n_tok, nk), BF16)]
    out_specs = [_tok_spec(nq), _tok_spec(nk), _tok_spec(nk)]
    if rope:
        tps = st.tiles_per_seq
        tq, tk = _rope_tables(st.seq, N_HEADS), _rope_tables(st.seq, N_KV)
        tab_specs = ([pl.BlockSpec((TOK_TILE, nq), lambda i: (i % tps, 0))] * 3
                     + [pl.BlockSpec((TOK_TILE, nk), lambda i: (i % tps, 0))] * 3)
        return pl.pallas_call(
            _qkv_lat_kernel, grid=(st.tiles,), in_specs=base_specs + tab_specs, out_specs=out_specs,
            out_shape=outs, compiler_params=_cparams(("arbitrary",), VMEM_LIMIT_BYTES), name="qkv_latent",
        )(x, mods, g1, w_qkv, qg, kg, bd, *tq, *tk)
    outs += [jax.ShapeDtypeStruct((st.n_tok, nk), F32)] * 2
    out_specs += [_tok_spec(nk)] * 2
    return pl.pallas_call(
        _qkv_ctx_kernel, grid=(st.tiles,), in_specs=base_specs, out_specs=out_specs,
        out_shape=outs, compiler_params=_cparams(("arbitrary",), VMEM_LIMIT_BYTES), name="qkv_context",
    )(x, mods, g1, w_qkv, qg, kg, bd)


def _softmax_av(s, v, sink):
    mx = jnp.maximum(jnp.max(s, axis=1, keepdims=True), sink)
    p = jnp.exp(s - mx)
    den = jnp.sum(p, axis=1, keepdims=True) + jnp.exp(sink - mx)
    return _dot(p.astype(BF16), v) / den


def _qk(q, k):
    return lax.dot_general(q, k, (((1,), (1,)), ((), ())), preferred_element_type=F32)


def _gqa_attend(q, k, v, sink_ref, ok):
    nq = q.shape[0]
    row = lax.broadcasted_iota(I32, (GQA_G * nq, 1), 0)
    outs = []
    for kv in range(N_KV):
        ks = slice(kv * HEAD_DIM, (kv + 1) * HEAD_DIM)
        h0 = kv * GQA_G
        qs = jnp.concatenate([q[:, (h0 + g) * HEAD_DIM:(h0 + g + 1) * HEAD_DIM] for g in range(GQA_G)], axis=0)
        sink = jnp.full((GQA_G * nq, 1), sink_ref[h0], F32)
        for g in range(1, GQA_G):
            sink = jnp.where(row >= g * nq, sink_ref[h0 + g], sink)
        s = _qk(qs, k[:, ks])
        if ok is not None:
            s = jnp.where(ok, s, -jnp.inf)
        o = _softmax_av(s, v[:, ks], sink)
        outs += [o[g * nq:(g + 1) * nq] for g in range(GQA_G)]
    return jnp.concatenate(outs, axis=1)


def _ctx_attn_kernel(sink_ref, q_ref, k_ref, v_ref, o_ref):
    o_ref[...] = _gqa_attend(q_ref[...], k_ref[...], v_ref[...], sink_ref, None).astype(BF16)


def _ctx_attn(st, q, k, v, sink):
    nq, nk = N_HEADS * HEAD_DIM, N_KV * HEAD_DIM
    seq = st.seq
    return pl.pallas_call(
        _ctx_attn_kernel,
        grid=(st.batch,),
        in_specs=[pl.BlockSpec(memory_space=pltpu.SMEM),
                  pl.BlockSpec((seq, nq), lambda b: (b, 0)),
                  pl.BlockSpec((seq, nk), lambda b: (b, 0)),
                  pl.BlockSpec((seq, nk), lambda b: (b, 0))],
        out_specs=pl.BlockSpec((seq, nq), lambda b: (b, 0)),
        out_shape=jax.ShapeDtypeStruct((st.n_tok, nq), BF16),
        compiler_params=_cparams(("arbitrary",), VMEM_LIMIT_BYTES),
        name="context_attention",
    )(sink, q, k, v)


def _lat_attn_kernel(sink_ref, q_ref, k_ref, v_ref, kc_ref, vc_ref, o_ref, *, n_blocks):
    j = pl.program_id(1)
    w = WINDOW
    jp = jnp.maximum(j - 1, 0)
    jn = jnp.minimum(j + 1, n_blocks - 1)

    def rows(ref, blk):
        return ref[pl.ds(pl.multiple_of(blk * w, w), w), :]

    kcat = jnp.concatenate([rows(k_ref, jp), rows(k_ref, j), rows(k_ref, jn), kc_ref[...].astype(BF16)], axis=0)
    vcat = jnp.concatenate([rows(v_ref, jp), rows(v_ref, j), rows(v_ref, jn), vc_ref[...].astype(BF16)], axis=0)
    n_keys = kcat.shape[0]
    qi = lax.broadcasted_iota(I32, (GQA_G * w, n_keys), 0) & (w - 1)
    ki = lax.broadcasted_iota(I32, (GQA_G * w, n_keys), 1)
    ok = (((ki < w) & (j > 0) & (ki >= qi))
          | ((ki >= w) & (ki < 2 * w))
          | ((ki >= 2 * w) & (ki < 3 * w) & (j < n_blocks - 1) & (ki - 2 * w <= qi))
          | (ki >= 3 * w))
    o_ref[...] = _gqa_attend(q_ref[...], kcat, vcat, sink_ref, ok).astype(BF16)


def _lat_attn(st, q, k, v, k_ctx, v_ctx, sink):
    nq, nk = N_HEADS * HEAD_DIM, N_KV * HEAD_DIM
    seq, past = st.seq, k_ctx.shape[0] // st.batch
    n_blocks = seq // WINDOW
    return pl.pallas_call(
        functools.partial(_lat_attn_kernel, n_blocks=n_blocks),
        grid=(st.batch, n_blocks),
        in_specs=[pl.BlockSpec(memory_space=pltpu.SMEM),
                  pl.BlockSpec((WINDOW, nq), lambda b, j: (b * n_blocks + j, 0)),
                  pl.BlockSpec((seq, nk), lambda b, j: (b, 0)),
                  pl.BlockSpec((seq, nk), lambda b, j: (b, 0)),
                  pl.BlockSpec((past, nk), lambda b, j: (b, 0)),
                  pl.BlockSpec((past, nk), lambda b, j: (b, 0))],
        out_specs=pl.BlockSpec((WINDOW, nq), lambda b, j: (b * n_blocks + j, 0)),
        out_shape=jax.ShapeDtypeStruct((st.n_tok, nq), BF16),
        compiler_params=_cparams(("arbitrary", "arbitrary"), VMEM_LIMIT_BYTES),
        name="latent_attention",
    )(sink, q, k, v, k_ctx, v_ctx)


def _fourier_kernel(x_ref, mod_ref, g_ref, cs_ref, ss_ref, cc_ref, sc_ref, o_ref, *, scale):
    m = mod_ref[...]
    h = _norm_mod(x_ref[...], g_ref[...], _mod_slice(m, 0), _mod_slice(m, 1)).astype(BF16)
    p = _dot(cs_ref[...], h).astype(BF16)
    q = _dot(ss_ref[...], h).astype(BF16)
    gw = FOURIER_GW
    cc, sc = cc_ref[...], sc_ref[...]
    outs = []
    for g in range(FOURIER_GROUPS):
        sl = slice(g * gw, (g + 1) * gw)
        outs.append(_dot(p[:, sl], cc) - _dot(q[:, sl], sc))
    o_ref[...] = (jnp.concatenate(outs, axis=1) * scale).astype(BF16)


def _dft_tables(n):
    k = jnp.arange(n, dtype=I32)
    ang = ((k[:, None] * k[None, :]) % n).astype(F32) * (2.0 * math.pi / n)
    return jnp.cos(ang).astype(BF16), jnp.sin(ang).astype(BF16)


def _fourier(st, x, mods, layer, g1):
    seq = st.seq
    cs, ss = _dft_tables(seq)
    cc, sc = _dft_tables(FOURIER_GW)
    mod_spec = pl.BlockSpec((None, None, 1, N_MOD * D), lambda b: (layer, 0 if st.shared else 1 + b, 0, 0))
    return pl.pallas_call(
        functools.partial(_fourier_kernel, scale=1.0 / math.sqrt(seq * FOURIER_GW)),
        grid=(st.batch,),
        in_specs=[pl.BlockSpec((seq, D), lambda b: (b, 0)), mod_spec, _const_spec((1, D)),
                  _const_spec((seq, seq)), _const_spec((seq, seq)),
                  _const_spec((FOURIER_GW, FOURIER_GW)), _const_spec((FOURIER_GW, FOURIER_GW))],
        out_specs=pl.BlockSpec((seq, D), lambda b: (b, 0)),
        out_shape=jax.ShapeDtypeStruct((st.n_tok, D), BF16),
        compiler_params=_cparams(("arbitrary",), VMEM_LIMIT_BYTES),
        name="fourier",
    )(x, mods, g1, cs, ss, cc, sc)


def kernel(x_prompt, x_sample, state_rglru, cache_k, cache_v, c, c_ctx, mod_w, mod_b, norm1_g, norm2_g,
           rg_w_in, rg_conv_w, rg_conv_b, rg_w_a, rg_b_a, rg_w_x, rg_b_x, rg_lambda, rg_w_out,
           at_w_qkv, at_q_norm, at_k_norm, at_sink, at_w_o, ft_w, moe_router, moe_w_gate, moe_w_up, moe_w_down):
    depth = mod_w.shape[0]
    batch, seq, _ = x_prompt.shape
    dec_batch, dec_seq, _ = x_sample.shape
    assert 1 + dec_batch <= MOD_ROWS
    streams = (_Stream(batch, seq, True), _Stream(dec_batch, dec_seq, False))
    cond = jnp.concatenate([c_ctx[None, :], c, jnp.zeros((MOD_ROWS - 1 - dec_batch, D), F32)], axis=0)
    mods = _modulation(cond, mod_w, mod_b).reshape(depth, MOD_ROWS, 1, N_MOD * D)

    xs = [x_prompt.reshape(batch * seq, D), x_sample.reshape(dec_batch * dec_seq, D)]
    new_rg, new_k, new_v = [], [], []
    n_mixers = 3
    for layer in range(depth):
        kind, j = layer % n_mixers, layer // n_mixers
        g1 = norm1_g[layer].reshape(1, D)
        g2 = norm2_g[layer].reshape(1, D)
        wr = _router_pieces(moe_router[layer])
        for si, st in enumerate(streams):
            x = xs[si]
            if kind == 0:
                gate, u = _rg_in(st, x, mods, layer, g1, rg_w_in[j].astype(BF16))
                if st.shared:
                    h0 = jnp.zeros((2, st.batch, D), F32)
                else:
                    h0 = jnp.transpose(state_rglru[:, j], (1, 0, 2))
                h, fin = _rg_scan(st, u, h0, rg_conv_w[j], rg_conv_b[j], rg_w_a[j], rg_b_a[j],
                                  rg_w_x[j], rg_b_x[j], rg_lambda[j])
                if st.shared:
                    new_rg.append(jnp.transpose(fin, (1, 0, 2)))
                x1, xnb, aff =_lin_out(st, (h, gate), rg_w_out[j].astype(BF16), x, mods, layer, g2, wr, True)
            elif kind == 1:
                w_qkv = at_w_qkv[j].astype(BF16)
                if st.shared:
                    q, k, v, kc, vc = _qkv(st, x, mods, layer, g1, w_qkv, at_q_norm[j], at_k_norm[j], False)
                    new_k.append(kc.reshape(st.batch, st.seq, N_KV, HEAD_DIM))
                    new_v.append(vc.reshape(st.batch, st.seq, N_KV, HEAD_DIM))
                    a = _ctx_attn(st, q, k, v, at_sink[j])
                else:
                    q, k, v = _qkv(st, x, mods, layer, g1, w_qkv, at_q_norm[j], at_k_norm[j], True)
                    nk = N_KV * HEAD_DIM
                    a = _lat_attn(st, q, k, v, cache_k[:, j].reshape(-1, nk), cache_v[:, j].reshape(-1, nk),
                                  at_sink[j])
                x1, xnb, aff =_lin_out(st, (a,), at_w_o[j].astype(BF16), x, mods, layer, g2, wr, False)
            else:
                a = _fourier(st, x, mods, layer, g1)
                x1, xnb, aff =_lin_out(st, (a,), ft_w[j].astype(BF16), x, mods, layer, g2, wr, False)
            xs[si] = _moe(st, x1, xnb, aff, mods, layer, moe_w_gate, moe_w_up, moe_w_down)
    return (xs[0].reshape(batch, seq, D), xs[1].reshape(dec_batch, dec_seq, D),
            jnp.stack(new_rg, axis=1), jnp.stack(new_k, axis=1), jnp.stack(new_v, axis=1))
```

```python
import functools
import math

import numpy as np
import jax
import jax.numpy as jnp
from jax import lax
from jax.experimental import pallas as pl
from jax.experimental.pallas import tpu as pltpu

F32 = jnp.float32
BF16 = jnp.bfloat16
I32 = jnp.int32
U32 = jnp.uint32

D = 1024
N_MOD = 6
EPS = 1e-6
GRID_W = 64
CONV_W = 4
CONV_LEFT = 2
LRU_C = 8.0
RNN_BLOCKS = 16
RNN_BLOCK = D // RNN_BLOCKS
N_HEADS = 16
N_KV = 4
HEAD_DIM = 64
GQA_G = N_HEADS // N_KV
WINDOW = 128
ROPE_BASE = 10000.0
FOURIER_GROUPS = 4
FOURIER_GW = D // FOURIER_GROUPS
N_EXPERTS = 16
EC_FACTOR = 2

LANES = 128
SUBLANES = 8
VMEM_LIMIT_BYTES = 56 * 1024 * 1024

TOK_TILE = 256
MOD_ROWS = 16
SCAN_CC = 128
SCAN_TC = 32
FF_CHUNK = 512


def _cparams(sem, vmem=None):
    return pltpu.CompilerParams(dimension_semantics=sem, vmem_limit_bytes=vmem)


def _split2(a):
    hi = a.astype(BF16)
    lo = (a - hi.astype(F32)).astype(BF16)
    return hi, lo


def _dot(a, b):
    return jnp.dot(a, b, preferred_element_type=F32)


def _dot3(a, b):
    a_hi, a_lo = _split2(a)
    b_hi, b_lo = _split2(b)
    return _dot(a_hi, b_hi) + _dot(a_hi, b_lo) + _dot(a_lo, b_hi)


def _norm_mod(x, g, shift, scale):
    ms = jnp.mean(x * x, axis=-1, keepdims=True)
    y = x * lax.rsqrt(ms + EPS) * g
    return y * (1.0 + scale) + shift


def _mod_slice(m, k):
    return m[:, k * D:(k + 1) * D]


class _Stream:
    def __init__(self, batch, seq, shared_cond):
        self.batch, self.seq, self.shared = batch, seq, shared_cond
        self.n_tok = batch * seq
        self.tiles = self.n_tok // TOK_TILE
        self.tiles_per_seq = seq // TOK_TILE

    def mod_row(self, i):
        return 0 if self.shared else 1 + i // self.tiles_per_seq

    def mod_spec(self, layer):
        return pl.BlockSpec((None, None, 1, N_MOD * D), lambda i: (layer, self.mod_row(i), 0, 0))

    def seq_major_spec(self):
        tps = self.tiles_per_seq
        return pl.BlockSpec((TOK_TILE, D), lambda i: (i % tps, i // tps))


def _tok_spec(width=D):
    return pl.BlockSpec((TOK_TILE, width), lambda i: (i, 0))


def _const_spec(shape):
    nd = len(shape)
    return pl.BlockSpec(shape, lambda i: (0,) * nd)


def _mod_kernel(c_ref, w_ref, b_ref, o_ref):
    c = c_ref[...]
    c = c * jax.nn.sigmoid(c)
    o_ref[...] = _dot3(c, w_ref[...]) + b_ref[...]


def _modulation(cond, mod_w, mod_b):
    depth = mod_w.shape[0]
    tn = N_MOD * D // 4
    return pl.pallas_call(
        _mod_kernel,
        grid=(depth, N_MOD * D // tn),
        in_specs=[pl.BlockSpec((MOD_ROWS, D), lambda l, n: (0, 0)),
                  pl.BlockSpec((None, D, tn), lambda l, n: (l, 0, n)),
                  pl.BlockSpec((None, 1, tn), lambda l, n: (l, 0, n))],
        out_specs=pl.BlockSpec((None, MOD_ROWS, tn), lambda l, n: (l, 0, n)),
        out_shape=jax.ShapeDtypeStruct((depth, MOD_ROWS, N_MOD * D), F32),
        compiler_params=_cparams(("arbitrary", "arbitrary"), VMEM_LIMIT_BYTES),
        name="modulation",
    )(cond, mod_w, mod_b.reshape(depth, 1, N_MOD * D))


def _rg_in_kernel(x_ref, mod_ref, g_ref, w_ref, gate_ref, u_ref):
    m = mod_ref[...]
    h = _norm_mod(x_ref[...], g_ref[...], _mod_slice(m, 0), _mod_slice(m, 1))
    gu = _dot(h.astype(BF16), w_ref[...])
    gate_ref[...] = gu[:, :D]
    u_ref[...] = gu[:, D:]


def _rg_in(st, x, mods, layer, g1, w_in):
    out = jax.ShapeDtypeStruct((st.seq, st.batch * D), F32)
    return pl.pallas_call(
        _rg_in_kernel,
        grid=(st.tiles,),
        in_specs=[_tok_spec(), st.mod_spec(layer), _const_spec((1, D)), _const_spec((D, 2 * D))],
        out_specs=[st.seq_major_spec(), st.seq_major_spec()],
        out_shape=[out, out],
        compiler_params=_cparams(("arbitrary",), VMEM_LIMIT_BYTES),
        name="rg_in",
    )(x, mods, g1, w_in)


def _softplus(z):
    return jnp.maximum(z, 0.0) + jnp.log1p(jnp.exp(-jnp.abs(z)))


def _rg_scan_kernel(u_ref, h0_ref, cw_ref, cb_ref, wg_ref, bg_ref, lam_ref, h_ref, fin_ref,
                    upad, a_f, b_f, a_b, b_b, *, seq):
    cc = u_ref.shape[-1]
    pad_hi = CONV_W - 1 - CONV_LEFT
    upad[0:CONV_LEFT] = jnp.zeros((CONV_LEFT, SUBLANES, cc), F32)
    upad[CONV_LEFT:CONV_LEFT + seq] = u_ref[...]
    upad[CONV_LEFT + seq:CONV_LEFT + seq + pad_hi] = jnp.zeros((pad_hi, SUBLANES, cc), F32)
    k2 = (-0.5 * LRU_C * math.log2(math.e)) * _softplus(-lam_ref[...])
    cw = cw_ref[...]
    rows = SCAN_TC * SUBLANES

    def coef(c, carry):
        t0 = pl.multiple_of(c * SCAN_TC, SCAN_TC)
        uc = cb_ref[...] + cw[0:1] * upad[pl.ds(t0, SCAN_TC)]
        for k in range(1, CONV_W):
            uc = uc + cw[k:k + 1] * upad[pl.ds(t0 + k, SCAN_TC)]
        u2 = uc.reshape(rows, cc)
        gth = jnp.tanh(_dot(u2.astype(BF16), wg_ref[...]) + bg_ref[...])
        hu2 = 0.5 * u2
        for d, (a_s, b_s) in enumerate(((a_f, b_f), (a_b, b_b))):
            r_th = gth[:, (2 * d) * cc:(2 * d + 1) * cc]
            i_th = gth[:, (2 * d + 1) * cc:(2 * d + 2) * cc]
            a = jnp.exp2(k2[d] * r_th + k2[d])
            q = 1.0 - a * a
            root = jnp.where(q > 0.0, q * lax.rsqrt(q), 0.0)
            a_s[pl.ds(t0, SCAN_TC)] = a.reshape(SCAN_TC, SUBLANES, cc)
            b_s[pl.ds(t0, SCAN_TC)] = (root * (i_th * hu2 + hu2)).reshape(SCAN_TC, SUBLANES, cc)
        return carry

    lax.fori_loop(0, seq // SCAN_TC, coef, 0)

    def step(i, carry):
        hf, hb = carry
        t0 = 2 * i
        a0, b0, a1, b1 = a_f[t0], b_f[t0], a_f[t0 + 1], b_f[t0 + 1]
        b_f[t0] = a0 * hf + b0
        hf = (a1 * a0) * hf + (a1 * b0 + b1)
        b_f[t0 + 1] = hf
        s0 = seq - 1 - t0
        c0, d0, c1, d1 = a_b[s0], b_b[s0], a_b[s0 - 1], b_b[s0 - 1]
        b_b[s0] = c0 * hb + d0
        hb = (c1 * c0) * hb + (c1 * d0 + d1)
        b_b[s0 - 1] = hb
        return hf, hb

    hf, hb = lax.fori_loop(0, seq // 2, step, (h0_ref[0], h0_ref[1]), unroll=4)
    fin_ref[0] = hf
    fin_ref[1] = hb
    h_ref[...] = b_f[...] + b_b[...]


def _blockdiag_pairs(w):
    per = SCAN_CC // RNN_BLOCK
    w4 = w.reshape(D // SCAN_CC, per, RNN_BLOCK, RNN_BLOCK)
    eye = jnp.eye(per, dtype=w.dtype)
    return jnp.einsum('cipq,ij->cipjq', w4, eye).reshape(D // SCAN_CC, SCAN_CC, SCAN_CC)


def _rg_scan(st, u, h0, conv_w, conv_b, w_a, b_a, w_x, b_x, lam):
    seq, batch = st.seq, st.batch
    n_cc = D // SCAN_CC
    wg = (0.5 * jnp.concatenate([_blockdiag_pairs(w_a[0]), _blockdiag_pairs(w_x[0]),
                                 _blockdiag_pairs(w_a[1]), _blockdiag_pairs(w_x[1])], axis=-1)).astype(BF16)
    bg = 0.5 * jnp.concatenate([b.reshape(n_cc, 1, SCAN_CC) for b in (b_a[0], b_x[0], b_a[1], b_x[1])], axis=-1)
    blk = (seq, SUBLANES, SCAN_CC)
    scr = pltpu.VMEM(blk, F32)
    h, fin = pl.pallas_call(
        functools.partial(_rg_scan_kernel, seq=seq),
        grid=(batch // SUBLANES, n_cc),
        in_specs=[pl.BlockSpec(blk, lambda b, c: (0, b, c)),
                  pl.BlockSpec((2, SUBLANES, SCAN_CC), lambda b, c: (0, b, c)),
                  pl.BlockSpec((CONV_W, 1, SCAN_CC), lambda b, c: (0, 0, c)),
                  pl.BlockSpec((1, 1, SCAN_CC), lambda b, c: (0, 0, c)),
                  pl.BlockSpec((None, SCAN_CC, 4 * SCAN_CC), lambda b, c: (c, 0, 0)),
                  pl.BlockSpec((None, 1, 4 * SCAN_CC), lambda b, c: (c, 0, 0)),
                  pl.BlockSpec((2, 1, SCAN_CC), lambda b, c: (0, 0, c))],
        out_specs=[pl.BlockSpec(blk, lambda b, c: (0, b, c)),
                   pl.BlockSpec((2, SUBLANES, SCAN_CC), lambda b, c: (0, b, c))],
        out_shape=[jax.ShapeDtypeStruct((seq, batch, D), F32),
                   jax.ShapeDtypeStruct((2, batch, D), F32)],
        scratch_shapes=[pltpu.VMEM((seq + CONV_W - 1, SUBLANES, SCAN_CC), F32), scr, scr, scr, scr],
        compiler_params=_cparams(("arbitrary", "arbitrary"), VMEM_LIMIT_BYTES),
        name="rg_scan",
    )(u.reshape(seq, batch, D), h0, conv_w.reshape(CONV_W, 1, D), conv_b.reshape(1, 1, D),
      wg, bg, lam.reshape(2, 1, D))
    return h.reshape(seq, batch * D), fin


LIN_CHUNK = LANES
LIN_CHUNKS = TOK_TILE // LIN_CHUNK


def _chunk_rows(k):
    return slice(k * LIN_CHUNK, (k + 1) * LIN_CHUNK)


def _route_and_pack(ys, m, x_ref, g2_ref, wr_ref, xo_ref, xnb_ref, aff_ref, slab_s):
    n_ch = D // LANES
    ks = range(LIN_CHUNKS)
    xs = [x_ref[_chunk_rows(k), :] + _mod_slice(m, 2) * ys[k] for k in ks]
    for k in ks:
        xo_ref[_chunk_rows(k), :] = xs[k]
    xns = [_norm_mod(x, g2_ref[...], _mod_slice(m, 3), _mod_slice(m, 4)) for x in xs]
    his = [xn.astype(BF16) for xn in xns]
    los = [(xn - hi.astype(F32)).astype(BF16) for xn, hi in zip(xns, his)]
    wr = wr_ref[...]
    l1s = [_qk(wr, hi) for hi in his]
    l2s = [_qk(wr, lo) for lo in los]
    for k in ks:
        for j in range(n_ch):
            slab_s[pl.ds(k * LIN_CHUNK * n_ch + j, LIN_CHUNK, stride=n_ch), :] = xns[k][:, j * LANES:(j + 1) * LANES]
    xnb_ref[...] = slab_s[...].astype(BF16).reshape(xnb_ref.shape)
    e = N_EXPERTS
    for k in ks:
        l1, l2 = l1s[k], l2s[k]
        logit = l1[0:e] + l1[e:2 * e] + l1[2 * e:3 * e] + l2[0:e] + l2[e:2 * e]
        mx = jnp.max(logit, axis=0, keepdims=True)
        ex = jnp.exp(logit - mx)
        aff_ref[k] = ex / jnp.sum(ex, axis=0, keepdims=True)


def _lin_out_kernel(a_ref, w_ref, x_ref, mod_ref, g2_ref, wr_ref, xo_ref, xnb_ref, aff_ref, slab_s):
    ys = [_dot(a_ref[_chunk_rows(k), :], w_ref[...]) for k in range(LIN_CHUNKS)]
    _route_and_pack(ys, mod_ref[...], x_ref, g2_ref, wr_ref, xo_ref, xnb_ref, aff_ref, slab_s)


def _lin_out_gated_kernel(h_ref, gate_ref, w_ref, x_ref, mod_ref, g2_ref, wr_ref, xo_ref, xnb_ref, aff_ref,
                          slab_s):
    acts = [(h_ref[_chunk_rows(k), :] * jax.nn.gelu(gate_ref[_chunk_rows(k), :])).astype(BF16)
            for k in range(LIN_CHUNKS)]
    ys = [_dot(a, w_ref[...]) for a in acts]
    _route_and_pack(ys, mod_ref[...], x_ref, g2_ref, wr_ref, xo_ref, xnb_ref, aff_ref, slab_s)


def _router_pieces(w_router):
    hi = w_router.astype(BF16)
    r1 = w_router - hi.astype(F32)
    mid = r1.astype(BF16)
    lo = (r1 - mid.astype(F32)).astype(BF16)
    return jnp.concatenate([hi, mid, lo], axis=1).T


def _lin_out(st, srcs, w, x, mods, layer, g2, wr, gated):
    n_chunks = st.n_tok // LANES
    if gated:
        body, src_specs = _lin_out_gated_kernel, [st.seq_major_spec(), st.seq_major_spec()]
    else:
        body, src_specs = _lin_out_kernel, [_tok_spec()]
    return pl.pallas_call(
        body,
        grid=(st.tiles,),
        in_specs=src_specs + [_const_spec((D, D)), _tok_spec(), st.mod_spec(layer), _const_spec((1, D)),
                              _const_spec((3 * N_EXPERTS, D))],
        out_specs=[_tok_spec(), pl.BlockSpec((TOK_TILE // 2, 2 * (D // LANES), LANES), lambda i: (i, 0, 0)),
                   pl.BlockSpec((TOK_TILE // LANES, N_EXPERTS, LANES), lambda i: (i, 0, 0))],
        out_shape=[jax.ShapeDtypeStruct((st.n_tok, D), F32),
                   jax.ShapeDtypeStruct((st.n_tok // 2, 2 * (D // LANES), LANES), BF16),
                   jax.ShapeDtypeStruct((n_chunks, N_EXPERTS, LANES), F32)],
        scratch_shapes=[pltpu.VMEM((TOK_TILE * (D // LANES), LANES), F32)],
        compiler_params=_cparams(("arbitrary",), VMEM_LIMIT_BYTES),
        name="lin_out_gated" if gated else "lin_out",
    )(*srcs, w, x, mods, g2, wr)


def _select_kernel(a_ref, idx_ref, g_ref, linc_s, cnt_s, crow_s, *, cap):
    n_e, n_ch, _ = a_ref.shape
    assert n_ch & (n_ch - 1) == 0 and n_ch <= LANES
    rows = n_e * n_ch
    a3 = a_ref[...]
    capf = jnp.float32(cap)

    def count(mask3):
        c = jnp.sum(mask3.astype(F32), axis=2, keepdims=True)
        return jnp.sum(c, axis=1, keepdims=True)

    def as_f32(bits):
        return pltpu.bitcast(bits, F32)

    def search(i, thr):
        cand = thr | (jnp.int32(1) << (30 - i))
        return jnp.where(count(a3 >= as_f32(cand)) >= capf, cand, thr)

    thr = lax.fori_loop(0, 31, search, jnp.zeros((n_e, 1, LANES), I32))
    gt3 = a3 >= as_f32(thr + 1)
    eq3 = jnp.logical_and(a3 >= as_f32(thr), jnp.logical_not(gt3))
    need = capf - count(gt3)

    li = lax.broadcasted_iota(I32, (LANES, LANES), 0)
    lj = lax.broadcasted_iota(I32, (LANES, LANES), 1)
    upper = (li <= lj).astype(BF16)
    ones = jnp.ones((LANES, LANES), BF16)
    ri = lax.broadcasted_iota(I32, (rows, rows), 0)
    rj = lax.broadcasted_iota(I32, (rows, rows), 1)
    sh = n_ch.bit_length() - 1
    before = (((ri >> sh) == (rj >> sh)) & (rj < ri)).astype(BF16)

    def prefixes(mask3):
        x = mask3.reshape(rows, LANES).astype(BF16)
        local = _dot(x, upper)
        tot = _dot(x, ones)
        return local, tot, _dot(before, tot.astype(BF16))

    gt_l, gt_t, gt_x = prefixes(gt3)
    eq_l, eq_t, eq_x = prefixes(eq3)
    need_r = jnp.broadcast_to(need, (n_e, n_ch, LANES)).reshape(rows, LANES)
    sel_incl = gt_x + gt_l + jnp.minimum(eq_x + eq_l, need_r)
    sel_x = gt_x + jnp.minimum(eq_x, need_r)
    sel_c = gt_x + gt_t + jnp.minimum(eq_x + eq_t, need_r)
    linc_s[...] = sel_incl - sel_x
    cnt_s[...] = sel_c - sel_x
    c3 = sel_c.reshape(n_e, n_ch, LANES)
    pick = lax.broadcasted_iota(I32, (1, n_ch, LANES), 1) == lax.broadcasted_iota(I32, (1, n_ch, LANES), 2)
    crow = jnp.sum(jnp.where(pick, c3, 0.0), axis=1)
    crow = jnp.where(lax.broadcasted_iota(I32, crow.shape, 1) < n_ch, crow, jnp.float32(2 * cap + n_ch * LANES))
    crow_s[...] = crow

    slot = lax.broadcasted_iota(I32, (cap, LANES), 0).astype(F32)
    lane = lax.broadcasted_iota(I32, (cap, LANES), 1).astype(F32)
    diag = lax.broadcasted_iota(I32, (LANES, LANES), 0) == lax.broadcasted_iota(I32, (LANES, LANES), 1)
    zpad = jnp.zeros((LANES - n_ch, LANES), BF16)

    def per_expert(e, carry):
        r0 = pl.multiple_of(e * n_ch, n_ch)
        passed = (crow_s[pl.ds(e, 1), :] <= slot).astype(BF16)
        cnt_e = cnt_s[pl.ds(r0, n_ch), :].astype(BF16)
        linc_e = linc_s[pl.ds(r0, n_ch), :].astype(BF16)
        a_e = a_ref[e]
        a_hi = a_e.astype(BF16)
        r1 = a_e - a_hi.astype(F32)
        a_mid = r1.astype(BF16)
        a_lo = (r1 - a_mid.astype(F32)).astype(BF16)
        zpad_w = jnp.zeros((LANES - n_ch, 4 * LANES), BF16)
        by_passed = jnp.concatenate([jnp.concatenate([ones[:n_ch], cnt_e], axis=1), zpad_w[:, :2 * LANES]], axis=0)
        by_chunk = jnp.concatenate([jnp.concatenate([linc_e, a_hi, a_mid, a_lo], axis=1), zpad_w], axis=0)
        res = _dot(passed, by_passed)
        chunk = res[:, :LANES]
        rank = slot - res[:, LANES:]
        onehot = (chunk == lane).astype(BF16)
        res = _dot(onehot, by_chunk)
        g = res[:, :LANES]
        within = _dot((g <= rank).astype(BF16), ones)
        tok = chunk * jnp.float32(LANES) + within
        arow = res[:, LANES:2 * LANES] + res[:, 2 * LANES:3 * LANES] + res[:, 3 * LANES:]
        gate = jnp.sum(jnp.where(lane == within, arow, 0.0), axis=1, keepdims=True)
        gate = jnp.broadcast_to(gate, (cap, LANES))

        def to_row(col):
            pieces = []
            for b in range(cap // LANES):
                blk = col[b * LANES:(b + 1) * LANES, :]
                pieces.append(jnp.sum(jnp.where(diag, blk, 0.0), axis=0, keepdims=True))
            return jnp.concatenate(pieces, axis=1)

        idx_ref[pl.ds(e, 1), :] = to_row(tok).astype(I32)
        g_ref[pl.ds(e, 1), :] = to_row(gate)
        return carry

    lax.fori_loop(0, n_e, per_expert, 0)


def _select(aff_chunks, cap):
    n_ch = aff_chunks.shape[0]
    a = jnp.transpose(aff_chunks, (1, 0, 2))
    rows = N_EXPERTS * n_ch
    return pl.pallas_call(
        functools.partial(_select_kernel, cap=cap),
        out_shape=[jax.ShapeDtypeStruct((N_EXPERTS, cap), I32),
                   jax.ShapeDtypeStruct((N_EXPERTS, cap), F32)],
        scratch_shapes=[pltpu.VMEM((rows, LANES), F32), pltpu.VMEM((rows, LANES), F32),
                        pltpu.VMEM((N_EXPERTS, LANES), F32)],
        compiler_params=pltpu.CompilerParams(vmem_limit_bytes=VMEM_LIMIT_BYTES),
        name="select",
    )(a)


ROW_CH = D // LANES
GATHER_UNROLL = 16
FFN_ROWS = 512


def _cm_stride(m):
    return m + SUBLANES


def _moe_ffn_kernel(idx_ref, src_ref, g_ref, wg_ref, wu_ref, wd_ref, o_ref, tile_s, xe_s, *, cap):
    e = pl.program_id(0)
    f = pl.program_id(1)
    stride = _cm_stride(cap)
    tm = min(FFN_ROWS, cap)

    def gather_token(p, n):
        pair = src_ref[n >> 1].astype(F32)
        odd = (jnp.full((ROW_CH, LANES), n, I32) & 1) == 1
        tile_s[pl.ds(p, ROW_CH, stride=stride), :] = jnp.where(odd, pair[ROW_CH:], pair[:ROW_CH])

    @pl.when(jnp.logical_and(e == 0, f == 0))
    def _first_gather():
        def body(gi, carry):
            base = pl.multiple_of(gi * GATHER_UNROLL, GATHER_UNROLL)
            for k in range(GATHER_UNROLL):
                gather_token(base + k, idx_ref[base + k])
            return carry

        lax.fori_loop(0, cap // GATHER_UNROLL, body, 0)

    def ffn(xt):
        hg = _dot(xt, wg_ref[...].astype(BF16))
        hu = _dot(xt, wu_ref[...].astype(BF16))
        h = (hg * jax.nn.sigmoid(hg) * hu).astype(BF16)
        return _dot(h, wd_ref[...].astype(BF16))

    @pl.when(f == 0)
    def _half0():
        for t in range(cap // tm):
            r0 = t * tm
            xt = jnp.concatenate([tile_s[pl.ds(j * stride + r0, tm), :].astype(BF16) for j in range(ROW_CH)],
                                 axis=1)
            xe_s[r0:r0 + tm, :] = xt
            y = ffn(xt)
            for j in range(ROW_CH):
                o_ref[pl.ds(j * stride + r0, tm), :] = y[:, j * LANES:(j + 1) * LANES]

    @pl.when(f == 1)
    def _half1():
        for t in range(cap // tm):
            r0 = t * tm
            y = ffn(xe_s[r0:r0 + tm, :])
            gt = g_ref[r0:r0 + tm, :]
            for j in range(ROW_CH):
                rows = pl.ds(j * stride + r0, tm)
                o_ref[rows, :] = (o_ref[rows, :] + y[:, j * LANES:(j + 1) * LANES]) * gt
        for j in range(ROW_CH):
            o_ref[pl.ds(j * stride + cap, stride - cap), :] = jnp.zeros((stride - cap, LANES), F32)
        nxt = (e + 1) * cap
        for p in range(cap):
            gather_token(p, idx_ref[nxt + p])


def _moe_ffn(idx, gates, xnb, w_gate, w_up, w_down, layer, cap):
    n_tok = 2 * xnb.shape[0]
    ff = w_gate.shape[-1]
    assert ff == 2 * FF_CHUNK
    stride = _cm_stride(cap)
    src = xnb
    idx = jnp.concatenate([idx, jnp.zeros((cap,), I32)])
    grid_spec = pltpu.PrefetchScalarGridSpec(
        num_scalar_prefetch=1,
        grid=(N_EXPERTS, ff // FF_CHUNK),
        in_specs=[pl.BlockSpec((n_tok // 2, 2 * ROW_CH, LANES), lambda e, f, idx: (0, 0, 0),
                               pipeline_mode=pl.Buffered(1)),
                  pl.BlockSpec((None, cap, 1), lambda e, f, idx: (e, 0, 0)),
                  pl.BlockSpec((None, None, D, FF_CHUNK), lambda e, f, idx: (layer, e, 0, f)),
                  pl.BlockSpec((None, None, D, FF_CHUNK), lambda e, f, idx: (layer, e, 0, f)),
                  pl.BlockSpec((None, None, FF_CHUNK, D), lambda e, f, idx: (layer, e, f, 0))],
        out_specs=pl.BlockSpec((None, ROW_CH * stride, LANES), lambda e, f, idx: (e, 0, 0)),
        scratch_shapes=[pltpu.VMEM((ROW_CH * stride, LANES), F32), pltpu.VMEM((cap, D), BF16)],
    )
    return pl.pallas_call(
        functools.partial(_moe_ffn_kernel, cap=cap),
        grid_spec=grid_spec,
        out_shape=jax.ShapeDtypeStruct((N_EXPERTS, ROW_CH * stride, LANES), F32),
        compiler_params=_cparams(("arbitrary", "arbitrary"), VMEM_LIMIT_BYTES),
        name="moe_ffn",
    )(idx, src, gates.reshape(N_EXPERTS, cap, 1), w_gate, w_up, w_down)


COMBINE_VMEM_LIMIT_BYTES = 60 * 1024 * 1024


def _combine_kernel(idx_ref, ye_ref, x_ref, mod_ref, o_ref, acc_s, *, cap):
    s = pl.program_id(0)
    stride = _cm_stride(cap)

    @pl.when(s == 0)
    def _zero():
        acc_s[...] = jnp.zeros(acc_s.shape, F32)

    @pl.when(s < N_EXPERTS)
    def _scatter():
        def body(gi, carry):
            base = pl.multiple_of(gi * GATHER_UNROLL, GATHER_UNROLL)
            rows, sums = [], []
            for k in range(GATHER_UNROLL):
                r = pl.multiple_of(idx_ref[s * cap + base + k] * ROW_CH, ROW_CH)
                rows.append(r)
                sums.append(acc_s[pl.ds(r, ROW_CH), :] + ye_ref[pl.ds(base + k, ROW_CH, stride=stride), :])
            for r, v in zip(rows, sums):
                acc_s[pl.ds(r, ROW_CH), :] = v
            return carry

        lax.fori_loop(0, cap // GATHER_UNROLL, body, 0)

    @pl.when(s >= N_EXPERTS)
    def _residual():
        r0 = pl.multiple_of((s - N_EXPERTS) * (TOK_TILE * ROW_CH), TOK_TILE * ROW_CH)
        y = jnp.concatenate([acc_s[pl.ds(r0 + j, TOK_TILE, stride=ROW_CH), :] for j in range(ROW_CH)], axis=1)
        o_ref[...] = x_ref[...] + _mod_slice(mod_ref[...], 5) * y


def _combine(st, idx, ye, x, mods, layer, cap):
    stride = _cm_stride(cap)
    n_e = N_EXPERTS

    def tile(s):
        return jnp.maximum(s - n_e, 0)

    grid_spec = pltpu.PrefetchScalarGridSpec(
        num_scalar_prefetch=1,
        grid=(n_e + st.tiles,),
        in_specs=[pl.BlockSpec((None, ROW_CH * stride, LANES), lambda s, idx: (jnp.minimum(s, n_e - 1), 0, 0)),
                  pl.BlockSpec((TOK_TILE, D), lambda s, idx: (tile(s), 0)),
                  pl.BlockSpec((None, None, 1, N_MOD * D), lambda s, idx: (layer, st.mod_row(tile(s)), 0, 0))],
        out_specs=pl.BlockSpec((TOK_TILE, D), lambda s, idx: (tile(s), 0)),
        scratch_shapes=[pltpu.VMEM((st.n_tok * ROW_CH, LANES), F32)],
    )
    return pl.pallas_call(
        functools.partial(_combine_kernel, cap=cap),
        grid_spec=grid_spec,
        out_shape=jax.ShapeDtypeStruct((st.n_tok, D), F32),
        compiler_params=_cparams(("arbitrary",), COMBINE_VMEM_LIMIT_BYTES),
        name="moe_combine",
    )(idx, ye, x, mods)


def _moe(st, x, xnb, aff_chunks, mods, layer, w_gate, w_up, w_down):
    cap = EC_FACTOR * st.n_tok // N_EXPERTS
    idx, gates = _select(aff_chunks, cap)
    idx = idx.reshape(N_EXPERTS * cap)
    ye = _moe_ffn(idx, gates, xnb, w_gate, w_up, w_down, layer, cap)
    return _combine(st, idx, ye, x, mods, layer, cap)


def _head_sumsq(x, bd):
    x2 = x * x
    hi, lo = _split2(x2)
    w = bd.shape[0]
    cols = []
    for c in range(x.shape[1] // w):
        sl = slice(c * w, (c + 1) * w)
        cols.append(_dot(hi[:, sl], bd) + _dot(lo[:, sl], bd))
    return cols[0] if len(cols) == 1 else jnp.concatenate(cols, axis=1)


def _qk_norm(x, gain, bd):
    ms = _head_sumsq(x, bd) * (1.0 / HEAD_DIM)
    return x * lax.rsqrt(ms + EPS) * gain


def _rope(x, cos, sin_dn, sin_up):
    n = x.shape[1]
    q = HEAD_DIM // 4
    return x * cos + pltpu.roll(x, n - q, 1) * sin_dn + pltpu.roll(x, q, 1) * sin_up


def _qkv_ctx_kernel(x_ref, mod_ref, g_ref, w_ref, qg_ref, kg_ref, bd_ref, q_ref, k_ref, v_ref, kc_ref, vc_ref):
    m = mod_ref[...]
    h = _norm_mod(x_ref[...], g_ref[...], _mod_slice(m, 0), _mod_slice(m, 1))
    qkv = _dot(h.astype(BF16), w_ref[...])
    nq, nk = N_HEADS * HEAD_DIM, N_KV * HEAD_DIM
    bd = bd_ref[...]
    q = _qk_norm(qkv[:, :nq], qg_ref[...], bd)
    k = _qk_norm(qkv[:, nq:nq + nk], kg_ref[...], bd)
    v = qkv[:, nq + nk:]
    q_ref[...] = (q * (HEAD_DIM ** -0.5)).astype(BF16)
    k_ref[...] = k.astype(BF16)
    v_ref[...] = v.astype(BF16)
    kc_ref[...] = k
    vc_ref[...] = v


def _qkv_lat_kernel(x_ref, mod_ref, g_ref, w_ref, qg_ref, kg_ref, bd_ref,
                    cq_ref, sdq_ref, suq_ref, ck_ref, sdk_ref, suk_ref, q_ref, k_ref, v_ref):
    m = mod_ref[...]
    h = _norm_mod(x_ref[...], g_ref[...], _mod_slice(m, 0), _mod_slice(m, 1))
    qkv = _dot(h.astype(BF16), w_ref[...])
    nq, nk = N_HEADS * HEAD_DIM, N_KV * HEAD_DIM
    bd = bd_ref[...]
    q = _qk_norm(qkv[:, :nq], qg_ref[...], bd)
    k = _qk_norm(qkv[:, nq:nq + nk], kg_ref[...], bd)
    q = _rope(q, cq_ref[...], sdq_ref[...], suq_ref[...])
    k = _rope(k, ck_ref[...], sdk_ref[...], suk_ref[...])
    q_ref[...] = (q * (HEAD_DIM ** -0.5)).astype(BF16)
    k_ref[...] = k.astype(BF16)
    v_ref[...] = qkv[:, nq + nk:].astype(BF16)


def _rope_tables(seq, n_heads):
    n_rows = seq // GRID_W
    row = jnp.repeat(jnp.arange(n_rows), GRID_W).astype(F32)
    col = jnp.tile(jnp.arange(GRID_W), n_rows).astype(F32)
    n_freq = HEAD_DIM // 4
    inv = ROPE_BASE ** (-jnp.arange(n_freq, dtype=F32) / n_freq)
    ar, ac = row[:, None] * inv, col[:, None] * inv
    ang = jnp.concatenate([ar, ar, ac, ac], axis=-1)
    cos, sin = jnp.cos(ang), jnp.sin(ang)
    even = ((jnp.arange(HEAD_DIM) // n_freq) % 2 == 0).astype(F32)
    sin_dn = -sin * even
    sin_up = sin * (1.0 - even)
    return tuple(jnp.tile(a, (1, n_heads)) for a in (cos, sin_dn, sin_up))


def _qkv(st, x, mods, layer, g1, w_qkv, q_gain, k_gain, rope):
    nq, nk = N_HEADS * HEAD_DIM, N_KV * HEAD_DIM
    bd = jnp.asarray(np.kron(np.eye(4, dtype=np.float32), np.ones((HEAD_DIM, HEAD_DIM), np.float32))).astype(BF16)
    qg = jnp.tile(q_gain.reshape(1, HEAD_DIM), (1, N_HEADS))
    kg = jnp.tile(k_gain.reshape(1, HEAD_DIM), (1, N_KV))
    base_specs = [_tok_spec(), st.mod_spec(layer), _const_spec((1, D)), _const_spec((D, nq + 2 * nk)),
                  _const_spec((1, nq)), _const_spec((1, nk)), _const_spec((4 * HEAD_DIM, 4 * HEAD_DIM))]
    outs = [jax.ShapeDtypeStruct((st.n_tok, nq), BF16), jax.ShapeDtypeStruct((st.n_tok, nk), BF16),
            jax.ShapeDtypeStruct((st.n_tok, nk), BF16)]
    out_specs = [_tok_spec(nq), _tok_spec(nk), _tok_spec(nk)]
    if rope:
        tps = st.tiles_per_seq
        tq, tk = _rope_tables(st.seq, N_HEADS), _rope_tables(st.seq, N_KV)
        tab_specs = ([pl.BlockSpec((TOK_TILE, nq), lambda i: (i % tps, 0))] * 3
                     + [pl.BlockSpec((TOK_TILE, nk), lambda i: (i % tps, 0))] * 3)
        return pl.pallas_call(
            _qkv_lat_kernel, grid=(st.tiles,), in_specs=base_specs + tab_specs, out_specs=out_specs,
            out_shape=outs, compiler_params=_cparams(("arbitrary",), VMEM_LIMIT_BYTES), name="qkv_latent",
        )(x, mods, g1, w_qkv, qg, kg, bd, *tq, *tk)
    outs += [jax.ShapeDtypeStruct((st.n_tok, nk), F32)] * 2
    out_specs += [_tok_spec(nk)] * 2
    return pl.pallas_call(
        _qkv_ctx_kernel, grid=(st.tiles,), in_specs=base_specs, out_specs=out_specs,
        out_shape=outs, compiler_params=_cparams(("arbitrary",), VMEM_LIMIT_BYTES), name="qkv_context",
    )(x, mods, g1, w_qkv, qg, kg, bd)


def _softmax_av(s, v, sink):
    mx = jnp.maximum(jnp.max(s, axis=1, keepdims=True), sink)
    p = jnp.exp(s - mx)
    den = jnp.sum(p, axis=1, keepdims=True) + jnp.exp(sink - mx)
    return _dot(p.astype(BF16), v) / den


def _qk(q, k):
    return lax.dot_general(q, k, (((1,), (1,)), ((), ())), preferred_element_type=F32)


def _gqa_attend(q, k, v, sink_ref, ok):
    nq = q.shape[0]
    row = lax.broadcasted_iota(I32, (GQA_G * nq, 1), 0)
    outs = []
    for kv in range(N_KV):
        ks = slice(kv * HEAD_DIM, (kv + 1) * HEAD_DIM)
        h0 = kv * GQA_G
        qs = jnp.concatenate([q[:, (h0 + g) * HEAD_DIM:(h0 + g + 1) * HEAD_DIM] for g in range(GQA_G)], axis=0)
        sink = jnp.full((GQA_G * nq, 1), sink_ref[h0], F32)
        for g in range(1, GQA_G):
            sink = jnp.where(row >= g * nq, sink_ref[h0 + g], sink)
        s = _qk(qs, k[:, ks])
        if ok is not None:
            s = jnp.where(ok, s, -jnp.inf)
        o = _softmax_av(s, v[:, ks], sink)
        outs += [o[g * nq:(g + 1) * nq] for g in range(GQA_G)]
    return jnp.concatenate(outs, axis=1)


def _ctx_attn_kernel(sink_ref, q_ref, k_ref, v_ref, o_ref):
    o_ref[...] = _gqa_attend(q_ref[...], k_ref[...], v_ref[...], sink_ref, None).astype(BF16)


def _ctx_attn(st, q, k, v, sink):
    nq, nk = N_HEADS * HEAD_DIM, N_KV * HEAD_DIM
    seq = st.seq
    return pl.pallas_call(
        _ctx_attn_kernel,
        grid=(st.batch,),
        in_specs=[pl.BlockSpec(memory_space=pltpu.SMEM),
                  pl.BlockSpec((seq, nq), lambda b: (b, 0)),
                  pl.BlockSpec((seq, nk), lambda b: (b, 0)),
                  pl.BlockSpec((seq, nk), lambda b: (b, 0))],
        out_specs=pl.BlockSpec((seq, nq), lambda b: (b, 0)),
        out_shape=jax.ShapeDtypeStruct((st.n_tok, nq), BF16),
        compiler_params=_cparams(("arbitrary",), VMEM_LIMIT_BYTES),
        name="context_attention",
    )(sink, q, k, v)


def _lat_attn_kernel(sink_ref, q_ref, k_ref, v_ref, kc_ref, vc_ref, o_ref, *, n_blocks):
    j = pl.program_id(1)
    w = WINDOW
    jp = jnp.maximum(j - 1, 0)
    jn = jnp.minimum(j + 1, n_blocks - 1)

    def rows(ref, blk):
        return ref[pl.ds(pl.multiple_of(blk * w, w), w), :]

    kcat = jnp.concatenate([rows(k_ref, jp), rows(k_ref, j), rows(k_ref, jn), kc_ref[...].astype(BF16)], axis=0)
    vcat = jnp.concatenate([rows(v_ref, jp), rows(v_ref, j), rows(v_ref, jn), vc_ref[...].astype(BF16)], axis=0)
    n_keys = kcat.shape[0]
    qi = lax.broadcasted_iota(I32, (GQA_G * w, n_keys), 0) & (w - 1)
    ki = lax.broadcasted_iota(I32, (GQA_G * w, n_keys), 1)
    ok = (((ki < w) & (j > 0) & (ki >= qi))
          | ((ki >= w) & (ki < 2 * w))
          | ((ki >= 2 * w) & (ki < 3 * w) & (j < n_blocks - 1) & (ki - 2 * w <= qi))
          | (ki >= 3 * w))
    o_ref[...] = _gqa_attend(q_ref[...], kcat, vcat, sink_ref, ok).astype(BF16)


def _lat_attn(st, q, k, v, k_ctx, v_ctx, sink):
    nq, nk = N_HEADS * HEAD_DIM, N_KV * HEAD_DIM
    seq, past = st.seq, k_ctx.shape[0] // st.batch
    n_blocks = seq // WINDOW
    return pl.pallas_call(
        functools.partial(_lat_attn_kernel, n_blocks=n_blocks),
        grid=(st.batch, n_blocks),
        in_specs=[pl.BlockSpec(memory_space=pltpu.SMEM),
                  pl.BlockSpec((WINDOW, nq), lambda b, j: (b * n_blocks + j, 0)),
                  pl.BlockSpec((seq, nk), lambda b, j: (b, 0)),
                  pl.BlockSpec((seq, nk), lambda b, j: (b, 0)),
                  pl.BlockSpec((past, nk), lambda b, j: (b, 0)),
                  pl.BlockSpec((past, nk), lambda b, j: (b, 0))],
        out_specs=pl.BlockSpec((WINDOW, nq), lambda b, j: (b * n_blocks + j, 0)),
        out_shape=jax.ShapeDtypeStruct((st.n_tok, nq), BF16),
        compiler_params=_cparams(("arbitrary", "arbitrary"), VMEM_LIMIT_BYTES),
        name="latent_attention",
    )(sink, q, k, v, k_ctx, v_ctx)


def _fourier_kernel(x_ref, mod_ref, g_ref, cs_ref, ss_ref, cc_ref, sc_ref, o_ref, *, scale):
    m = mod_ref[...]
    h = _norm_mod(x_ref[...], g_ref[...], _mod_slice(m, 0), _mod_slice(m, 1)).astype(BF16)
    p = _dot(cs_ref[...], h).astype(BF16)
    q = _dot(ss_ref[...], h).astype(BF16)
    gw = FOURIER_GW
    cc, sc = cc_ref[...], sc_ref[...]
    outs = []
    for g in range(FOURIER_GROUPS):
        sl = slice(g * gw, (g + 1) * gw)
        outs.append(_dot(p[:, sl], cc) - _dot(q[:, sl], sc))
    o_ref[...] = (jnp.concatenate(outs, axis=1) * scale).astype(BF16)


def _dft_tables(n):
    k = jnp.arange(n, dtype=I32)
    ang = ((k[:, None] * k[None, :]) % n).astype(F32) * (2.0 * math.pi / n)
    return jnp.cos(ang).astype(BF16), jnp.sin(ang).astype(BF16)


def _fourier(st, x, mods, layer, g1):
    seq = st.seq
    cs, ss = _dft_tables(seq)
    cc, sc = _dft_tables(FOURIER_GW)
    mod_spec = pl.BlockSpec((None, None, 1, N_MOD * D), lambda b: (layer, 0 if st.shared else 1 + b, 0, 0))
    return pl.pallas_call(
        functools.partial(_fourier_kernel, scale=1.0 / math.sqrt(seq * FOURIER_GW)),
        grid=(st.batch,),
        in_specs=[pl.BlockSpec((seq, D), lambda b: (b, 0)), mod_spec, _const_spec((1, D)),
                  _const_spec((seq, seq)), _const_spec((seq, seq)),
                  _const_spec((FOURIER_GW, FOURIER_GW)), _const_spec((FOURIER_GW, FOURIER_GW))],
        out_specs=pl.BlockSpec((seq, D), lambda b: (b, 0)),
        out_shape=jax.ShapeDtypeStruct((st.n_tok, D), BF16),
        compiler_params=_cparams(("arbitrary",), VMEM_LIMIT_BYTES),
        name="fourier",
    )(x, mods, g1, cs, ss, cc, sc)


def kernel(x_prompt, x_sample, state_rglru, cache_k, cache_v, c, c_ctx, mod_w, mod_b, norm1_g, norm2_g,
           rg_w_in, rg_conv_w, rg_conv_b, rg_w_a, rg_b_a, rg_w_x, rg_b_x, rg_lambda, rg_w_out,
           at_w_qkv, at_q_norm, at_k_norm, at_sink, at_w_o, ft_w, moe_router, moe_w_gate, moe_w_up, moe_w_down):
    depth = mod_w.shape[0]
    batch, seq, _ = x_prompt.shape
    dec_batch, dec_seq, _ = x_sample.shape
    assert 1 + dec_batch <= MOD_ROWS
    streams = (_Stream(batch, seq, True), _Stream(dec_batch, dec_seq, False))
    cond = jnp.concatenate([c_ctx[None, :], c, jnp.zeros((MOD_ROWS - 1 - dec_batch, D), F32)], axis=0)
    mods = _modulation(cond, mod_w, mod_b).reshape(depth, MOD_ROWS, 1, N_MOD * D)

    xs = [x_prompt.reshape(batch * seq, D), x_sample.reshape(dec_batch * dec_seq, D)]
    new_rg, new_k, new_v = [], [], []
    n_mixers = 3
    for layer in range(depth):
        kind, j = layer % n_mixers, layer // n_mixers
        g1 = norm1_g[layer].reshape(1, D)
        g2 = norm2_g[layer].reshape(1, D)
        wr = _router_pieces(moe_router[layer])
        for si, st in enumerate(streams):
            x = xs[si]
            if kind == 0:
                gate, u = _rg_in(st, x, mods, layer, g1, rg_w_in[j].astype(BF16))
                if st.shared:
                    h0 = jnp.zeros((2, st.batch, D), F32)
                else:
                    h0 = jnp.transpose(state_rglru[:, j], (1, 0, 2))
                h, fin = _rg_scan(st, u, h0, rg_conv_w[j], rg_conv_b[j], rg_w_a[j], rg_b_a[j],
                                  rg_w_x[j], rg_b_x[j], rg_lambda[j])
                if st.shared:
                    new_rg.append(jnp.transpose(fin, (1, 0, 2)))
                x1, xnb, aff =_lin_out(st, (h, gate), rg_w_out[j].astype(BF16), x, mods, layer, g2, wr, True)
            elif kind == 1:
                w_qkv = at_w_qkv[j].astype(BF16)
                if st.shared:
                    q, k, v, kc, vc = _qkv(st, x, mods, layer, g1, w_qkv, at_q_norm[j], at_k_norm[j], False)
                    new_k.append(kc.reshape(st.batch, st.seq, N_KV, HEAD_DIM))
                    new_v.append(vc.reshape(st.batch, st.seq, N_KV, HEAD_DIM))
                    a = _ctx_attn(st, q, k, v, at_sink[j])
                else:
                    q, k, v = _qkv(st, x, mods, layer, g1, w_qkv, at_q_norm[j], at_k_norm[j], True)
                    nk = N_KV * HEAD_DIM
                    a = _lat_attn(st, q, k, v, cache_k[:, j].reshape(-1, nk), cache_v[:, j].reshape(-1, nk),
                                  at_sink[j])
                x1, xnb, aff =_lin_out(st, (a,), at_w_o[j].astype(BF16), x, mods, layer, g2, wr, False)
            else:
                a = _fourier(st, x, mods, layer, g1)
                x1, xnb, aff =_lin_out(st, (a,), ft_w[j].astype(BF16), x, mods, layer, g2, wr, False)
            xs[si] = _moe(st, x1, xnb, aff, mods, layer, moe_w_gate, moe_w_up, moe_w_down)
    return (xs[0].reshape(batch, seq, D), xs[1].reshape(dec_batch, dec_seq, D),
            jnp.stack(new_rg, axis=1), jnp.stack(new_k, axis=1), jnp.stack(new_v, axis=1))
```

```python
import functools
import math

import numpy as np
import jax
import jax.numpy as jnp
from jax import lax
from jax.experimental import pallas as pl
from jax.experimental.pallas import tpu as pltpu

F32 = jnp.float32
BF16 = jnp.bfloat16
I32 = jnp.int32
U32 = jnp.uint32

D = 1024
N_MOD = 6
EPS = 1e-6
GRID_W = 64
CONV_W = 4
CONV_LEFT = 2
LRU_C = 8.0
RNN_BLOCKS = 16
RNN_BLOCK = D // RNN_BLOCKS
N_HEADS = 16
N_KV = 4
HEAD_DIM = 64
GQA_G = N_HEADS // N_KV
WINDOW = 128
ROPE_BASE = 10000.0
FOURIER_GROUPS = 4
FOURIER_GW = D // FOURIER_GROUPS
N_EXPERTS = 16
EC_FACTOR = 2

LANES = 128
SUBLANES = 8
VMEM_LIMIT_BYTES = 56 * 1024 * 1024

TOK_TILE = 256
MOD_ROWS = 16
SCAN_CC = 128
SCAN_TC = 128
FF_CHUNK = 512


def _cparams(sem, vmem=None):
    return pltpu.CompilerParams(dimension_semantics=sem, vmem_limit_bytes=vmem)


def _split2(a):
    hi = a.astype(BF16)
    lo = (a - hi.astype(F32)).astype(BF16)
    return hi, lo


def _dot(a, b):
    return jnp.dot(a, b, preferred_element_type=F32)


def _dot3(a, b):
    a_hi, a_lo = _split2(a)
    b_hi, b_lo = _split2(b)
    return _dot(a_hi, b_hi) + _dot(a_hi, b_lo) + _dot(a_lo, b_hi)


def _norm_mod(x, g, shift, scale):
    ms = jnp.mean(x * x, axis=-1, keepdims=True)
    y = x * lax.rsqrt(ms + EPS) * g
    return y * (1.0 + scale) + shift


def _mod_slice(m, k):
    return m[:, k * D:(k + 1) * D]


class _Stream:
    def __init__(self, batch, seq, shared_cond):
        self.batch, self.seq, self.shared = batch, seq, shared_cond
        self.n_tok = batch * seq
        self.tiles = self.n_tok // TOK_TILE
        self.tiles_per_seq = seq // TOK_TILE

    def mod_row(self, i):
        return 0 if self.shared else 1 + i // self.tiles_per_seq

    def mod_spec(self, layer):
        return pl.BlockSpec((None, None, 1, N_MOD * D), lambda i: (layer, self.mod_row(i), 0, 0))

    def seq_major_spec(self):
        tps = self.tiles_per_seq
        return pl.BlockSpec((TOK_TILE, D), lambda i: (i % tps, i // tps))


def _tok_spec(width=D):
    return pl.BlockSpec((TOK_TILE, width), lambda i: (i, 0))


def _const_spec(shape):
    nd = len(shape)
    return pl.BlockSpec(shape, lambda i: (0,) * nd)


def _mod_kernel(c_ref, w_ref, b_ref, o_ref):
    c = c_ref[...]
    c = c * jax.nn.sigmoid(c)
    o_ref[...] = _dot3(c, w_ref[...]) + b_ref[...]


def _modulation(cond, mod_w, mod_b):
    depth = mod_w.shape[0]
    tn = N_MOD * D // 4
    return pl.pallas_call(
        _mod_kernel,
        grid=(depth, N_MOD * D // tn),
        in_specs=[pl.BlockSpec((MOD_ROWS, D), lambda l, n: (0, 0)),
                  pl.BlockSpec((None, D, tn), lambda l, n: (l, 0, n)),
                  pl.BlockSpec((None, 1, tn), lambda l, n: (l, 0, n))],
        out_specs=pl.BlockSpec((None, MOD_ROWS, tn), lambda l, n: (l, 0, n)),
        out_shape=jax.ShapeDtypeStruct((depth, MOD_ROWS, N_MOD * D), F32),
        compiler_params=_cparams(("arbitrary", "arbitrary"), VMEM_LIMIT_BYTES),
        name="modulation",
    )(cond, mod_w, mod_b.reshape(depth, 1, N_MOD * D))


def _rg_in_kernel(x_ref, mod_ref, g_ref, w_ref, gate_ref, u_ref):
    m = mod_ref[...]
    h = _norm_mod(x_ref[...], g_ref[...], _mod_slice(m, 0), _mod_slice(m, 1))
    gu = _dot(h.astype(BF16), w_ref[...])
    gate_ref[...] = gu[:, :D].astype(BF16)
    u_ref[...] = gu[:, D:]


def _rg_in(st, x, mods, layer, g1, w_in):
    shape = (st.seq, st.batch * D)
    return pl.pallas_call(
        _rg_in_kernel,
        grid=(st.tiles,),
        in_specs=[_tok_spec(), st.mod_spec(layer), _const_spec((1, D)), _const_spec((D, 2 * D))],
        out_specs=[st.seq_major_spec(), st.seq_major_spec()],
        out_shape=[jax.ShapeDtypeStruct(shape, BF16), jax.ShapeDtypeStruct(shape, F32)],
        compiler_params=_cparams(("arbitrary",), VMEM_LIMIT_BYTES),
        name="rg_in",
    )(x, mods, g1, w_in)


def _softplus(z):
    return jnp.maximum(z, 0.0) + jnp.log1p(jnp.exp(-jnp.abs(z)))


def _rg_scan_kernel(u_ref, h0_ref, cw_ref, cb_ref, wg_ref, bg_ref, lam_ref, h_ref, fin_ref,
                    upad, a_f, b_f, a_b, b_b, *, seq):
    cc = u_ref.shape[-1]
    pad_hi = CONV_W - 1 - CONV_LEFT
    upad[0:CONV_LEFT] = jnp.zeros((CONV_LEFT, SUBLANES, cc), F32)
    upad[CONV_LEFT:CONV_LEFT + seq] = u_ref[...]
    upad[CONV_LEFT + seq:CONV_LEFT + seq + pad_hi] = jnp.zeros((pad_hi, SUBLANES, cc), F32)
    k2 = (-0.5 * LRU_C * math.log2(math.e)) * _softplus(-lam_ref[...])
    cw = cw_ref[...]
    rows = SCAN_TC * SUBLANES

    def coef(c, carry):
        t0 = pl.multiple_of(c * SCAN_TC, SCAN_TC)
        uc = cb_ref[...] + cw[0:1] * upad[pl.ds(t0, SCAN_TC)]
        for k in range(1, CONV_W):
            uc = uc + cw[k:k + 1] * upad[pl.ds(t0 + k, SCAN_TC)]
        u2 = uc.reshape(rows, cc)
        gth = jnp.tanh(_dot(u2.astype(BF16), wg_ref[...]) + bg_ref[...])
        hu2 = 0.5 * u2
        for d, (a_s, b_s) in enumerate(((a_f, b_f), (a_b, b_b))):
            r_th = gth[:, (2 * d) * cc:(2 * d + 1) * cc]
            i_th = gth[:, (2 * d + 1) * cc:(2 * d + 2) * cc]
            a = jnp.exp2(k2[d] * r_th + k2[d])
            q = 1.0 - a * a
            root = jnp.where(q > 0.0, q * lax.rsqrt(q), 0.0)
            a_s[pl.ds(t0, SCAN_TC)] = a.reshape(SCAN_TC, SUBLANES, cc)
            b_s[pl.ds(t0, SCAN_TC)] = (root * (i_th * hu2 + hu2)).reshape(SCAN_TC, SUBLANES, cc)
        return carry

    lax.fori_loop(0, seq // SCAN_TC, coef, 0)

    def step(i, carry):
        hf, hb = carry
        t0 = 2 * i
        a0, b0, a1, b1 = a_f[t0], b_f[t0], a_f[t0 + 1], b_f[t0 + 1]
        b_f[t0] = a0 * hf + b0
        hf = (a1 * a0) * hf + (a1 * b0 + b1)
        b_f[t0 + 1] = hf
        s0 = seq - 1 - t0
        c0, d0, c1, d1 = a_b[s0], b_b[s0], a_b[s0 - 1], b_b[s0 - 1]
        b_b[s0] = c0 * hb + d0
        hb = (c1 * c0) * hb + (c1 * d0 + d1)
        b_b[s0 - 1] = hb
        return hf, hb

    hf, hb = lax.fori_loop(0, seq // 2, step, (h0_ref[0], h0_ref[1]), unroll=4)
    fin_ref[0] = hf
    fin_ref[1] = hb
    h_ref[...] = b_f[...] + b_b[...]


def _blockdiag_pairs(w):
    per = SCAN_CC // RNN_BLOCK
    w4 = w.reshape(D // SCAN_CC, per, RNN_BLOCK, RNN_BLOCK)
    eye = jnp.eye(per, dtype=w.dtype)
    return jnp.einsum('cipq,ij->cipjq', w4, eye).reshape(D // SCAN_CC, SCAN_CC, SCAN_CC)


def _rg_scan(st, u, h0, conv_w, conv_b, w_a, b_a, w_x, b_x, lam):
    seq, batch = st.seq, st.batch
    n_cc = D // SCAN_CC
    wg = (0.5 * jnp.concatenate([_blockdiag_pairs(w_a[0]), _blockdiag_pairs(w_x[0]),
                                 _blockdiag_pairs(w_a[1]), _blockdiag_pairs(w_x[1])], axis=-1)).astype(BF16)
    bg = 0.5 * jnp.concatenate([b.reshape(n_cc, 1, SCAN_CC) for b in (b_a[0], b_x[0], b_a[1], b_x[1])], axis=-1)
    blk = (seq, SUBLANES, SCAN_CC)
    scr = pltpu.VMEM(blk, F32)
    h, fin = pl.pallas_call(
        functools.partial(_rg_scan_kernel, seq=seq),
        grid=(batch // SUBLANES, n_cc),
        in_specs=[pl.BlockSpec(blk, lambda b, c: (0, b, c)),
                  pl.BlockSpec((2, SUBLANES, SCAN_CC), lambda b, c: (0, b, c)),
                  pl.BlockSpec((CONV_W, 1, SCAN_CC), lambda b, c: (0, 0, c)),
                  pl.BlockSpec((1, 1, SCAN_CC), lambda b, c: (0, 0, c)),
                  pl.BlockSpec((None, SCAN_CC, 4 * SCAN_CC), lambda b, c: (c, 0, 0)),
                  pl.BlockSpec((None, 1, 4 * SCAN_CC), lambda b, c: (c, 0, 0)),
                  pl.BlockSpec((2, 1, SCAN_CC), lambda b, c: (0, 0, c))],
        out_specs=[pl.BlockSpec(blk, lambda b, c: (0, b, c)),
                   pl.BlockSpec((2, SUBLANES, SCAN_CC), lambda b, c: (0, b, c))],
        out_shape=[jax.ShapeDtypeStruct((seq, batch, D), F32),
                   jax.ShapeDtypeStruct((2, batch, D), F32)],
        scratch_shapes=[pltpu.VMEM((seq + CONV_W - 1, SUBLANES, SCAN_CC), F32), scr, scr, scr, scr],
        compiler_params=_cparams(("arbitrary", "arbitrary"), VMEM_LIMIT_BYTES),
        name="rg_scan",
    )(u.reshape(seq, batch, D), h0, conv_w.reshape(CONV_W, 1, D), conv_b.reshape(1, 1, D),
      wg, bg, lam.reshape(2, 1, D))
    return h.reshape(seq, batch * D), fin


LIN_CHUNK = LANES
LIN_CHUNKS = TOK_TILE // LIN_CHUNK


def _chunk_rows(k):
    return slice(k * LIN_CHUNK, (k + 1) * LIN_CHUNK)


def _route_and_pack(ys, m, x_ref, g2_ref, wr_ref, xo_ref, xnb_ref, aff_ref, slab_s):
    n_ch = D // LANES
    ks = range(LIN_CHUNKS)
    xs = [x_ref[_chunk_rows(k), :] + _mod_slice(m, 2) * ys[k] for k in ks]
    for k in ks:
        xo_ref[_chunk_rows(k), :] = xs[k]
    xns = [_norm_mod(x, g2_ref[...], _mod_slice(m, 3), _mod_slice(m, 4)) for x in xs]
    his = [xn.astype(BF16) for xn in xns]
    los = [(xn - hi.astype(F32)).astype(BF16) for xn, hi in zip(xns, his)]
    wr = wr_ref[...]
    l1s = [_qk(wr, hi) for hi in his]
    l2s = [_qk(wr, lo) for lo in los]
    for k in ks:
        for j in range(n_ch):
            slab_s[pl.ds(k * LIN_CHUNK * n_ch + j, LIN_CHUNK, stride=n_ch), :] = xns[k][:, j * LANES:(j + 1) * LANES]
    xnb_ref[...] = slab_s[...].astype(BF16).reshape(xnb_ref.shape)
    e = N_EXPERTS
    for k in ks:
        l1, l2 = l1s[k], l2s[k]
        logit = l1[0:e] + l1[e:2 * e] + l1[2 * e:3 * e] + l2[0:e] + l2[e:2 * e]
        mx = jnp.max(logit, axis=0, keepdims=True)
        ex = jnp.exp(logit - mx)
        aff_ref[k] = ex / jnp.sum(ex, axis=0, keepdims=True)


def _lin_out_kernel(a_ref, w_ref, x_ref, mod_ref, g2_ref, wr_ref, xo_ref, xnb_ref, aff_ref, slab_s):
    ys = [_dot(a_ref[_chunk_rows(k), :], w_ref[...]) for k in range(LIN_CHUNKS)]
    _route_and_pack(ys, mod_ref[...], x_ref, g2_ref, wr_ref, xo_ref, xnb_ref, aff_ref, slab_s)


def _lin_out_gated_kernel(h_ref, gate_ref, w_ref, x_ref, mod_ref, g2_ref, wr_ref, xo_ref, xnb_ref, aff_ref,
                          slab_s):
    acts = [(h_ref[_chunk_rows(k), :] * jax.nn.gelu(gate_ref[_chunk_rows(k), :].astype(F32))).astype(BF16)
            for k in range(LIN_CHUNKS)]
    ys = [_dot(a, w_ref[...]) for a in acts]
    _route_and_pack(ys, mod_ref[...], x_ref, g2_ref, wr_ref, xo_ref, xnb_ref, aff_ref, slab_s)


def _router_pieces(w_router):
    hi = w_router.astype(BF16)
    r1 = w_router - hi.astype(F32)
    mid = r1.astype(BF16)
    lo = (r1 - mid.astype(F32)).astype(BF16)
    return jnp.concatenate([hi, mid, lo], axis=1).T


def _lin_out(st, srcs, w, x, mods, layer, g2, wr, gated):
    n_chunks = st.n_tok // LANES
    if gated:
        body, src_specs = _lin_out_gated_kernel, [st.seq_major_spec(), st.seq_major_spec()]
    else:
        body, src_specs = _lin_out_kernel, [_tok_spec()]
    return pl.pallas_call(
        body,
        grid=(st.tiles,),
        in_specs=src_specs + [_const_spec((D, D)), _tok_spec(), st.mod_spec(layer), _const_spec((1, D)),
                              _const_spec((3 * N_EXPERTS, D))],
        out_specs=[_tok_spec(), pl.BlockSpec((TOK_TILE // 2, 2 * (D // LANES), LANES), lambda i: (i, 0, 0)),
                   pl.BlockSpec((TOK_TILE // LANES, N_EXPERTS, LANES), lambda i: (i, 0, 0))],
        out_shape=[jax.ShapeDtypeStruct((st.n_tok, D), F32),
                   jax.ShapeDtypeStruct((st.n_tok // 2, 2 * (D // LANES), LANES), BF16),
                   jax.ShapeDtypeStruct((n_chunks, N_EXPERTS, LANES), F32)],
        scratch_shapes=[pltpu.VMEM((TOK_TILE * (D // LANES), LANES), F32)],
        compiler_params=_cparams(("arbitrary",), VMEM_LIMIT_BYTES),
        name="lin_out_gated" if gated else "lin_out",
    )(*srcs, w, x, mods, g2, wr)


def _select_kernel(a_ref, idx_ref, g_ref, linc_s, cnt_s, crow_s, *, cap):
    n_e, n_ch, _ = a_ref.shape
    assert n_ch & (n_ch - 1) == 0 and n_ch <= LANES
    rows = n_e * n_ch
    a3 = a_ref[...]
    capf = jnp.float32(cap)

    def count(mask3):
        c = jnp.sum(mask3.astype(F32), axis=2, keepdims=True)
        return jnp.sum(c, axis=1, keepdims=True)

    def as_f32(bits):
        return pltpu.bitcast(bits, F32)

    def search(i, thr):
        cand = thr | (jnp.int32(1) << (30 - i))
        return jnp.where(count(a3 >= as_f32(cand)) >= capf, cand, thr)

    thr = lax.fori_loop(0, 31, search, jnp.zeros((n_e, 1, LANES), I32))
    gt3 = a3 >= as_f32(thr + 1)
    eq3 = jnp.logical_and(a3 >= as_f32(thr), jnp.logical_not(gt3))
    need = capf - count(gt3)

    li = lax.broadcasted_iota(I32, (LANES, LANES), 0)
    lj = lax.broadcasted_iota(I32, (LANES, LANES), 1)
    upper = (li <= lj).astype(BF16)
    ones = jnp.ones((LANES, LANES), BF16)
    ri = lax.broadcasted_iota(I32, (rows, rows), 0)
    rj = lax.broadcasted_iota(I32, (rows, rows), 1)
    sh = n_ch.bit_length() - 1
    before = (((ri >> sh) == (rj >> sh)) & (rj < ri)).astype(BF16)

    def prefixes(mask3):
        x = mask3.reshape(rows, LANES).astype(BF16)
        local = _dot(x, upper)
        tot = _dot(x, ones)
        return local, tot, _dot(before, tot.astype(BF16))

    gt_l, gt_t, gt_x = prefixes(gt3)
    eq_l, eq_t, eq_x = prefixes(eq3)
    need_r = jnp.broadcast_to(need, (n_e, n_ch, LANES)).reshape(rows, LANES)
    sel_incl = gt_x + gt_l + jnp.minimum(eq_x + eq_l, need_r)
    sel_x = gt_x + jnp.minimum(eq_x, need_r)
    sel_c = gt_x + gt_t + jnp.minimum(eq_x + eq_t, need_r)
    linc_s[...] = sel_incl - sel_x
    cnt_s[...] = sel_c - sel_x
    c3 = sel_c.reshape(n_e, n_ch, LANES)
    pick = lax.broadcasted_iota(I32, (1, n_ch, LANES), 1) == lax.broadcasted_iota(I32, (1, n_ch, LANES), 2)
    crow = jnp.sum(jnp.where(pick, c3, 0.0), axis=1)
    crow = jnp.where(lax.broadcasted_iota(I32, crow.shape, 1) < n_ch, crow, jnp.float32(2 * cap + n_ch * LANES))
    crow_s[...] = crow

    slot = lax.broadcasted_iota(I32, (cap, LANES), 0).astype(F32)
    lane = lax.broadcasted_iota(I32, (cap, LANES), 1).astype(F32)
    diag = lax.broadcasted_iota(I32, (LANES, LANES), 0) == lax.broadcasted_iota(I32, (LANES, LANES), 1)
    zpad = jnp.zeros((LANES - n_ch, LANES), BF16)

    def per_expert(e, carry):
        r0 = pl.multiple_of(e * n_ch, n_ch)
        passed = (crow_s[pl.ds(e, 1), :] <= slot).astype(BF16)
        cnt_e = cnt_s[pl.ds(r0, n_ch), :].astype(BF16)
        linc_e = linc_s[pl.ds(r0, n_ch), :].astype(BF16)
        a_e = a_ref[e]
        a_hi = a_e.astype(BF16)
        r1 = a_e - a_hi.astype(F32)
        a_mid = r1.astype(BF16)
        a_lo = (r1 - a_mid.astype(F32)).astype(BF16)
        zpad_w = jnp.zeros((LANES - n_ch, 4 * LANES), BF16)
        by_passed = jnp.concatenate([jnp.concatenate([ones[:n_ch], cnt_e], axis=1), zpad_w[:, :2 * LANES]], axis=0)
        by_chunk = jnp.concatenate([jnp.concatenate([linc_e, a_hi, a_mid, a_lo], axis=1), zpad_w], axis=0)
        res = _dot(passed, by_passed)
        chunk = res[:, :LANES]
        rank = slot - res[:, LANES:]
        onehot = (chunk == lane).astype(BF16)
        res = _dot(onehot, by_chunk)
        g = res[:, :LANES]
        within = _dot((g <= rank).astype(BF16), ones)
        tok = chunk * jnp.float32(LANES) + within
        arow = res[:, LANES:2 * LANES] + res[:, 2 * LANES:3 * LANES] + res[:, 3 * LANES:]
        gate = jnp.sum(jnp.where(lane == within, arow, 0.0), axis=1, keepdims=True)
        gate = jnp.broadcast_to(gate, (cap, LANES))

        def to_row(col):
            pieces = []
            for b in range(cap // LANES):
                blk = col[b * LANES:(b + 1) * LANES, :]
                pieces.append(jnp.sum(jnp.where(diag, blk, 0.0), axis=0, keepdims=True))
            return jnp.concatenate(pieces, axis=1)

        idx_ref[pl.ds(e, 1), :] = to_row(tok).astype(I32)
        g_ref[pl.ds(e, 1), :] = to_row(gate)
        return carry

    lax.fori_loop(0, n_e, per_expert, 0)


def _select(aff_chunks, cap):
    n_ch = aff_chunks.shape[0]
    a = jnp.transpose(aff_chunks, (1, 0, 2))
    rows = N_EXPERTS * n_ch
    return pl.pallas_call(
        functools.partial(_select_kernel, cap=cap),
        out_shape=[jax.ShapeDtypeStruct((N_EXPERTS, cap), I32),
                   jax.ShapeDtypeStruct((N_EXPERTS, cap), F32)],
        scratch_shapes=[pltpu.VMEM((rows, LANES), F32), pltpu.VMEM((rows, LANES), F32),
                        pltpu.VMEM((N_EXPERTS, LANES), F32)],
        compiler_params=pltpu.CompilerParams(vmem_limit_bytes=VMEM_LIMIT_BYTES),
        name="select",
    )(a)


ROW_CH = D // LANES
GATHER_UNROLL = 16
FFN_ROWS = 512


def _cm_stride(m):
    return m + SUBLANES


def _moe_ffn_kernel(idx_ref, src_ref, g_ref, wg_ref, wu_ref, wd_ref, o_ref, tile_s, xe_s, *, cap):
    e = pl.program_id(0)
    f = pl.program_id(1)
    stride = _cm_stride(cap)
    tm = min(FFN_ROWS, cap)

    def gather_token(p, n):
        pair = src_ref[n >> 1].astype(F32)
        odd = (jnp.full((ROW_CH, LANES), n, I32) & 1) == 1
        tile_s[pl.ds(p, ROW_CH, stride=stride), :] = jnp.where(odd, pair[ROW_CH:], pair[:ROW_CH])

    @pl.when(jnp.logical_and(e == 0, f == 0))
    def _first_gather():
        def body(gi, carry):
            base = pl.multiple_of(gi * GATHER_UNROLL, GATHER_UNROLL)
            for k in range(GATHER_UNROLL):
                gather_token(base + k, idx_ref[base + k])
            return carry

        lax.fori_loop(0, cap // GATHER_UNROLL, body, 0)

    def ffn(xt):
        hg = _dot(xt, wg_ref[...].astype(BF16))
        hu = _dot(xt, wu_ref[...].astype(BF16))
        h = (hg * jax.nn.sigmoid(hg) * hu).astype(BF16)
        return _dot(h, wd_ref[...].astype(BF16))

    @pl.when(f == 0)
    def _half0():
        for t in range(cap // tm):
            r0 = t * tm
            xt = jnp.concatenate([tile_s[pl.ds(j * stride + r0, tm), :].astype(BF16) for j in range(ROW_CH)],
                                 axis=1)
            xe_s[r0:r0 + tm, :] = xt
            y = ffn(xt)
            for j in range(ROW_CH):
                o_ref[pl.ds(j * stride + r0, tm), :] = y[:, j * LANES:(j + 1) * LANES]

    @pl.when(f == 1)
    def _half1():
        nxt = (e + 1) * cap
        for t in range(cap // tm):
            r0 = t * tm
            def gather_share(q):
                for p in range(r0 + q * tm // 4, r0 + (q + 1) * tm // 4):
                    gather_token(p, idx_ref[nxt + p])

            xt = xe_s[r0:r0 + tm, :]
            gather_share(0)
            hg = _dot(xt, wg_ref[...].astype(BF16))
            gather_share(1)
            hu = _dot(xt, wu_ref[...].astype(BF16))
            h = (hg * jax.nn.sigmoid(hg) * hu).astype(BF16)
            gather_share(2)
            y = _dot(h, wd_ref[...].astype(BF16))
            gather_share(3)
            gt = g_ref[r0:r0 + tm, :]
            for j in range(ROW_CH):
                rows = pl.ds(j * stride + r0, tm)
                o_ref[rows, :] = (o_ref[rows, :] + y[:, j * LANES:(j + 1) * LANES]) * gt
        for j in range(ROW_CH):
            o_ref[pl.ds(j * stride + cap, stride - cap), :] = jnp.zeros((stride - cap, LANES), F32)


def _moe_ffn(idx, gates, xnb, w_gate, w_up, w_down, layer, cap):
    n_tok = 2 * xnb.shape[0]
    ff = w_gate.shape[-1]
    assert ff == 2 * FF_CHUNK
    stride = _cm_stride(cap)
    src = xnb
    idx = jnp.concatenate([idx, jnp.zeros((cap,), I32)])
    grid_spec = pltpu.PrefetchScalarGridSpec(
        num_scalar_prefetch=1,
        grid=(N_EXPERTS, ff // FF_CHUNK),
        in_specs=[pl.BlockSpec((n_tok // 2, 2 * ROW_CH, LANES), lambda e, f, idx: (0, 0, 0),
                               pipeline_mode=pl.Buffered(1)),
                  pl.BlockSpec((None, cap, 1), lambda e, f, idx: (e, 0, 0)),
                  pl.BlockSpec((None, None, D, FF_CHUNK), lambda e, f, idx: (layer, e, 0, f)),
                  pl.BlockSpec((None, None, D, FF_CHUNK), lambda e, f, idx: (layer, e, 0, f)),
                  pl.BlockSpec((None, None, FF_CHUNK, D), lambda e, f, idx: (layer, e, f, 0))],
        out_specs=pl.BlockSpec((None, ROW_CH * stride, LANES), lambda e, f, idx: (e, 0, 0)),
        scratch_shapes=[pltpu.VMEM((ROW_CH * stride, LANES), F32), pltpu.VMEM((cap, D), BF16)],
    )
    return pl.pallas_call(
        functools.partial(_moe_ffn_kernel, cap=cap),
        grid_spec=grid_spec,
        out_shape=jax.ShapeDtypeStruct((N_EXPERTS, ROW_CH * stride, LANES), F32),
        compiler_params=_cparams(("arbitrary", "arbitrary"), VMEM_LIMIT_BYTES),
        name="moe_ffn",
    )(idx, src, gates.reshape(N_EXPERTS, cap, 1), w_gate, w_up, w_down)


COMBINE_VMEM_LIMIT_BYTES = 60 * 1024 * 1024


def _combine_kernel(idx_ref, ye_ref, x_ref, mod_ref, o_ref, acc_s, *, cap):
    s = pl.program_id(0)
    stride = _cm_stride(cap)

    @pl.when(s == 0)
    def _zero():
        acc_s[...] = jnp.zeros(acc_s.shape, F32)

    @pl.when(s < N_EXPERTS)
    def _scatter():
        def body(gi, carry):
            base = pl.multiple_of(gi * GATHER_UNROLL, GATHER_UNROLL)
            rows, sums = [], []
            for k in range(GATHER_UNROLL):
                r = pl.multiple_of(idx_ref[s * cap + base + k] * ROW_CH, ROW_CH)
                rows.append(r)
                sums.append(acc_s[pl.ds(r, ROW_CH), :] + ye_ref[pl.ds(base + k, ROW_CH, stride=stride), :])
            for r, v in zip(rows, sums):
                acc_s[pl.ds(r, ROW_CH), :] = v
            return carry

        lax.fori_loop(0, cap // GATHER_UNROLL, body, 0)

    @pl.when(s >= N_EXPERTS)
    def _residual():
        r0 = pl.multiple_of((s - N_EXPERTS) * (TOK_TILE * ROW_CH), TOK_TILE * ROW_CH)
        y = jnp.concatenate([acc_s[pl.ds(r0 + j, TOK_TILE, stride=ROW_CH), :] for j in range(ROW_CH)], axis=1)
        o_ref[...] = x_ref[...] + _mod_slice(mod_ref[...], 5) * y


def _combine(st, idx, ye, x, mods, layer, cap):
    stride = _cm_stride(cap)
    n_e = N_EXPERTS

    def tile(s):
        return jnp.maximum(s - n_e, 0)

    grid_spec = pltpu.PrefetchScalarGridSpec(
        num_scalar_prefetch=1,
        grid=(n_e + st.tiles,),
        in_specs=[pl.BlockSpec((None, ROW_CH * stride, LANES), lambda s, idx: (jnp.minimum(s, n_e - 1), 0, 0)),
                  pl.BlockSpec((TOK_TILE, D), lambda s, idx: (tile(s), 0)),
                  pl.BlockSpec((None, None, 1, N_MOD * D), lambda s, idx: (layer, st.mod_row(tile(s)), 0, 0))],
        out_specs=pl.BlockSpec((TOK_TILE, D), lambda s, idx: (tile(s), 0)),
        scratch_shapes=[pltpu.VMEM((st.n_tok * ROW_CH, LANES), F32)],
    )
    return pl.pallas_call(
        functools.partial(_combine_kernel, cap=cap),
        grid_spec=grid_spec,
        out_shape=jax.ShapeDtypeStruct((st.n_tok, D), F32),
        compiler_params=_cparams(("arbitrary",), COMBINE_VMEM_LIMIT_BYTES),
        name="moe_combine",
    )(idx, ye, x, mods)


def _moe(st, x, xnb, aff_chunks, mods, layer, w_gate, w_up, w_down):
    cap = EC_FACTOR * st.n_tok // N_EXPERTS
    idx, gates = _select(aff_chunks, cap)
    idx = idx.reshape(N_EXPERTS * cap)
    ye = _moe_ffn(idx, gates, xnb, w_gate, w_up, w_down, layer, cap)
    return _combine(st, idx, ye, x, mods, layer, cap)


def _head_sumsq(x, bd):
    x2 = x * x
    hi, lo = _split2(x2)
    w = bd.shape[0]
    cols = []
    for c in range(x.shape[1] // w):
        sl = slice(c * w, (c + 1) * w)
        cols.append(_dot(hi[:, sl], bd) + _dot(lo[:, sl], bd))
    return cols[0] if len(cols) == 1 else jnp.concatenate(cols, axis=1)


def _qk_norm(x, gain, bd):
    ms = _head_sumsq(x, bd) * (1.0 / HEAD_DIM)
    return x * lax.rsqrt(ms + EPS) * gain


def _rope(x, cos, sin_dn, sin_up):
    n = x.shape[1]
    q = HEAD_DIM // 4
    return x * cos + pltpu.roll(x, n - q, 1) * sin_dn + pltpu.roll(x, q, 1) * sin_up


def _qkv_ctx_kernel(x_ref, mod_ref, g_ref, w_ref, qg_ref, kg_ref, bd_ref, q_ref, k_ref, v_ref, kc_ref, vc_ref):
    m = mod_ref[...]
    h = _norm_mod(x_ref[...], g_ref[...], _mod_slice(m, 0), _mod_slice(m, 1))
    qkv = _dot(h.astype(BF16), w_ref[...])
    nq, nk = N_HEADS * HEAD_DIM, N_KV * HEAD_DIM
    bd = bd_ref[...]
    q = _qk_norm(qkv[:, :nq], qg_ref[...], bd)
    k = _qk_norm(qkv[:, nq:nq + nk], kg_ref[...], bd)
    v = qkv[:, nq + nk:]
    q_ref[...] = (q * (HEAD_DIM ** -0.5)).astype(BF16)
    k_ref[...] = k.astype(BF16)
    v_ref[...] = v.astype(BF16)
    kc_ref[...] = k
    vc_ref[...] = v


def _qkv_lat_kernel(x_ref, mod_ref, g_ref, w_ref, qg_ref, kg_ref, bd_ref,
                    cq_ref, sdq_ref, suq_ref, ck_ref, sdk_ref, suk_ref, q_ref, k_ref, v_ref):
    m = mod_ref[...]
    h = _norm_mod(x_ref[...], g_ref[...], _mod_slice(m, 0), _mod_slice(m, 1))
    qkv = _dot(h.astype(BF16), w_ref[...])
    nq, nk = N_HEADS * HEAD_DIM, N_KV * HEAD_DIM
    bd = bd_ref[...]
    q = _qk_norm(qkv[:, :nq], qg_ref[...], bd)
    k = _qk_norm(qkv[:, nq:nq + nk], kg_ref[...], bd)
    q = _rope(q, cq_ref[...], sdq_ref[...], suq_ref[...])
    k = _rope(k, ck_ref[...], sdk_ref[...], suk_ref[...])
    q_ref[...] = (q * (HEAD_DIM ** -0.5)).astype(BF16)
    k_ref[...] = k.astype(BF16)
    v_ref[...] = qkv[:, nq + nk:].astype(BF16)


def _rope_tables(seq, n_heads):
    n_rows = seq // GRID_W
    row = jnp.repeat(jnp.arange(n_rows), GRID_W).astype(F32)
    col = jnp.tile(jnp.arange(GRID_W), n_rows).astype(F32)
    n_freq = HEAD_DIM // 4
    inv = ROPE_BASE ** (-jnp.arange(n_freq, dtype=F32) / n_freq)
    ar, ac = row[:, None] * inv, col[:, None] * inv
    ang = jnp.concatenate([ar, ar, ac, ac], axis=-1)
    cos, sin = jnp.cos(ang), jnp.sin(ang)
    even = ((jnp.arange(HEAD_DIM) // n_freq) % 2 == 0).astype(F32)
    sin_dn = -sin * even
    sin_up = sin * (1.0 - even)
    return tuple(jnp.tile(a, (1, n_heads)) for a in (cos, sin_dn, sin_up))


def _qkv(st, x, mods, layer, g1, w_qkv, q_gain, k_gain, rope):
    nq, nk = N_HEADS * HEAD_DIM, N_KV * HEAD_DIM
    bd = jnp.asarray(np.kron(np.eye(4, dtype=np.float32), np.ones((HEAD_DIM, HEAD_DIM), np.float32))).astype(BF16)
    qg = jnp.tile(q_gain.reshape(1, HEAD_DIM), (1, N_HEADS))
    kg = jnp.tile(k_gain.reshape(1, HEAD_DIM), (1, N_KV))
    base_specs = [_tok_spec(), st.mod_spec(layer), _const_spec((1, D)), _const_spec((D, nq + 2 * nk)),
                  _const_spec((1, nq)), _const_spec((1, nk)), _const_spec((4 * HEAD_DIM, 4 * HEAD_DIM))]
    outs = [jax.ShapeDtypeStruct((st.n_tok, nq), BF16), jax.ShapeDtypeStruct((st.n_tok, nk), BF16),
            jax.ShapeDtypeStruct((st.n_tok, nk), BF16)]
    out_specs = [_tok_spec(nq), _tok_spec(nk), _tok_spec(nk)]
    if rope:
        tps = st.tiles_per_seq
        tq, tk = _rope_tables(st.seq, N_HEADS), _rope_tables(st.seq, N_KV)
        tab_specs = ([pl.BlockSpec((TOK_TILE, nq), lambda i: (i % tps, 0))] * 3
                     + [pl.BlockSpec((TOK_TILE, nk), lambda i: (i % tps, 0))] * 3)
        return pl.pallas_call(
            _qkv_lat_kernel, grid=(st.tiles,), in_specs=base_specs + tab_specs, out_specs=out_specs,
            out_shape=outs, compiler_params=_cparams(("arbitrary",), VMEM_LIMIT_BYTES), name="qkv_latent",
        )(x, mods, g1, w_qkv, qg, kg, bd, *tq, *tk)
    outs += [jax.ShapeDtypeStruct((st.n_tok, nk), F32)] * 2
    out_specs += [_tok_spec(nk)] * 2
    return pl.pallas_call(
        _qkv_ctx_kernel, grid=(st.tiles,), in_specs=base_specs, out_specs=out_specs,
        out_shape=outs, compiler_params=_cparams(("arbitrary",), VMEM_LIMIT_BYTES), name="qkv_context",
    )(x, mods, g1, w_qkv, qg, kg, bd)


def _softmax_av(s, v, sink):
    mx = jnp.maximum(jnp.max(s, axis=1, keepdims=True), sink)
    p = jnp.exp(s - mx)
    den = jnp.sum(p, axis=1, keepdims=True) + jnp.exp(sink - mx)
    return _dot(p.astype(BF16), v) / den


def _qk(q, k):
    return lax.dot_general(q, k, (((1,), (1,)), ((), ())), preferred_element_type=F32)


def _gqa_attend(q, k, v, sink_ref, ok):
    nq = q.shape[0]
    row = lax.broadcasted_iota(I32, (GQA_G * nq, 1), 0)
    outs = []
    for kv in range(N_KV):
        ks = slice(kv * HEAD_DIM, (kv + 1) * HEAD_DIM)
        h0 = kv * GQA_G
        qs = jnp.concatenate([q[:, (h0 + g) * HEAD_DIM:(h0 + g + 1) * HEAD_DIM] for g in range(GQA_G)], axis=0)
        sink = jnp.full((GQA_G * nq, 1), sink_ref[h0], F32)
        for g in range(1, GQA_G):
            sink = jnp.where(row >= g * nq, sink_ref[h0 + g], sink)
        s = _qk(qs, k[:, ks])
        if ok is not None:
            s = jnp.where(ok, s, -jnp.inf)
        o = _softmax_av(s, v[:, ks], sink)
        outs += [o[g * nq:(g + 1) * nq] for g in range(GQA_G)]
    return jnp.concatenate(outs, axis=1)


def _ctx_attn_kernel(sink_ref, q_ref, k_ref, v_ref, o_ref):
    o_ref[...] = _gqa_attend(q_ref[...], k_ref[...], v_ref[...], sink_ref, None).astype(BF16)


def _ctx_attn(st, q, k, v, sink):
    nq, nk = N_HEADS * HEAD_DIM, N_KV * HEAD_DIM
    seq = st.seq
    return pl.pallas_call(
        _ctx_attn_kernel,
        grid=(st.batch,),
        in_specs=[pl.BlockSpec(memory_space=pltpu.SMEM),
                  pl.BlockSpec((seq, nq), lambda b: (b, 0)),
                  pl.BlockSpec((seq, nk), lambda b: (b, 0)),
                  pl.BlockSpec((seq, nk), lambda b: (b, 0))],
        out_specs=pl.BlockSpec((seq, nq), lambda b: (b, 0)),
        out_shape=jax.ShapeDtypeStruct((st.n_tok, nq), BF16),
        compiler_params=_cparams(("arbitrary",), VMEM_LIMIT_BYTES),
        name="context_attention",
    )(sink, q, k, v)


def _lat_attn_kernel(sink_ref, q_ref, k_ref, v_ref, kc_ref, vc_ref, o_ref, *, n_blocks):
    j = pl.program_id(1)
    w = WINDOW
    jp = jnp.maximum(j - 1, 0)
    jn = jnp.minimum(j + 1, n_blocks - 1)

    def rows(ref, blk):
        return ref[pl.ds(pl.multiple_of(blk * w, w), w), :]

    kcat = jnp.concatenate([rows(k_ref, jp), rows(k_ref, j), rows(k_ref, jn), kc_ref[...].astype(BF16)], axis=0)
    vcat = jnp.concatenate([rows(v_ref, jp), rows(v_ref, j), rows(v_ref, jn), vc_ref[...].astype(BF16)], axis=0)
    n_keys = kcat.shape[0]
    qi = lax.broadcasted_iota(I32, (GQA_G * w, n_keys), 0) & (w - 1)
    ki = lax.broadcasted_iota(I32, (GQA_G * w, n_keys), 1)
    ok = (((ki < w) & (j > 0) & (ki >= qi))
          | ((ki >= w) & (ki < 2 * w))
          | ((ki >= 2 * w) & (ki < 3 * w) & (j < n_blocks - 1) & (ki - 2 * w <= qi))
          | (ki >= 3 * w))
    o_ref[...] = _gqa_attend(q_ref[...], kcat, vcat, sink_ref, ok).astype(BF16)


def _lat_attn(st, q, k, v, k_ctx, v_ctx, sink):
    nq, nk = N_HEADS * HEAD_DIM, N_KV * HEAD_DIM
    seq, past = st.seq, k_ctx.shape[0] // st.batch
    n_blocks = seq // WINDOW
    return pl.pallas_call(
        functools.partial(_lat_attn_kernel, n_blocks=n_blocks),
        grid=(st.batch, n_blocks),
        in_specs=[pl.BlockSpec(memory_space=pltpu.SMEM),
                  pl.BlockSpec((WINDOW, nq), lambda b, j: (b * n_blocks + j, 0)),
                  pl.BlockSpec((seq, nk), lambda b, j: (b, 0)),
                  pl.BlockSpec((seq, nk), lambda b, j: (b, 0)),
                  pl.BlockSpec((past, nk), lambda b, j: (b, 0)),
                  pl.BlockSpec((past, nk), lambda b, j: (b, 0))],
        out_specs=pl.BlockSpec((WINDOW, nq), lambda b, j: (b * n_blocks + j, 0)),
        out_shape=jax.ShapeDtypeStruct((st.n_tok, nq), BF16),
        compiler_params=_cparams(("arbitrary", "arbitrary"), VMEM_LIMIT_BYTES),
        name="latent_attention",
    )(sink, q, k, v, k_ctx, v_ctx)


def _fourier_kernel(x_ref, mod_ref, g_ref, cs_ref, ss_ref, cc_ref, sc_ref, o_ref, *, scale):
    m = mod_ref[...]
    h = _norm_mod(x_ref[...], g_ref[...], _mod_slice(m, 0), _mod_slice(m, 1)).astype(BF16)
    p = _dot(cs_ref[...], h).astype(BF16)
    q = _dot(ss_ref[...], h).astype(BF16)
    gw = FOURIER_GW
    cc, sc = cc_ref[...], sc_ref[...]
    outs = []
    for g in range(FOURIER_GROUPS):
        sl = slice(g * gw, (g + 1) * gw)
        outs.append(_dot(p[:, sl], cc) - _dot(q[:, sl], sc))
    o_ref[...] = (jnp.concatenate(outs, axis=1) * scale).astype(BF16)


def _dft_tables(n):
    k = jnp.arange(n, dtype=I32)
    ang = ((k[:, None] * k[None, :]) % n).astype(F32) * (2.0 * math.pi / n)
    return jnp.cos(ang).astype(BF16), jnp.sin(ang).astype(BF16)


def _fourier(st, x, mods, layer, g1):
    seq = st.seq
    cs, ss = _dft_tables(seq)
    cc, sc = _dft_tables(FOURIER_GW)
    mod_spec = pl.BlockSpec((None, None, 1, N_MOD * D), lambda b: (layer, 0 if st.shared else 1 + b, 0, 0))
    return pl.pallas_call(
        functools.partial(_fourier_kernel, scale=1.0 / math.sqrt(seq * FOURIER_GW)),
        grid=(st.batch,),
        in_specs=[pl.BlockSpec((seq, D), lambda b: (b, 0)), mod_spec, _const_spec((1, D)),
                  _const_spec((seq, seq)), _const_spec((seq, seq)),
                  _const_spec((FOURIER_GW, FOURIER_GW)), _const_spec((FOURIER_GW, FOURIER_GW))],
        out_specs=pl.BlockSpec((seq, D), lambda b: (b, 0)),
        out_shape=jax.ShapeDtypeStruct((st.n_tok, D), BF16),
        compiler_params=_cparams(("arbitrary",), VMEM_LIMIT_BYTES),
        name="fourier",
    )(x, mods, g1, cs, ss, cc, sc)


def kernel(x_prompt, x_sample, state_rglru, cache_k, cache_v, c, c_ctx, mod_w, mod_b, norm1_g, norm2_g,
           rg_w_in, rg_conv_w, rg_conv_b, rg_w_a, rg_b_a, rg_w_x, rg_b_x, rg_lambda, rg_w_out,
           at_w_qkv, at_q_norm, at_k_norm, at_sink, at_w_o, ft_w, moe_router, moe_w_gate, moe_w_up, moe_w_down):
    depth = mod_w.shape[0]
    batch, seq, _ = x_prompt.shape
    dec_batch, dec_seq, _ = x_sample.shape
    assert 1 + dec_batch <= MOD_ROWS
    streams = (_Stream(batch, seq, True), _Stream(dec_batch, dec_seq, False))
    cond = jnp.concatenate([c_ctx[None, :], c, jnp.zeros((MOD_ROWS - 1 - dec_batch, D), F32)], axis=0)
    mods = _modulation(cond, mod_w, mod_b).reshape(depth, MOD_ROWS, 1, N_MOD * D)

    xs = [x_prompt.reshape(batch * seq, D), x_sample.reshape(dec_batch * dec_seq, D)]
    new_rg, new_k, new_v = [], [], []
    n_mixers = 3
    for layer in range(depth):
        kind, j = layer % n_mixers, layer // n_mixers
        g1 = norm1_g[layer].reshape(1, D)
        g2 = norm2_g[layer].reshape(1, D)
        wr = _router_pieces(moe_router[layer])
        for si, st in enumerate(streams):
            x = xs[si]
            if kind == 0:
                gate, u = _rg_in(st, x, mods, layer, g1, rg_w_in[j].astype(BF16))
                if st.shared:
                    h0 = jnp.zeros((2, st.batch, D), F32)
                else:
                    h0 = jnp.transpose(state_rglru[:, j], (1, 0, 2))
                h, fin = _rg_scan(st, u, h0, rg_conv_w[j], rg_conv_b[j], rg_w_a[j], rg_b_a[j],
                                  rg_w_x[j], rg_b_x[j], rg_lambda[j])
                if st.shared:
                    new_rg.append(jnp.transpose(fin, (1, 0, 2)))
                x1, xnb, aff =_lin_out(st, (h, gate), rg_w_out[j].astype(BF16), x, mods, layer, g2, wr, True)
            elif kind == 1:
                w_qkv = at_w_qkv[j].astype(BF16)
                if st.shared:
                    q, k, v, kc, vc = _qkv(st, x, mods, layer, g1, w_qkv, at_q_norm[j], at_k_norm[j], False)
                    new_k.append(kc.reshape(st.batch, st.seq, N_KV, HEAD_DIM))
                    new_v.append(vc.reshape(st.batch, st.seq, N_KV, HEAD_DIM))
                    a = _ctx_attn(st, q, k, v, at_sink[j])
                else:
                    q, k, v = _qkv(st, x, mods, layer, g1, w_qkv, at_q_norm[j], at_k_norm[j], True)
                    nk = N_KV * HEAD_DIM
                    a = _lat_attn(st, q, k, v, cache_k[:, j].reshape(-1, nk), cache_v[:, j].reshape(-1, nk),
                                  at_sink[j])
                x1, xnb, aff =_lin_out(st, (a,), at_w_o[j].astype(BF16), x, mods, layer, g2, wr, False)
            else:
                a = _fourier(st, x, mods, layer, g1)
                x1, xnb, aff =_lin_out(st, (a,), ft_w[j].astype(BF16), x, mods, layer, g2, wr, False)
            xs[si] = _moe(st, x1, xnb, aff, mods, layer, moe_w_gate, moe_w_up, moe_w_down)
    return (xs[0].reshape(batch, seq, D), xs[1].reshape(dec_batch, dec_seq, D),
            jnp.stack(new_rg, axis=1), jnp.stack(new_k, axis=1), jnp.stack(new_v, axis=1))
```

```python
import functools
import math

import numpy as np
import jax
import jax.numpy as jnp
from jax import lax
from jax.experimental import pallas as pl
from jax.experimental.pallas import tpu as pltpu

F32 = jnp.float32
BF16 = jnp.bfloat16
I32 = jnp.int32
U32 = jnp.uint32

D = 1024
N_MOD = 6
EPS = 1e-6
GRID_W = 64
CONV_W = 4
CONV_LEFT = 2
LRU_C = 8.0
RNN_BLOCKS = 16
RNN_BLOCK = D // RNN_BLOCKS
N_HEADS = 16
N_KV = 4
HEAD_DIM = 64
GQA_G = N_HEADS // N_KV
WINDOW = 128
ROPE_BASE = 10000.0
FOURIER_GROUPS = 4
FOURIER_GW = D // FOURIER_GROUPS
N_EXPERTS = 16
EC_FACTOR = 2

LANES = 128
SUBLANES = 8
VMEM_LIMIT_BYTES = 56 * 1024 * 1024

TOK_TILE = 256
MOD_ROWS = 16
SCAN_CC = 128
SCAN_TC = 128
FF_CHUNK = 512


def _cparams(sem, vmem=None):
    return pltpu.CompilerParams(dimension_semantics=sem, vmem_limit_bytes=vmem)


def _split2(a):
    hi = a.astype(BF16)
    lo = (a - hi.astype(F32)).astype(BF16)
    return hi, lo


def _dot(a, b):
    return jnp.dot(a, b, preferred_element_type=F32)


def _dot3(a, b):
    a_hi, a_lo = _split2(a)
    b_hi, b_lo = _split2(b)
    return _dot(a_hi, b_hi) + _dot(a_hi, b_lo) + _dot(a_lo, b_hi)


def _norm_mod(x, g, shift, scale):
    ms = jnp.mean(x * x, axis=-1, keepdims=True)
    y = x * lax.rsqrt(ms + EPS) * g
    return y * (1.0 + scale) + shift


def _mod_slice(m, k):
    return m[:, k * D:(k + 1) * D]


class _Stream:
    def __init__(self, batch, seq, shared_cond):
        self.batch, self.seq, self.shared = batch, seq, shared_cond
        self.n_tok = batch * seq
        self.tiles = self.n_tok // TOK_TILE
        self.tiles_per_seq = seq // TOK_TILE

    def mod_row(self, i):
        return 0 if self.shared else 1 + i // self.tiles_per_seq

    def mod_spec(self, layer):
        return pl.BlockSpec((None, None, 1, N_MOD * D), lambda i: (layer, self.mod_row(i), 0, 0))

    def seq_major_spec(self):
        tps = self.tiles_per_seq
        return pl.BlockSpec((TOK_TILE, D), lambda i: (i % tps, i // tps))


def _tok_spec(width=D):
    return pl.BlockSpec((TOK_TILE, width), lambda i: (i, 0))


def _const_spec(shape):
    nd = len(shape)
    return pl.BlockSpec(shape, lambda i: (0,) * nd)


def _mod_kernel(c_ref, w_ref, b_ref, o_ref):
    c = c_ref[...]
    c = c * jax.nn.sigmoid(c)
    o_ref[...] = _dot3(c, w_ref[...]) + b_ref[...]


def _modulation(cond, mod_w, mod_b):
    depth = mod_w.shape[0]
    tn = N_MOD * D // 4
    return pl.pallas_call(
        _mod_kernel,
        grid=(depth, N_MOD * D // tn),
        in_specs=[pl.BlockSpec((MOD_ROWS, D), lambda l, n: (0, 0)),
                  pl.BlockSpec((None, D, tn), lambda l, n: (l, 0, n)),
                  pl.BlockSpec((None, 1, tn), lambda l, n: (l, 0, n))],
        out_specs=pl.BlockSpec((None, MOD_ROWS, tn), lambda l, n: (l, 0, n)),
        out_shape=jax.ShapeDtypeStruct((depth, MOD_ROWS, N_MOD * D), F32),
        compiler_params=_cparams(("arbitrary", "arbitrary"), VMEM_LIMIT_BYTES),
        name="modulation",
    )(cond, mod_w, mod_b.reshape(depth, 1, N_MOD * D))


def _rg_in_kernel(x_ref, mod_ref, g_ref, w_ref, gate_ref, u_ref):
    m = mod_ref[...]
    h = _norm_mod(x_ref[...], g_ref[...], _mod_slice(m, 0), _mod_slice(m, 1))
    gu = _dot(h.astype(BF16), w_ref[...])
    gate_ref[...] = gu[:, :D].astype(BF16)
    u_ref[...] = gu[:, D:]


def _rg_in(st, x, mods, layer, g1, w_in):
    shape = (st.seq, st.batch * D)
    return pl.pallas_call(
        _rg_in_kernel,
        grid=(st.tiles,),
        in_specs=[_tok_spec(), st.mod_spec(layer), _const_spec((1, D)), _const_spec((D, 2 * D))],
        out_specs=[st.seq_major_spec(), st.seq_major_spec()],
        out_shape=[jax.ShapeDtypeStruct(shape, BF16), jax.ShapeDtypeStruct(shape, F32)],
        compiler_params=_cparams(("arbitrary",), VMEM_LIMIT_BYTES),
        name="rg_in",
    )(x, mods, g1, w_in)


def _softplus(z):
    return jnp.maximum(z, 0.0) + jnp.log1p(jnp.exp(-jnp.abs(z)))


def _rg_scan_kernel(u_ref, h0_ref, cw_ref, cb_ref, wg_ref, bg_ref, lam_ref, h_ref, fin_ref,
                    upad, a_f, b_f, a_b, b_b, *, seq):
    cc = u_ref.shape[-1]
    pad_hi = CONV_W - 1 - CONV_LEFT
    upad[0:CONV_LEFT] = jnp.zeros((CONV_LEFT, SUBLANES, cc), F32)
    upad[CONV_LEFT:CONV_LEFT + seq] = u_ref[...]
    upad[CONV_LEFT + seq:CONV_LEFT + seq + pad_hi] = jnp.zeros((pad_hi, SUBLANES, cc), F32)
    k2 = (-0.5 * LRU_C * math.log2(math.e)) * _softplus(-lam_ref[...])
    cw = cw_ref[...]
    rows = SCAN_TC * SUBLANES

    def coef(c, carry):
        t0 = pl.multiple_of(c * SCAN_TC, SCAN_TC)
        uc = cb_ref[...] + cw[0:1] * upad[pl.ds(t0, SCAN_TC)]
        for k in range(1, CONV_W):
            uc = uc + cw[k:k + 1] * upad[pl.ds(t0 + k, SCAN_TC)]
        u2 = uc.reshape(rows, cc)
        gth = jnp.tanh(_dot(u2.astype(BF16), wg_ref[...]) + bg_ref[...])
        hu2 = 0.5 * u2
        for d, (a_s, b_s) in enumerate(((a_f, b_f), (a_b, b_b))):
            r_th = gth[:, (2 * d) * cc:(2 * d + 1) * cc]
            i_th = gth[:, (2 * d + 1) * cc:(2 * d + 2) * cc]
            a = jnp.exp2(k2[d] * r_th + k2[d])
            q = 1.0 - a * a
            root = jnp.where(q > 0.0, q * lax.rsqrt(q), 0.0)
            a_s[pl.ds(t0, SCAN_TC)] = a.reshape(SCAN_TC, SUBLANES, cc)
            b_s[pl.ds(t0, SCAN_TC)] = (root * (i_th * hu2 + hu2)).reshape(SCAN_TC, SUBLANES, cc)
        return carry

    lax.fori_loop(0, seq // SCAN_TC, coef, 0)

    blk = SUBLANES

    def step(i, carry):
        hf, hb = carry
        t0 = pl.multiple_of(i * blk, blk)
        av, bv = a_f[pl.ds(t0, blk)], b_f[pl.ds(t0, blk)]
        outs = []
        for k in range(0, blk, 2):
            a0, b0, a1, b1 = av[k], bv[k], av[k + 1], bv[k + 1]
            outs.append(a0 * hf + b0)
            hf = (a1 * a0) * hf + (a1 * b0 + b1)
            outs.append(hf)
        b_f[pl.ds(t0, blk)] = jnp.stack(outs, axis=0)
        s0 = seq - blk - t0
        cv, dv = a_b[pl.ds(s0, blk)], b_b[pl.ds(s0, blk)]
        outs = [None] * blk
        for k in range(blk - 1, 0, -2):
            c0, d0, c1, d1 = cv[k], dv[k], cv[k - 1], dv[k - 1]
            outs[k] = c0 * hb + d0
            hb = (c1 * c0) * hb + (c1 * d0 + d1)
            outs[k - 1] = hb
        b_b[pl.ds(s0, blk)] = jnp.stack(outs, axis=0)
        return hf, hb

    hf, hb = lax.fori_loop(0, seq // blk, step, (h0_ref[0], h0_ref[1]))
    fin_ref[0] = hf
    fin_ref[1] = hb
    h_ref[...] = b_f[...] + b_b[...]


def _blockdiag_pairs(w):
    per = SCAN_CC // RNN_BLOCK
    w4 = w.reshape(D // SCAN_CC, per, RNN_BLOCK, RNN_BLOCK)
    eye = jnp.eye(per, dtype=w.dtype)
    return jnp.einsum('cipq,ij->cipjq', w4, eye).reshape(D // SCAN_CC, SCAN_CC, SCAN_CC)


def _rg_scan(st, u, h0, conv_w, conv_b, w_a, b_a, w_x, b_x, lam):
    seq, batch = st.seq, st.batch
    n_cc = D // SCAN_CC
    wg = (0.5 * jnp.concatenate([_blockdiag_pairs(w_a[0]), _blockdiag_pairs(w_x[0]),
                                 _blockdiag_pairs(w_a[1]), _blockdiag_pairs(w_x[1])], axis=-1)).astype(BF16)
    bg = 0.5 * jnp.concatenate([b.reshape(n_cc, 1, SCAN_CC) for b in (b_a[0], b_x[0], b_a[1], b_x[1])], axis=-1)
    blk = (seq, SUBLANES, SCAN_CC)
    scr = pltpu.VMEM(blk, F32)
    h, fin = pl.pallas_call(
        functools.partial(_rg_scan_kernel, seq=seq),
        grid=(batch // SUBLANES, n_cc),
        in_specs=[pl.BlockSpec(blk, lambda b, c: (0, b, c)),
                  pl.BlockSpec((2, SUBLANES, SCAN_CC), lambda b, c: (0, b, c)),
                  pl.BlockSpec((CONV_W, 1, SCAN_CC), lambda b, c: (0, 0, c)),
                  pl.BlockSpec((1, 1, SCAN_CC), lambda b, c: (0, 0, c)),
                  pl.BlockSpec((None, SCAN_CC, 4 * SCAN_CC), lambda b, c: (c, 0, 0)),
                  pl.BlockSpec((None, 1, 4 * SCAN_CC), lambda b, c: (c, 0, 0)),
                  pl.BlockSpec((2, 1, SCAN_CC), lambda b, c: (0, 0, c))],
        out_specs=[pl.BlockSpec(blk, lambda b, c: (0, b, c)),
                   pl.BlockSpec((2, SUBLANES, SCAN_CC), lambda b, c: (0, b, c))],
        out_shape=[jax.ShapeDtypeStruct((seq, batch, D), F32),
                   jax.ShapeDtypeStruct((2, batch, D), F32)],
        scratch_shapes=[pltpu.VMEM((seq + CONV_W - 1, SUBLANES, SCAN_CC), F32), scr, scr, scr, scr],
        compiler_params=_cparams(("arbitrary", "arbitrary"), VMEM_LIMIT_BYTES),
        name="rg_scan",
    )(u.reshape(seq, batch, D), h0, conv_w.reshape(CONV_W, 1, D), conv_b.reshape(1, 1, D),
      wg, bg, lam.reshape(2, 1, D))
    return h.reshape(seq, batch * D), fin


LIN_CHUNK = LANES
LIN_CHUNKS = TOK_TILE // LIN_CHUNK


def _chunk_rows(k):
    return slice(k * LIN_CHUNK, (k + 1) * LIN_CHUNK)


def _route_and_pack(ys, m, x_ref, g2_ref, wr_ref, xo_ref, xnb_ref, aff_ref, slab_s):
    n_ch = D // LANES
    ks = range(LIN_CHUNKS)
    xs = [x_ref[_chunk_rows(k), :] + _mod_slice(m, 2) * ys[k] for k in ks]
    for k in ks:
        xo_ref[_chunk_rows(k), :] = xs[k]
    xns = [_norm_mod(x, g2_ref[...], _mod_slice(m, 3), _mod_slice(m, 4)) for x in xs]
    his = [xn.astype(BF16) for xn in xns]
    los = [(xn - hi.astype(F32)).astype(BF16) for xn, hi in zip(xns, his)]
    wr = wr_ref[...]
    l1s = [_qk(wr, hi) for hi in his]
    l2s = [_qk(wr, lo) for lo in los]
    for k in ks:
        for j in range(n_ch):
            slab_s[pl.ds(k * LIN_CHUNK * n_ch + j, LIN_CHUNK, stride=n_ch), :] = xns[k][:, j * LANES:(j + 1) * LANES]
    xnb_ref[...] = slab_s[...].astype(BF16).reshape(xnb_ref.shape)
    e = N_EXPERTS
    for k in ks:
        l1, l2 = l1s[k], l2s[k]
        logit = l1[0:e] + l1[e:2 * e] + l1[2 * e:3 * e] + l2[0:e] + l2[e:2 * e]
        mx = jnp.max(logit, axis=0, keepdims=True)
        ex = jnp.exp(logit - mx)
        aff_ref[k] = ex / jnp.sum(ex, axis=0, keepdims=True)


def _lin_out_kernel(a_ref, w_ref, x_ref, mod_ref, g2_ref, wr_ref, xo_ref, xnb_ref, aff_ref, slab_s):
    ys = [_dot(a_ref[_chunk_rows(k), :], w_ref[...]) for k in range(LIN_CHUNKS)]
    _route_and_pack(ys, mod_ref[...], x_ref, g2_ref, wr_ref, xo_ref, xnb_ref, aff_ref, slab_s)


def _lin_out_gated_kernel(h_ref, gate_ref, w_ref, x_ref, mod_ref, g2_ref, wr_ref, xo_ref, xnb_ref, aff_ref,
                          slab_s):
    acts = [(h_ref[_chunk_rows(k), :] * jax.nn.gelu(gate_ref[_chunk_rows(k), :].astype(F32))).astype(BF16)
            for k in range(LIN_CHUNKS)]
    ys = [_dot(a, w_ref[...]) for a in acts]
    _route_and_pack(ys, mod_ref[...], x_ref, g2_ref, wr_ref, xo_ref, xnb_ref, aff_ref, slab_s)


def _router_pieces(w_router):
    hi = w_router.astype(BF16)
    r1 = w_router - hi.astype(F32)
    mid = r1.astype(BF16)
    lo = (r1 - mid.astype(F32)).astype(BF16)
    return jnp.concatenate([hi, mid, lo], axis=1).T


def _lin_out(st, srcs, w, x, mods, layer, g2, wr, gated):
    n_chunks = st.n_tok // LANES
    if gated:
        body, src_specs = _lin_out_gated_kernel, [st.seq_major_spec(), st.seq_major_spec()]
    else:
        body, src_specs = _lin_out_kernel, [_tok_spec()]
    return pl.pallas_call(
        body,
        grid=(st.tiles,),
        in_specs=src_specs + [_const_spec((D, D)), _tok_spec(), st.mod_spec(layer), _const_spec((1, D)),
                              _const_spec((3 * N_EXPERTS, D))],
        out_specs=[_tok_spec(), pl.BlockSpec((TOK_TILE // 2, 2 * (D // LANES), LANES), lambda i: (i, 0, 0)),
                   pl.BlockSpec((TOK_TILE // LANES, N_EXPERTS, LANES), lambda i: (i, 0, 0))],
        out_shape=[jax.ShapeDtypeStruct((st.n_tok, D), F32),
                   jax.ShapeDtypeStruct((st.n_tok // 2, 2 * (D // LANES), LANES), BF16),
                   jax.ShapeDtypeStruct((n_chunks, N_EXPERTS, LANES), F32)],
        scratch_shapes=[pltpu.VMEM((TOK_TILE * (D // LANES), LANES), F32)],
        compiler_params=_cparams(("arbitrary",), VMEM_LIMIT_BYTES),
        name="lin_out_gated" if gated else "lin_out",
    )(*srcs, w, x, mods, g2, wr)


def _select_kernel(a_ref, idx_ref, g_ref, linc_s, cnt_s, crow_s, *, cap):
    n_e, n_ch, _ = a_ref.shape
    assert n_ch & (n_ch - 1) == 0 and n_ch <= LANES
    rows = n_e * n_ch
    a3 = a_ref[...]
    capf = jnp.float32(cap)

    def count(mask3):
        c = jnp.sum(mask3.astype(F32), axis=1, keepdims=True)
        return jnp.sum(c, axis=2, keepdims=True)

    def as_f32(bits):
        return pltpu.bitcast(bits, F32)

    def search(i, thr):
        cand = thr | (jnp.int32(1) << (30 - i))
        return jnp.where(count(a3 >= as_f32(cand)) >= capf, cand, thr)

    thr = lax.fori_loop(0, 31, search, jnp.zeros((n_e, 1, LANES), I32))
    gt3 = a3 >= as_f32(thr + 1)
    eq3 = jnp.logical_and(a3 >= as_f32(thr), jnp.logical_not(gt3))
    need = capf - count(gt3)

    li = lax.broadcasted_iota(I32, (LANES, LANES), 0)
    lj = lax.broadcasted_iota(I32, (LANES, LANES), 1)
    upper = (li <= lj).astype(BF16)
    ones = jnp.ones((LANES, LANES), BF16)
    ri = lax.broadcasted_iota(I32, (rows, rows), 0)
    rj = lax.broadcasted_iota(I32, (rows, rows), 1)
    sh = n_ch.bit_length() - 1
    before = (((ri >> sh) == (rj >> sh)) & (rj < ri)).astype(BF16)

    def prefixes(mask3):
        x = mask3.reshape(rows, LANES).astype(BF16)
        local = _dot(x, upper)
        tot = _dot(x, ones)
        return local, tot, _dot(before, tot.astype(BF16))

    gt_l, gt_t, gt_x = prefixes(gt3)
    eq_l, eq_t, eq_x = prefixes(eq3)
    need_r = jnp.broadcast_to(need, (n_e, n_ch, LANES)).reshape(rows, LANES)
    sel_incl = gt_x + gt_l + jnp.minimum(eq_x + eq_l, need_r)
    sel_x = gt_x + jnp.minimum(eq_x, need_r)
    sel_c = gt_x + gt_t + jnp.minimum(eq_x + eq_t, need_r)
    linc_s[...] = sel_incl - sel_x
    cnt_s[...] = sel_c - sel_x
    c3 = sel_c.reshape(n_e, n_ch, LANES)
    pick = lax.broadcasted_iota(I32, (1, n_ch, LANES), 1) == lax.broadcasted_iota(I32, (1, n_ch, LANES), 2)
    crow = jnp.sum(jnp.where(pick, c3, 0.0), axis=1)
    crow = jnp.where(lax.broadcasted_iota(I32, crow.shape, 1) < n_ch, crow, jnp.float32(2 * cap + n_ch * LANES))
    crow_s[...] = crow

    slot = lax.broadcasted_iota(I32, (cap, LANES), 0).astype(F32)
    lane = lax.broadcasted_iota(I32, (cap, LANES), 1).astype(F32)
    diag = lax.broadcasted_iota(I32, (LANES, LANES), 0) == lax.broadcasted_iota(I32, (LANES, LANES), 1)
    zpad = jnp.zeros((LANES - n_ch, LANES), BF16)

    def per_expert(e, carry):
        r0 = pl.multiple_of(e * n_ch, n_ch)
        passed = (crow_s[pl.ds(e, 1), :] <= slot).astype(BF16)
        cnt_e = cnt_s[pl.ds(r0, n_ch), :].astype(BF16)
        linc_e = linc_s[pl.ds(r0, n_ch), :].astype(BF16)
        a_e = a_ref[e]
        a_hi = a_e.astype(BF16)
        r1 = a_e - a_hi.astype(F32)
        a_mid = r1.astype(BF16)
        a_lo = (r1 - a_mid.astype(F32)).astype(BF16)
        zpad_w = jnp.zeros((LANES - n_ch, 4 * LANES), BF16)
        by_passed = jnp.concatenate([jnp.concatenate([ones[:n_ch], cnt_e], axis=1), zpad_w[:, :2 * LANES]], axis=0)
        by_chunk = jnp.concatenate([jnp.concatenate([linc_e, a_hi, a_mid, a_lo], axis=1), zpad_w], axis=0)
        res = _dot(passed, by_passed)
        chunk = res[:, :LANES]
        rank = slot - res[:, LANES:]
        onehot = (chunk == lane).astype(BF16)
        res = _dot(onehot, by_chunk)
        g = res[:, :LANES]
        within = _dot((g <= rank).astype(BF16), ones)
        tok = chunk * jnp.float32(LANES) + within
        arow = res[:, LANES:2 * LANES] + res[:, 2 * LANES:3 * LANES] + res[:, 3 * LANES:]
        gate = jnp.sum(jnp.where(lane == within, arow, 0.0), axis=1, keepdims=True)
        gate = jnp.broadcast_to(gate, (cap, LANES))

        def to_row(col):
            pieces = []
            for b in range(cap // LANES):
                blk = col[b * LANES:(b + 1) * LANES, :]
                pieces.append(jnp.sum(jnp.where(diag, blk, 0.0), axis=0, keepdims=True))
            return jnp.concatenate(pieces, axis=1)

        idx_ref[pl.ds(e, 1), :] = to_row(tok).astype(I32)
        g_ref[pl.ds(e, 1), :] = to_row(gate)
        return carry

    lax.fori_loop(0, n_e, per_expert, 0)


def _select(aff_chunks, cap):
    n_ch = aff_chunks.shape[0]
    a = jnp.transpose(aff_chunks, (1, 0, 2))
    rows = N_EXPERTS * n_ch
    return pl.pallas_call(
        functools.partial(_select_kernel, cap=cap),
        out_shape=[jax.ShapeDtypeStruct((N_EXPERTS, cap), I32),
                   jax.ShapeDtypeStruct((N_EXPERTS, cap), F32)],
        scratch_shapes=[pltpu.VMEM((rows, LANES), F32), pltpu.VMEM((rows, LANES), F32),
                        pltpu.VMEM((N_EXPERTS, LANES), F32)],
        compiler_params=pltpu.CompilerParams(vmem_limit_bytes=VMEM_LIMIT_BYTES),
        name="select",
    )(a)


ROW_CH = D // LANES
GATHER_UNROLL = 16
FFN_ROWS = 512


def _cm_stride(m):
    return m + SUBLANES


def _moe_ffn_kernel(idx_ref, src_ref, g_ref, wg_ref, wu_ref, wd_ref, o_ref, tile_s, xe_s, *, cap):
    e = pl.program_id(0)
    f = pl.program_id(1)
    stride = _cm_stride(cap)
    tm = min(FFN_ROWS, cap)

    def gather_token(p, n):
        pair = src_ref[n >> 1].astype(F32)
        odd = (jnp.full((ROW_CH, LANES), n, I32) & 1) == 1
        tile_s[pl.ds(p, ROW_CH, stride=stride), :] = jnp.where(odd, pair[ROW_CH:], pair[:ROW_CH])

    @pl.when(jnp.logical_and(e == 0, f == 0))
    def _first_gather():
        def body(gi, carry):
            base = pl.multiple_of(gi * GATHER_UNROLL, GATHER_UNROLL)
            for k in range(GATHER_UNROLL):
                gather_token(base + k, idx_ref[base + k])
            return carry

        lax.fori_loop(0, cap // GATHER_UNROLL, body, 0)

    def ffn(xt):
        hg = _dot(xt, wg_ref[...].astype(BF16))
        hu = _dot(xt, wu_ref[...].astype(BF16))
        h = (hg * jax.nn.sigmoid(hg) * hu).astype(BF16)
        return _dot(h, wd_ref[...].astype(BF16))

    @pl.when(f == 0)
    def _half0():
        for t in range(cap // tm):
            r0 = t * tm
            xt = jnp.concatenate([tile_s[pl.ds(j * stride + r0, tm), :].astype(BF16) for j in range(ROW_CH)],
                                 axis=1)
            xe_s[r0:r0 + tm, :] = xt
            y = ffn(xt)
            for j in range(ROW_CH):
                o_ref[pl.ds(j * stride + r0, tm), :] = y[:, j * LANES:(j + 1) * LANES]

    @pl.when(f == 1)
    def _half1():
        nxt = (e + 1) * cap
        for t in range(cap // tm):
            r0 = t * tm
            def gather_share(q):
                for p in range(r0 + q * tm // 4, r0 + (q + 1) * tm // 4):
                    gather_token(p, idx_ref[nxt + p])

            xt = xe_s[r0:r0 + tm, :]
            gather_share(0)
            hg = _dot(xt, wg_ref[...].astype(BF16))
            gather_share(1)
            hu = _dot(xt, wu_ref[...].astype(BF16))
            h = (hg * jax.nn.sigmoid(hg) * hu).astype(BF16)
            gather_share(2)
            y = _dot(h, wd_ref[...].astype(BF16))
            gather_share(3)
            gt = g_ref[r0:r0 + tm, :]
            for j in range(ROW_CH):
                rows = pl.ds(j * stride + r0, tm)
                o_ref[rows, :] = (o_ref[rows, :] + y[:, j * LANES:(j + 1) * LANES]) * gt
        for j in range(ROW_CH):
            o_ref[pl.ds(j * stride + cap, stride - cap), :] = jnp.zeros((stride - cap, LANES), F32)


def _moe_ffn(idx, gates, xnb, w_gate, w_up, w_down, layer, cap):
    n_tok = 2 * xnb.shape[0]
    ff = w_gate.shape[-1]
    assert ff == 2 * FF_CHUNK
    stride = _cm_stride(cap)
    src = xnb
    idx = jnp.concatenate([idx, jnp.zeros((cap,), I32)])
    grid_spec = pltpu.PrefetchScalarGridSpec(
        num_scalar_prefetch=1,
        grid=(N_EXPERTS, ff // FF_CHUNK),
        in_specs=[pl.BlockSpec((n_tok // 2, 2 * ROW_CH, LANES), lambda e, f, idx: (0, 0, 0),
                               pipeline_mode=pl.Buffered(1)),
                  pl.BlockSpec((None, cap, 1), lambda e, f, idx: (e, 0, 0)),
                  pl.BlockSpec((None, None, D, FF_CHUNK), lambda e, f, idx: (layer, e, 0, f)),
                  pl.BlockSpec((None, None, D, FF_CHUNK), lambda e, f, idx: (layer, e, 0, f)),
                  pl.BlockSpec((None, None, FF_CHUNK, D), lambda e, f, idx: (layer, e, f, 0))],
        out_specs=pl.BlockSpec((None, ROW_CH * stride, LANES), lambda e, f, idx: (e, 0, 0)),
        scratch_shapes=[pltpu.VMEM((ROW_CH * stride, LANES), F32), pltpu.VMEM((cap, D), BF16)],
    )
    return pl.pallas_call(
        functools.partial(_moe_ffn_kernel, cap=cap),
        grid_spec=grid_spec,
        out_shape=jax.ShapeDtypeStruct((N_EXPERTS, ROW_CH * stride, LANES), F32),
        compiler_params=_cparams(("arbitrary", "arbitrary"), VMEM_LIMIT_BYTES),
        name="moe_ffn",
    )(idx, src, gates.reshape(N_EXPERTS, cap, 1), w_gate, w_up, w_down)


COMBINE_VMEM_LIMIT_BYTES = 60 * 1024 * 1024


def _combine_kernel(idx_ref, ye_ref, x_ref, mod_ref, o_ref, acc_s, *, cap):
    s = pl.program_id(0)
    stride = _cm_stride(cap)

    @pl.when(s == 0)
    def _zero():
        acc_s[...] = jnp.zeros(acc_s.shape, F32)

    @pl.when(s < N_EXPERTS)
    def _scatter():
        def body(gi, carry):
            base = pl.multiple_of(gi * GATHER_UNROLL, GATHER_UNROLL)
            rows, sums = [], []
            for k in range(GATHER_UNROLL):
                r = pl.multiple_of(idx_ref[s * cap + base + k] * ROW_CH, ROW_CH)
                rows.append(r)
                sums.append(acc_s[pl.ds(r, ROW_CH), :] + ye_ref[pl.ds(base + k, ROW_CH, stride=stride), :])
            for r, v in zip(rows, sums):
                acc_s[pl.ds(r, ROW_CH), :] = v
            return carry

        lax.fori_loop(0, cap // GATHER_UNROLL, body, 0)

    @pl.when(s >= N_EXPERTS)
    def _residual():
        r0 = pl.multiple_of((s - N_EXPERTS) * (TOK_TILE * ROW_CH), TOK_TILE * ROW_CH)
        y = jnp.concatenate([acc_s[pl.ds(r0 + j, TOK_TILE, stride=ROW_CH), :] for j in range(ROW_CH)], axis=1)
        o_ref[...] = x_ref[...] + _mod_slice(mod_ref[...], 5) * y


def _combine(st, idx, ye, x, mods, layer, cap):
    stride = _cm_stride(cap)
    n_e = N_EXPERTS

    def tile(s):
        return jnp.maximum(s - n_e, 0)

    grid_spec = pltpu.PrefetchScalarGridSpec(
        num_scalar_prefetch=1,
        grid=(n_e + st.tiles,),
        in_specs=[pl.BlockSpec((None, ROW_CH * stride, LANES), lambda s, idx: (jnp.minimum(s, n_e - 1), 0, 0)),
                  pl.BlockSpec((TOK_TILE, D), lambda s, idx: (tile(s), 0)),
                  pl.BlockSpec((None, None, 1, N_MOD * D), lambda s, idx: (layer, st.mod_row(tile(s)), 0, 0))],
        out_specs=pl.BlockSpec((TOK_TILE, D), lambda s, idx: (tile(s), 0)),
        scratch_shapes=[pltpu.VMEM((st.n_tok * ROW_CH, LANES), F32)],
    )
    return pl.pallas_call(
        functools.partial(_combine_kernel, cap=cap),
        grid_spec=grid_spec,
        out_shape=jax.ShapeDtypeStruct((st.n_tok, D), F32),
        compiler_params=_cparams(("arbitrary",), COMBINE_VMEM_LIMIT_BYTES),
        name="moe_combine",
    )(idx, ye, x, mods)


def _moe(st, x, xnb, aff_chunks, mods, layer, w_gate, w_up, w_down):
    cap = EC_FACTOR * st.n_tok // N_EXPERTS
    idx, gates = _select(aff_chunks, cap)
    idx = idx.reshape(N_EXPERTS * cap)
    ye = _moe_ffn(idx, gates, xnb, w_gate, w_up, w_down, layer, cap)
    return _combine(st, idx, ye, x, mods, layer, cap)


def _head_sumsq(x, bd):
    x2 = x * x
    hi, lo = _split2(x2)
    w = bd.shape[0]
    cols = []
    for c in range(x.shape[1] // w):
        sl = slice(c * w, (c + 1) * w)
        cols.append(_dot(hi[:, sl], bd) + _dot(lo[:, sl], bd))
    return cols[0] if len(cols) == 1 else jnp.concatenate(cols, axis=1)


def _qk_norm(x, gain, bd):
    ms = _head_sumsq(x, bd) * (1.0 / HEAD_DIM)
    return x * lax.rsqrt(ms + EPS) * gain


def _rope(x, cos, sin_dn, sin_up):
    n = x.shape[1]
    q = HEAD_DIM // 4
    return x * cos + pltpu.roll(x, n - q, 1) * sin_dn + pltpu.roll(x, q, 1) * sin_up


def _qkv_ctx_kernel(x_ref, mod_ref, g_ref, w_ref, qg_ref, kg_ref, bd_ref, q_ref, k_ref, v_ref, kc_ref, vc_ref):
    m = mod_ref[...]
    h = _norm_mod(x_ref[...], g_ref[...], _mod_slice(m, 0), _mod_slice(m, 1))
    qkv = _dot(h.astype(BF16), w_ref[...])
    nq, nk = N_HEADS * HEAD_DIM, N_KV * HEAD_DIM
    bd = bd_ref[...]
    q = _qk_norm(qkv[:, :nq], qg_ref[...], bd)
    k = _qk_norm(qkv[:, nq:nq + nk], kg_ref[...], bd)
    v = qkv[:, nq + nk:]
    q_ref[...] = (q * (HEAD_DIM ** -0.5)).astype(BF16)
    k_ref[...] = k.astype(BF16)
    v_ref[...] = v.astype(BF16)
    kc_ref[...] = k
    vc_ref[...] = v


def _qkv_lat_kernel(x_ref, mod_ref, g_ref, w_ref, qg_ref, kg_ref, bd_ref,
                    cos_ref, sdn_ref, sup_ref, q_ref, k_ref, v_ref):
    m = mod_ref[...]
    h = _norm_mod(x_ref[...], g_ref[...], _mod_slice(m, 0), _mod_slice(m, 1))
    qkv = _dot(h.astype(BF16), w_ref[...])
    nq, nk = N_HEADS * HEAD_DIM, N_KV * HEAD_DIM
    bd = bd_ref[...]
    q = _qk_norm(qkv[:, :nq], qg_ref[...], bd)
    k = _qk_norm(qkv[:, nq:nq + nk], kg_ref[...], bd)
    tabs = (cos_ref[...], sdn_ref[...], sup_ref[...])
    q = _rope(q, *(_tile_lanes(t, nq) for t in tabs))
    k = _rope(k, *(_tile_lanes(t, nk) for t in tabs))
    q_ref[...] = (q * (HEAD_DIM ** -0.5)).astype(BF16)
    k_ref[...] = k.astype(BF16)
    v_ref[...] = qkv[:, nq + nk:].astype(BF16)


def _rope_tables(seq, n_heads):
    n_rows = seq // GRID_W
    row = jnp.repeat(jnp.arange(n_rows), GRID_W).astype(F32)
    col = jnp.tile(jnp.arange(GRID_W), n_rows).astype(F32)
    n_freq = HEAD_DIM // 4
    inv = ROPE_BASE ** (-jnp.arange(n_freq, dtype=F32) / n_freq)
    ar, ac = row[:, None] * inv, col[:, None] * inv
    ang = jnp.concatenate([ar, ar, ac, ac], axis=-1)
    cos, sin = jnp.cos(ang), jnp.sin(ang)
    even = ((jnp.arange(HEAD_DIM) // n_freq) % 2 == 0).astype(F32)
    sin_dn = -sin * even
    sin_up = sin * (1.0 - even)
    return tuple(jnp.tile(a, (1, n_heads)) for a in (cos, sin_dn, sin_up))


def _tile_lanes(t, width):
    return jnp.concatenate([t] * (width // t.shape[1]), axis=1)


def _qkv(st, x, mods, layer, g1, w_qkv, q_gain, k_gain, rope):
    nq, nk = N_HEADS * HEAD_DIM, N_KV * HEAD_DIM
    bd = jnp.asarray(np.kron(np.eye(4, dtype=np.float32), np.ones((HEAD_DIM, HEAD_DIM), np.float32))).astype(BF16)
    qg = jnp.tile(q_gain.reshape(1, HEAD_DIM), (1, N_HEADS))
    kg = jnp.tile(k_gain.reshape(1, HEAD_DIM), (1, N_KV))
    base_specs = [_tok_spec(), st.mod_spec(layer), _const_spec((1, D)), _const_spec((D, nq + 2 * nk)),
                  _const_spec((1, nq)), _const_spec((1, nk)), _const_spec((4 * HEAD_DIM, 4 * HEAD_DIM))]
    outs = [jax.ShapeDtypeStruct((st.n_tok, nq), BF16), jax.ShapeDtypeStruct((st.n_tok, nk), BF16),
            jax.ShapeDtypeStruct((st.n_tok, nk), BF16)]
    out_specs = [_tok_spec(nq), _tok_spec(nk), _tok_spec(nk)]
    if rope:
        tps = st.tiles_per_seq
        tabs = _rope_tables(st.seq, LANES // HEAD_DIM)
        tab_specs = [pl.BlockSpec((TOK_TILE, LANES), lambda i: (i % tps, 0))] * 3
        return pl.pallas_call(
            _qkv_lat_kernel, grid=(st.tiles,), in_specs=base_specs + tab_specs, out_specs=out_specs,
            out_shape=outs, compiler_params=_cparams(("arbitrary",), VMEM_LIMIT_BYTES), name="qkv_latent",
        )(x, mods, g1, w_qkv, qg, kg, bd, *tabs)
    outs += [jax.ShapeDtypeStruct((st.n_tok, nk), F32)] * 2
    out_specs += [_tok_spec(nk)] * 2
    return pl.pallas_call(
        _qkv_ctx_kernel, grid=(st.tiles,), in_specs=base_specs, out_specs=out_specs,
        out_shape=outs, compiler_params=_cparams(("arbitrary",), VMEM_LIMIT_BYTES), name="qkv_context",
    )(x, mods, g1, w_qkv, qg, kg, bd)


def _softmax_av(s, v, sink):
    mx = jnp.maximum(jnp.max(s, axis=1, keepdims=True), sink)
    p = jnp.exp(s - mx)
    den = jnp.sum(p, axis=1, keepdims=True) + jnp.exp(sink - mx)
    return _dot(p.astype(BF16), v) / den


def _qk(q, k):
    return lax.dot_general(q, k, (((1,), (1,)), ((), ())), preferred_element_type=F32)


def _gqa_attend(q, k, v, sink_ref, ok):
    nq = q.shape[0]
    row = lax.broadcasted_iota(I32, (GQA_G * nq, 1), 0)
    outs = []
    for kv in range(N_KV):
        ks = slice(kv * HEAD_DIM, (kv + 1) * HEAD_DIM)
        h0 = kv * GQA_G
        qs = jnp.concatenate([q[:, (h0 + g) * HEAD_DIM:(h0 + g + 1) * HEAD_DIM] for g in range(GQA_G)], axis=0)
        sink = jnp.full((GQA_G * nq, 1), sink_ref[h0], F32)
        for g in range(1, GQA_G):
            sink = jnp.where(row >= g * nq, sink_ref[h0 + g], sink)
        s = _qk(qs, k[:, ks])
        if ok is not None:
            s = jnp.where(ok, s, -jnp.inf)
        o = _softmax_av(s, v[:, ks], sink)
        outs += [o[g * nq:(g + 1) * nq] for g in range(GQA_G)]
    return jnp.concatenate(outs, axis=1)


def _ctx_attn_kernel(sink_ref, q_ref, k_ref, v_ref, o_ref):
    o_ref[...] = _gqa_attend(q_ref[...], k_ref[...], v_ref[...], sink_ref, None).astype(BF16)


def _ctx_attn(st, q, k, v, sink):
    nq, nk = N_HEADS * HEAD_DIM, N_KV * HEAD_DIM
    seq = st.seq
    return pl.pallas_call(
        _ctx_attn_kernel,
        grid=(st.batch,),
        in_specs=[pl.BlockSpec(memory_space=pltpu.SMEM),
                  pl.BlockSpec((seq, nq), lambda b: (b, 0)),
                  pl.BlockSpec((seq, nk), lambda b: (b, 0)),
                  pl.BlockSpec((seq, nk), lambda b: (b, 0))],
        out_specs=pl.BlockSpec((seq, nq), lambda b: (b, 0)),
        out_shape=jax.ShapeDtypeStruct((st.n_tok, nq), BF16),
        compiler_params=_cparams(("arbitrary",), VMEM_LIMIT_BYTES),
        name="context_attention",
    )(sink, q, k, v)


def _lat_attn_kernel(sink_ref, q_ref, k_ref, v_ref, kc_ref, vc_ref, o_ref, *, n_blocks):
    j = pl.program_id(1)
    w = WINDOW
    jp = jnp.maximum(j - 1, 0)
    jn = jnp.minimum(j + 1, n_blocks - 1)

    def rows(ref, blk):
        return ref[pl.ds(pl.multiple_of(blk * w, w), w), :]

    kcat = jnp.concatenate([rows(k_ref, jp), rows(k_ref, j), rows(k_ref, jn), kc_ref[...].astype(BF16)], axis=0)
    vcat = jnp.concatenate([rows(v_ref, jp), rows(v_ref, j), rows(v_ref, jn), vc_ref[...].astype(BF16)], axis=0)
    n_keys = kcat.shape[0]
    qi = lax.broadcasted_iota(I32, (GQA_G * w, n_keys), 0) & (w - 1)
    ki = lax.broadcasted_iota(I32, (GQA_G * w, n_keys), 1)
    ok = (((ki < w) & (j > 0) & (ki >= qi))
          | ((ki >= w) & (ki < 2 * w))
          | ((ki >= 2 * w) & (ki < 3 * w) & (j < n_blocks - 1) & (ki - 2 * w <= qi))
          | (ki >= 3 * w))
    o_ref[...] = _gqa_attend(q_ref[...], kcat, vcat, sink_ref, ok).astype(BF16)


def _lat_attn(st, q, k, v, k_ctx, v_ctx, sink):
    nq, nk = N_HEADS * HEAD_DIM, N_KV * HEAD_DIM
    seq, past = st.seq, k_ctx.shape[0] // st.batch
    n_blocks = seq // WINDOW
    return pl.pallas_call(
        functools.partial(_lat_attn_kernel, n_blocks=n_blocks),
        grid=(st.batch, n_blocks),
        in_specs=[pl.BlockSpec(memory_space=pltpu.SMEM),
                  pl.BlockSpec((WINDOW, nq), lambda b, j: (b * n_blocks + j, 0)),
                  pl.BlockSpec((seq, nk), lambda b, j: (b, 0)),
                  pl.BlockSpec((seq, nk), lambda b, j: (b, 0)),
                  pl.BlockSpec((past, nk), lambda b, j: (b, 0)),
                  pl.BlockSpec((past, nk), lambda b, j: (b, 0))],
        out_specs=pl.BlockSpec((WINDOW, nq), lambda b, j: (b * n_blocks + j, 0)),
        out_shape=jax.ShapeDtypeStruct((st.n_tok, nq), BF16),
        compiler_params=_cparams(("arbitrary", "arbitrary"), VMEM_LIMIT_BYTES),
        name="latent_attention",
    )(sink, q, k, v, k_ctx, v_ctx)


def _fourier_kernel(x_ref, mod_ref, g_ref, cs_ref, ss_ref, cc_ref, sc_ref, o_ref, *, scale):
    m = mod_ref[...]
    h = _norm_mod(x_ref[...], g_ref[...], _mod_slice(m, 0), _mod_slice(m, 1)).astype(BF16)
    p = _dot(cs_ref[...], h).astype(BF16)
    q = _dot(ss_ref[...], h).astype(BF16)
    gw = FOURIER_GW
    cc, sc = cc_ref[...], sc_ref[...]
    outs = []
    for g in range(FOURIER_GROUPS):
        sl = slice(g * gw, (g + 1) * gw)
        outs.append(_dot(p[:, sl], cc) - _dot(q[:, sl], sc))
    o_ref[...] = (jnp.concatenate(outs, axis=1) * scale).astype(BF16)


def _dft_tables(n):
    k = np.arange(n, dtype=np.int64)
    ang = ((k[:, None] * k[None, :]) % n).astype(np.float64) * (2.0 * math.pi / n)
    return jnp.asarray(np.cos(ang), dtype=BF16), jnp.asarray(np.sin(ang), dtype=BF16)


def _fourier(st, x, mods, layer, g1):
    seq = st.seq
    cs, ss = _dft_tables(seq)
    cc, sc = _dft_tables(FOURIER_GW)
    mod_spec = pl.BlockSpec((None, None, 1, N_MOD * D), lambda b: (layer, 0 if st.shared else 1 + b, 0, 0))
    return pl.pallas_call(
        functools.partial(_fourier_kernel, scale=1.0 / math.sqrt(seq * FOURIER_GW)),
        grid=(st.batch,),
        in_specs=[pl.BlockSpec((seq, D), lambda b: (b, 0)), mod_spec, _const_spec((1, D)),
                  _const_spec((seq, seq)), _const_spec((seq, seq)),
                  _const_spec((FOURIER_GW, FOURIER_GW)), _const_spec((FOURIER_GW, FOURIER_GW))],
        out_specs=pl.BlockSpec((seq, D), lambda b: (b, 0)),
        out_shape=jax.ShapeDtypeStruct((st.n_tok, D), BF16),
        compiler_params=_cparams(("arbitrary",), VMEM_LIMIT_BYTES),
        name="fourier",
    )(x, mods, g1, cs, ss, cc, sc)


def kernel(x_prompt, x_sample, state_rglru, cache_k, cache_v, c, c_ctx, mod_w, mod_b, norm1_g, norm2_g,
           rg_w_in, rg_conv_w, rg_conv_b, rg_w_a, rg_b_a, rg_w_x, rg_b_x, rg_lambda, rg_w_out,
           at_w_qkv, at_q_norm, at_k_norm, at_sink, at_w_o, ft_w, moe_router, moe_w_gate, moe_w_up, moe_w_down):
    depth = mod_w.shape[0]
    batch, seq, _ = x_prompt.shape
    dec_batch, dec_seq, _ = x_sample.shape
    assert 1 + dec_batch <= MOD_ROWS
    streams = (_Stream(batch, seq, True), _Stream(dec_batch, dec_seq, False))
    cond = jnp.concatenate([c_ctx[None, :], c, jnp.zeros((MOD_ROWS - 1 - dec_batch, D), F32)], axis=0)
    mods = _modulation(cond, mod_w, mod_b).reshape(depth, MOD_ROWS, 1, N_MOD * D)

    xs = [x_prompt.reshape(batch * seq, D), x_sample.reshape(dec_batch * dec_seq, D)]
    new_rg, new_k, new_v = [], [], []
    n_mixers = 3
    for layer in range(depth):
        kind, j = layer % n_mixers, layer // n_mixers
        g1 = norm1_g[layer].reshape(1, D)
        g2 = norm2_g[layer].reshape(1, D)
        wr = _router_pieces(moe_router[layer])
        for si, st in enumerate(streams):
            x = xs[si]
            if kind == 0:
                gate, u = _rg_in(st, x, mods, layer, g1, rg_w_in[j].astype(BF16))
                if st.shared:
                    h0 = jnp.zeros((2, st.batch, D), F32)
                else:
                    h0 = jnp.transpose(state_rglru[:, j], (1, 0, 2))
                h, fin = _rg_scan(st, u, h0, rg_conv_w[j], rg_conv_b[j], rg_w_a[j], rg_b_a[j],
                                  rg_w_x[j], rg_b_x[j], rg_lambda[j])
                if st.shared:
                    new_rg.append(jnp.transpose(fin, (1, 0, 2)))
                x1, xnb, aff =_lin_out(st, (h, gate), rg_w_out[j].astype(BF16), x, mods, layer, g2, wr, True)
            elif kind == 1:
                w_qkv = at_w_qkv[j].astype(BF16)
                if st.shared:
                    q, k, v, kc, vc = _qkv(st, x, mods, layer, g1, w_qkv, at_q_norm[j], at_k_norm[j], False)
                    new_k.append(kc.reshape(st.batch, st.seq, N_KV, HEAD_DIM))
                    new_v.append(vc.reshape(st.batch, st.seq, N_KV, HEAD_DIM))
                    a = _ctx_attn(st, q, k, v, at_sink[j])
                else:
                    q, k, v = _qkv(st, x, mods, layer, g1, w_qkv, at_q_norm[j], at_k_norm[j], True)
                    nk = N_KV * HEAD_DIM
                    a = _lat_attn(st, q, k, v, cache_k[:, j].reshape(-1, nk), cache_v[:, j].reshape(-1, nk),
                                  at_sink[j])
                x1, xnb, aff =_lin_out(st, (a,), at_w_o[j].astype(BF16), x, mods, layer, g2, wr, False)
            else:
                a = _fourier(st, x, mods, layer, g1)
                x1, xnb, aff =_lin_out(st, (a,), ft_w[j].astype(BF16), x, mods, layer, g2, wr, False)
            xs[si] = _moe(st, x1, xnb, aff, mods, layer, moe_w_gate, moe_w_up, moe_w_down)
    return (xs[0].reshape(batch, seq, D), xs[1].reshape(dec_batch, dec_seq, D),
            jnp.stack(new_rg, axis=1), jnp.stack(new_k, axis=1), jnp.stack(new_v, axis=1))
```

```python
import functools
import math

import numpy as np
import jax
import jax.numpy as jnp
from jax import lax
from jax.experimental import pallas as pl
from jax.experimental.pallas import tpu as pltpu

F32 = jnp.float32
BF16 = jnp.bfloat16
I32 = jnp.int32
U32 = jnp.uint32

D = 1024
N_MOD = 6
EPS = 1e-6
GRID_W = 64
CONV_W = 4
CONV_LEFT = 2
LRU_C = 8.0
RNN_BLOCKS = 16
RNN_BLOCK = D // RNN_BLOCKS
N_HEADS = 16
N_KV = 4
HEAD_DIM = 64
GQA_G = N_HEADS // N_KV
WINDOW = 128
ROPE_BASE = 10000.0
FOURIER_GROUPS = 4
FOURIER_GW = D // FOURIER_GROUPS
N_EXPERTS = 16
EC_FACTOR = 2

LANES = 128
SUBLANES = 8
VMEM_LIMIT_BYTES = 56 * 1024 * 1024

TOK_TILE = 256
MOD_ROWS = 16
SCAN_CC = 128
SCAN_TC = 128
FF_CHUNK = 512


def _cparams(sem, vmem=None):
    return pltpu.CompilerParams(dimension_semantics=sem, vmem_limit_bytes=vmem)


def _split2(a):
    hi = a.astype(BF16)
    lo = (a - hi.astype(F32)).astype(BF16)
    return hi, lo


def _dot(a, b):
    return jnp.dot(a, b, preferred_element_type=F32)


def _dot3(a, b):
    a_hi, a_lo = _split2(a)
    b_hi, b_lo = _split2(b)
    return _dot(a_hi, b_hi) + _dot(a_hi, b_lo) + _dot(a_lo, b_hi)


def _norm_mod(x, g, shift, scale):
    ms = jnp.mean(x * x, axis=-1, keepdims=True)
    y = x * lax.rsqrt(ms + EPS) * g
    return y * (1.0 + scale) + shift


def _mod_slice(m, k):
    return m[:, k * D:(k + 1) * D]


class _Stream:
    def __init__(self, batch, seq, shared_cond):
        self.batch, self.seq, self.shared = batch, seq, shared_cond
        self.n_tok = batch * seq
        self.tiles = self.n_tok // TOK_TILE
        self.tiles_per_seq = seq // TOK_TILE

    def mod_row(self, i):
        return 0 if self.shared else 1 + i // self.tiles_per_seq

    def mod_spec(self, layer):
        return pl.BlockSpec((None, None, 1, N_MOD * D), lambda i: (layer, self.mod_row(i), 0, 0))

    def seq_major_spec(self):
        tps = self.tiles_per_seq
        return pl.BlockSpec((TOK_TILE, D), lambda i: (i % tps, i // tps))


def _tok_spec(width=D):
    return pl.BlockSpec((TOK_TILE, width), lambda i: (i, 0))


def _const_spec(shape):
    nd = len(shape)
    return pl.BlockSpec(shape, lambda i: (0,) * nd)


def _mod_kernel(c_ref, w_ref, b_ref, o_ref):
    c = c_ref[...]
    c = c * jax.nn.sigmoid(c)
    o_ref[...] = _dot3(c, w_ref[...]) + b_ref[...]


def _modulation(cond, mod_w, mod_b):
    depth = mod_w.shape[0]
    tn = N_MOD * D // 4
    return pl.pallas_call(
        _mod_kernel,
        grid=(depth, N_MOD * D // tn),
        in_specs=[pl.BlockSpec((MOD_ROWS, D), lambda l, n: (0, 0)),
                  pl.BlockSpec((None, D, tn), lambda l, n: (l, 0, n)),
                  pl.BlockSpec((None, 1, tn), lambda l, n: (l, 0, n))],
        out_specs=pl.BlockSpec((None, MOD_ROWS, tn), lambda l, n: (l, 0, n)),
        out_shape=jax.ShapeDtypeStruct((depth, MOD_ROWS, N_MOD * D), F32),
        compiler_params=_cparams(("arbitrary", "arbitrary"), VMEM_LIMIT_BYTES),
        name="modulation",
    )(cond, mod_w, mod_b.reshape(depth, 1, N_MOD * D))


def _rg_in_kernel(x_ref, mod_ref, g_ref, w_ref, gate_ref, u_ref):
    m = mod_ref[...]
    h = _norm_mod(x_ref[...], g_ref[...], _mod_slice(m, 0), _mod_slice(m, 1))
    gu = _dot(h.astype(BF16), w_ref[...])
    gate_ref[...] = gu[:, :D].astype(BF16)
    u_ref[...] = gu[:, D:]


def _rg_in(st, x, mods, layer, g1, w_in):
    shape = (st.seq, st.batch * D)
    return pl.pallas_call(
        _rg_in_kernel,
        grid=(st.tiles,),
        in_specs=[_tok_spec(), st.mod_spec(layer), _const_spec((1, D)), _const_spec((D, 2 * D))],
        out_specs=[st.seq_major_spec(), st.seq_major_spec()],
        out_shape=[jax.ShapeDtypeStruct(shape, BF16), jax.ShapeDtypeStruct(shape, F32)],
        compiler_params=_cparams(("arbitrary",), VMEM_LIMIT_BYTES),
        name="rg_in",
    )(x, mods, g1, w_in)


def _softplus(z):
    return jnp.maximum(z, 0.0) + jnp.log1p(jnp.exp(-jnp.abs(z)))


def _rg_scan_kernel(u_ref, h0_ref, cw_ref, cb_ref, wg_ref, bg_ref, lam_ref, h_ref, fin_ref,
                    upad, a_f, b_f, a_b, b_b, *, seq):
    cc = u_ref.shape[-1]
    pad_hi = CONV_W - 1 - CONV_LEFT
    upad[0:CONV_LEFT] = jnp.zeros((CONV_LEFT, SUBLANES, cc), F32)
    upad[CONV_LEFT:CONV_LEFT + seq] = u_ref[...]
    upad[CONV_LEFT + seq:CONV_LEFT + seq + pad_hi] = jnp.zeros((pad_hi, SUBLANES, cc), F32)
    k2 = (-0.5 * LRU_C * math.log2(math.e)) * _softplus(-lam_ref[...])
    cw = cw_ref[...]
    rows = SCAN_TC * SUBLANES

    def coef(c, carry):
        t0 = pl.multiple_of(c * SCAN_TC, SCAN_TC)
        uc = cb_ref[...] + cw[0:1] * upad[pl.ds(t0, SCAN_TC)]
        for k in range(1, CONV_W):
            uc = uc + cw[k:k + 1] * upad[pl.ds(t0 + k, SCAN_TC)]
        u2 = uc.reshape(rows, cc)
        gth = jnp.tanh(_dot(u2.astype(BF16), wg_ref[...]) + bg_ref[...])
        hu2 = 0.5 * u2
        for d, (a_s, b_s) in enumerate(((a_f, b_f), (a_b, b_b))):
            r_th = gth[:, (2 * d) * cc:(2 * d + 1) * cc]
            i_th = gth[:, (2 * d + 1) * cc:(2 * d + 2) * cc]
            a = jnp.exp2(k2[d] * r_th + k2[d])
            q = 1.0 - a * a
            root = jnp.where(q > 0.0, q * lax.rsqrt(q), 0.0)
            a_s[pl.ds(t0, SCAN_TC)] = a.reshape(SCAN_TC, SUBLANES, cc)
            b_s[pl.ds(t0, SCAN_TC)] = (root * (i_th * hu2 + hu2)).reshape(SCAN_TC, SUBLANES, cc)
        return carry

    lax.fori_loop(0, seq // SCAN_TC, coef, 0)

    blk = SUBLANES

    def step(i, carry):
        hf, hb = carry
        t0 = pl.multiple_of(i * blk, blk)
        av, bv = a_f[pl.ds(t0, blk)], b_f[pl.ds(t0, blk)]
        outs = []
        for k in range(0, blk, 2):
            a0, b0, a1, b1 = av[k], bv[k], av[k + 1], bv[k + 1]
            outs.append(a0 * hf + b0)
            hf = (a1 * a0) * hf + (a1 * b0 + b1)
            outs.append(hf)
        b_f[pl.ds(t0, blk)] = jnp.stack(outs, axis=0)
        s0 = seq - blk - t0
        cv, dv = a_b[pl.ds(s0, blk)], b_b[pl.ds(s0, blk)]
        outs = [None] * blk
        for k in range(blk - 1, 0, -2):
            c0, d0, c1, d1 = cv[k], dv[k], cv[k - 1], dv[k - 1]
            outs[k] = c0 * hb + d0
            hb = (c1 * c0) * hb + (c1 * d0 + d1)
            outs[k - 1] = hb
        b_b[pl.ds(s0, blk)] = jnp.stack(outs, axis=0)
        return hf, hb

    hf, hb = lax.fori_loop(0, seq // blk, step, (h0_ref[0], h0_ref[1]))
    fin_ref[0] = hf
    fin_ref[1] = hb
    h_ref[...] = b_f[...] + b_b[...]


def _blockdiag_pairs(w):
    per = SCAN_CC // RNN_BLOCK
    w4 = w.reshape(D // SCAN_CC, per, RNN_BLOCK, RNN_BLOCK)
    eye = jnp.eye(per, dtype=w.dtype)
    return jnp.einsum('cipq,ij->cipjq', w4, eye).reshape(D // SCAN_CC, SCAN_CC, SCAN_CC)


def _rg_scan(st, u, h0, conv_w, conv_b, w_a, b_a, w_x, b_x, lam):
    seq, batch = st.seq, st.batch
    n_cc = D // SCAN_CC
    wg = (0.5 * jnp.concatenate([_blockdiag_pairs(w_a[0]), _blockdiag_pairs(w_x[0]),
                                 _blockdiag_pairs(w_a[1]), _blockdiag_pairs(w_x[1])], axis=-1)).astype(BF16)
    bg = 0.5 * jnp.concatenate([b.reshape(n_cc, 1, SCAN_CC) for b in (b_a[0], b_x[0], b_a[1], b_x[1])], axis=-1)
    blk = (seq, SUBLANES, SCAN_CC)
    scr = pltpu.VMEM(blk, F32)
    h, fin = pl.pallas_call(
        functools.partial(_rg_scan_kernel, seq=seq),
        grid=(batch // SUBLANES, n_cc),
        in_specs=[pl.BlockSpec(blk, lambda b, c: (0, b, c)),
                  pl.BlockSpec((2, SUBLANES, SCAN_CC), lambda b, c: (0, b, c)),
                  pl.BlockSpec((CONV_W, 1, SCAN_CC), lambda b, c: (0, 0, c)),
                  pl.BlockSpec((1, 1, SCAN_CC), lambda b, c: (0, 0, c)),
                  pl.BlockSpec((None, SCAN_CC, 4 * SCAN_CC), lambda b, c: (c, 0, 0)),
                  pl.BlockSpec((None, 1, 4 * SCAN_CC), lambda b, c: (c, 0, 0)),
                  pl.BlockSpec((2, 1, SCAN_CC), lambda b, c: (0, 0, c))],
        out_specs=[pl.BlockSpec(blk, lambda b, c: (0, b, c)),
                   pl.BlockSpec((2, SUBLANES, SCAN_CC), lambda b, c: (0, b, c))],
        out_shape=[jax.ShapeDtypeStruct((seq, batch, D), F32),
                   jax.ShapeDtypeStruct((2, batch, D), F32)],
        scratch_shapes=[pltpu.VMEM((seq + CONV_W - 1, SUBLANES, SCAN_CC), F32), scr, scr, scr, scr],
        compiler_params=_cparams(("arbitrary", "arbitrary"), VMEM_LIMIT_BYTES),
        name="rg_scan",
    )(u.reshape(seq, batch, D), h0, conv_w.reshape(CONV_W, 1, D), conv_b.reshape(1, 1, D),
      wg, bg, lam.reshape(2, 1, D))
    return h.reshape(seq, batch * D), fin


LIN_CHUNK = LANES
LIN_CHUNKS = TOK_TILE // LIN_CHUNK


def _chunk_rows(k):
    return slice(k * LIN_CHUNK, (k + 1) * LIN_CHUNK)


def _route_and_pack(ys, m, x_ref, g2_ref, wr_ref, xo_ref, xnb_ref, aff_ref, slab_s):
    n_ch = D // LANES
    ks = range(LIN_CHUNKS)
    xs = [x_ref[_chunk_rows(k), :] + _mod_slice(m, 2) * ys[k] for k in ks]
    for k in ks:
        xo_ref[_chunk_rows(k), :] = xs[k]
    xns = [_norm_mod(x, g2_ref[...], _mod_slice(m, 3), _mod_slice(m, 4)) for x in xs]
    his = [xn.astype(BF16) for xn in xns]
    los = [(xn - hi.astype(F32)).astype(BF16) for xn, hi in zip(xns, his)]
    wr = wr_ref[...]
    l1s = [_qk(wr, hi) for hi in his]
    l2s = [_qk(wr, lo) for lo in los]
    for k in ks:
        for j in range(n_ch):
            slab_s[pl.ds(k * LIN_CHUNK * n_ch + j, LIN_CHUNK, stride=n_ch), :] = xns[k][:, j * LANES:(j + 1) * LANES]
    xnb_ref[...] = slab_s[...].astype(BF16).reshape(xnb_ref.shape)
    e = N_EXPERTS
    for k in ks:
        l1, l2 = l1s[k], l2s[k]
        logit = l1[0:e] + l1[e:2 * e] + l1[2 * e:3 * e] + l2[0:e] + l2[e:2 * e]
        mx = jnp.max(logit, axis=0, keepdims=True)
        ex = jnp.exp(logit - mx)
        aff_ref[k] = ex / jnp.sum(ex, axis=0, keepdims=True)


def _lin_out_kernel(a_ref, w_ref, x_ref, mod_ref, g2_ref, wr_ref, xo_ref, xnb_ref, aff_ref, slab_s):
    ys = [_dot(a_ref[_chunk_rows(k), :], w_ref[...]) for k in range(LIN_CHUNKS)]
    _route_and_pack(ys, mod_ref[...], x_ref, g2_ref, wr_ref, xo_ref, xnb_ref, aff_ref, slab_s)


def _lin_out_gated_kernel(h_ref, gate_ref, w_ref, x_ref, mod_ref, g2_ref, wr_ref, xo_ref, xnb_ref, aff_ref,
                          slab_s):
    acts = [(h_ref[_chunk_rows(k), :] * jax.nn.gelu(gate_ref[_chunk_rows(k), :].astype(F32))).astype(BF16)
            for k in range(LIN_CHUNKS)]
    ys = [_dot(a, w_ref[...]) for a in acts]
    _route_and_pack(ys, mod_ref[...], x_ref, g2_ref, wr_ref, xo_ref, xnb_ref, aff_ref, slab_s)


def _router_pieces(w_router):
    hi = w_router.astype(BF16)
    r1 = w_router - hi.astype(F32)
    mid = r1.astype(BF16)
    lo = (r1 - mid.astype(F32)).astype(BF16)
    return jnp.concatenate([hi, mid, lo], axis=1).T


def _lin_out(st, srcs, w, x, mods, layer, g2, wr, gated):
    n_chunks = st.n_tok // LANES
    if gated:
        body, src_specs = _lin_out_gated_kernel, [st.seq_major_spec(), st.seq_major_spec()]
    else:
        body, src_specs = _lin_out_kernel, [_tok_spec()]
    return pl.pallas_call(
        body,
        grid=(st.tiles,),
        in_specs=src_specs + [_const_spec((D, D)), _tok_spec(), st.mod_spec(layer), _const_spec((1, D)),
                              _const_spec((3 * N_EXPERTS, D))],
        out_specs=[_tok_spec(), pl.BlockSpec((TOK_TILE // 2, 2 * (D // LANES), LANES), lambda i: (i, 0, 0)),
                   pl.BlockSpec((TOK_TILE // LANES, N_EXPERTS, LANES), lambda i: (i, 0, 0))],
        out_shape=[jax.ShapeDtypeStruct((st.n_tok, D), F32),
                   jax.ShapeDtypeStruct((st.n_tok // 2, 2 * (D // LANES), LANES), BF16),
                   jax.ShapeDtypeStruct((n_chunks, N_EXPERTS, LANES), F32)],
        scratch_shapes=[pltpu.VMEM((TOK_TILE * (D // LANES), LANES), F32)],
        compiler_params=_cparams(("arbitrary",), VMEM_LIMIT_BYTES),
        name="lin_out_gated" if gated else "lin_out",
    )(*srcs, w, x, mods, g2, wr)


def _select_kernel(a_ref, idx_ref, g_ref, linc_s, cnt_s, crow_s, *, cap):
    n_e, n_ch, _ = a_ref.shape
    assert n_ch & (n_ch - 1) == 0 and n_ch <= LANES
    rows = n_e * n_ch
    a3 = a_ref[...]
    capf = jnp.float32(cap)

    def count(mask3):
        c = jnp.sum(mask3.astype(F32), axis=1, keepdims=True)
        return jnp.sum(c, axis=2, keepdims=True)

    def as_f32(bits):
        return pltpu.bitcast(bits, F32)

    def search(i, thr):
        cand = thr | (jnp.int32(1) << (30 - i))
        return jnp.where(count(a3 >= as_f32(cand)) >= capf, cand, thr)

    thr = lax.fori_loop(0, 31, search, jnp.zeros((n_e, 1, LANES), I32))
    gt3 = a3 >= as_f32(thr + 1)
    eq3 = jnp.logical_and(a3 >= as_f32(thr), jnp.logical_not(gt3))
    need = capf - count(gt3)

    li = lax.broadcasted_iota(I32, (LANES, LANES), 0)
    lj = lax.broadcasted_iota(I32, (LANES, LANES), 1)
    upper = (li <= lj).astype(BF16)
    ones = jnp.ones((LANES, LANES), BF16)
    ri = lax.broadcasted_iota(I32, (rows, rows), 0)
    rj = lax.broadcasted_iota(I32, (rows, rows), 1)
    sh = n_ch.bit_length() - 1
    before = (((ri >> sh) == (rj >> sh)) & (rj < ri)).astype(BF16)

    def prefixes(mask3):
        x = mask3.reshape(rows, LANES).astype(BF16)
        local = _dot(x, upper)
        tot = _dot(x, ones)
        return local, tot, _dot(before, tot.astype(BF16))

    gt_l, gt_t, gt_x = prefixes(gt3)
    eq_l, eq_t, eq_x = prefixes(eq3)
    need_r = jnp.broadcast_to(need, (n_e, n_ch, LANES)).reshape(rows, LANES)
    sel_incl = gt_x + gt_l + jnp.minimum(eq_x + eq_l, need_r)
    sel_x = gt_x + jnp.minimum(eq_x, need_r)
    sel_c = gt_x + gt_t + jnp.minimum(eq_x + eq_t, need_r)
    linc_s[...] = sel_incl - sel_x
    cnt_s[...] = sel_c - sel_x
    c3 = sel_c.reshape(n_e, n_ch, LANES)
    pick = lax.broadcasted_iota(I32, (1, n_ch, LANES), 1) == lax.broadcasted_iota(I32, (1, n_ch, LANES), 2)
    crow = jnp.sum(jnp.where(pick, c3, 0.0), axis=1)
    crow = jnp.where(lax.broadcasted_iota(I32, crow.shape, 1) < n_ch, crow, jnp.float32(2 * cap + n_ch * LANES))
    crow_s[...] = crow

    slot = lax.broadcasted_iota(I32, (cap, LANES), 0).astype(F32)
    lane = lax.broadcasted_iota(I32, (cap, LANES), 1).astype(F32)
    diag = lax.broadcasted_iota(I32, (LANES, LANES), 0) == lax.broadcasted_iota(I32, (LANES, LANES), 1)
    zpad_w = jnp.zeros((LANES - n_ch, 4 * LANES), BF16)
    group = 2
    assert n_e % group == 0

    def operands(e):
        r0 = pl.multiple_of(e * n_ch, n_ch)
        cnt_e = cnt_s[pl.ds(r0, n_ch), :].astype(BF16)
        linc_e = linc_s[pl.ds(r0, n_ch), :].astype(BF16)
        a_e = a_ref[e]
        a_hi = a_e.astype(BF16)
        r1 = a_e - a_hi.astype(F32)
        a_mid = r1.astype(BF16)
        a_lo = (r1 - a_mid.astype(F32)).astype(BF16)
        by_passed = jnp.concatenate([jnp.concatenate([ones[:n_ch], cnt_e], axis=1), zpad_w[:, :2 * LANES]], axis=0)
        by_chunk = jnp.concatenate([jnp.concatenate([linc_e, a_hi, a_mid, a_lo], axis=1), zpad_w], axis=0)
        return by_passed, by_chunk

    def to_row(col):
        pieces = []
        for b in range(cap // LANES):
            blk = col[b * LANES:(b + 1) * LANES, :]
            pieces.append(jnp.sum(jnp.where(diag, blk, 0.0), axis=0, keepdims=True))
        return jnp.concatenate(pieces, axis=1)

    def per_group(i, carry):
        es = [i * group + k for k in range(group)]
        ops = [operands(e) for e in es]
        passed = [(crow_s[pl.ds(e, 1), :] <= slot).astype(BF16) for e in es]
        res1 = [_dot(p, o[0]) for p, o in zip(passed, ops)]
        chunks = [r[:, :LANES] for r in res1]
        ranks = [slot - r[:, LANES:] for r in res1]
        res2 = [_dot((c == lane).astype(BF16), o[1]) for c, o in zip(chunks, ops)]
        withins = [_dot((r2[:, :LANES] <= rk).astype(BF16), ones) for r2, rk in zip(res2, ranks)]
        for e, c, w, r2 in zip(es, chunks, withins, res2):
            tok = c * jnp.float32(LANES) + w
            arow = r2[:, LANES:2 * LANES] + r2[:, 2 * LANES:3 * LANES] + r2[:, 3 * LANES:]
            gate = jnp.sum(jnp.where(lane == w, arow, 0.0), axis=1, keepdims=True)
            idx_ref[pl.ds(e, 1), :] = to_row(tok).astype(I32)
            g_ref[pl.ds(e, 1), :] = to_row(jnp.broadcast_to(gate, (cap, LANES)))
        return carry

    lax.fori_loop(0, n_e // group, per_group, 0)


def _select(aff_chunks, cap):
    n_ch = aff_chunks.shape[0]
    a = jnp.transpose(aff_chunks, (1, 0, 2))
    rows = N_EXPERTS * n_ch
    return pl.pallas_call(
        functools.partial(_select_kernel, cap=cap),
        out_shape=[jax.ShapeDtypeStruct((N_EXPERTS, cap), I32),
                   jax.ShapeDtypeStruct((N_EXPERTS, cap), F32)],
        scratch_shapes=[pltpu.VMEM((rows, LANES), F32), pltpu.VMEM((rows, LANES), F32),
                        pltpu.VMEM((N_EXPERTS, LANES), F32)],
        compiler_params=pltpu.CompilerParams(vmem_limit_bytes=VMEM_LIMIT_BYTES),
        name="select",
    )(a)


ROW_CH = D // LANES
GATHER_UNROLL = 16
FFN_ROWS = 512


def _cm_stride(m):
    return m + SUBLANES


def _moe_ffn_kernel(idx_ref, src_ref, g_ref, wg_ref, wu_ref, wd_ref, o_ref, tile_s, xe_s, *, cap):
    e = pl.program_id(0)
    f = pl.program_id(1)
    stride = _cm_stride(cap)
    tm = min(FFN_ROWS, cap)

    def gather_token(p, n):
        pair = src_ref[n >> 1].astype(F32)
        odd = (jnp.full((ROW_CH, LANES), n, I32) & 1) == 1
        tile_s[pl.ds(p, ROW_CH, stride=stride), :] = jnp.where(odd, pair[ROW_CH:], pair[:ROW_CH])

    @pl.when(jnp.logical_and(e == 0, f == 0))
    def _first_gather():
        def body(gi, carry):
            base = pl.multiple_of(gi * GATHER_UNROLL, GATHER_UNROLL)
            for k in range(GATHER_UNROLL):
                gather_token(base + k, idx_ref[base + k])
            return carry

        lax.fori_loop(0, cap // GATHER_UNROLL, body, 0)

    def ffn(xt):
        hg = _dot(xt, wg_ref[...].astype(BF16))
        hu = _dot(xt, wu_ref[...].astype(BF16))
        h = (hg * jax.nn.sigmoid(hg) * hu).astype(BF16)
        return _dot(h, wd_ref[...].astype(BF16))

    @pl.when(f == 0)
    def _half0():
        for t in range(cap // tm):
            r0 = t * tm
            xt = jnp.concatenate([tile_s[pl.ds(j * stride + r0, tm), :].astype(BF16) for j in range(ROW_CH)],
                                 axis=1)
            xe_s[r0:r0 + tm, :] = xt
            y = ffn(xt)
            for j in range(ROW_CH):
                o_ref[pl.ds(j * stride + r0, tm), :] = y[:, j * LANES:(j + 1) * LANES]

    @pl.when(f == 1)
    def _half1():
        nxt = (e + 1) * cap
        for t in range(cap // tm):
            r0 = t * tm
            def gather_share(q):
                for p in range(r0 + q * tm // 4, r0 + (q + 1) * tm // 4):
                    gather_token(p, idx_ref[nxt + p])

            xt = xe_s[r0:r0 + tm, :]
            gather_share(0)
            hg = _dot(xt, wg_ref[...].astype(BF16))
            gather_share(1)
            hu = _dot(xt, wu_ref[...].astype(BF16))
            h = (hg * jax.nn.sigmoid(hg) * hu).astype(BF16)
            gather_share(2)
            y = _dot(h, wd_ref[...].astype(BF16))
            gather_share(3)
            gt = g_ref[r0:r0 + tm, :]
            for j in range(ROW_CH):
                rows = pl.ds(j * stride + r0, tm)
                o_ref[rows, :] = (o_ref[rows, :] + y[:, j * LANES:(j + 1) * LANES]) * gt
        for j in range(ROW_CH):
            o_ref[pl.ds(j * stride + cap, stride - cap), :] = jnp.zeros((stride - cap, LANES), F32)


def _moe_ffn(idx, gates, xnb, w_gate, w_up, w_down, layer, cap):
    n_tok = 2 * xnb.shape[0]
    ff = w_gate.shape[-1]
    assert ff == 2 * FF_CHUNK
    stride = _cm_stride(cap)
    src = xnb
    idx = jnp.concatenate([idx, jnp.zeros((cap,), I32)])
    grid_spec = pltpu.PrefetchScalarGridSpec(
        num_scalar_prefetch=1,
        grid=(N_EXPERTS, ff // FF_CHUNK),
        in_specs=[pl.BlockSpec((n_tok // 2, 2 * ROW_CH, LANES), lambda e, f, idx: (0, 0, 0),
                               pipeline_mode=pl.Buffered(1)),
                  pl.BlockSpec((None, cap, 1), lambda e, f, idx: (e, 0, 0)),
                  pl.BlockSpec((None, None, D, FF_CHUNK), lambda e, f, idx: (layer, e, 0, f)),
                  pl.BlockSpec((None, None, D, FF_CHUNK), lambda e, f, idx: (layer, e, 0, f)),
                  pl.BlockSpec((None, None, FF_CHUNK, D), lambda e, f, idx: (layer, e, f, 0))],
        out_specs=pl.BlockSpec((None, ROW_CH * stride, LANES), lambda e, f, idx: (e, 0, 0)),
        scratch_shapes=[pltpu.VMEM((ROW_CH * stride, LANES), F32), pltpu.VMEM((cap, D), BF16)],
    )
    return pl.pallas_call(
        functools.partial(_moe_ffn_kernel, cap=cap),
        grid_spec=grid_spec,
        out_shape=jax.ShapeDtypeStruct((N_EXPERTS, ROW_CH * stride, LANES), F32),
        compiler_params=_cparams(("arbitrary", "arbitrary"), VMEM_LIMIT_BYTES),
        name="moe_ffn",
    )(idx, src, gates.reshape(N_EXPERTS, cap, 1), w_gate, w_up, w_down)


COMBINE_VMEM_LIMIT_BYTES = 60 * 1024 * 1024


def _combine_kernel(idx_ref, ye_ref, x_ref, mod_ref, o_ref, acc_s, *, cap):
    s = pl.program_id(0)
    stride = _cm_stride(cap)

    @pl.when(s == 0)
    def _zero():
        acc_s[...] = jnp.zeros(acc_s.shape, F32)

    @pl.when(s < N_EXPERTS)
    def _scatter():
        def body(gi, carry):
            base = pl.multiple_of(gi * GATHER_UNROLL, GATHER_UNROLL)
            rows, sums = [], []
            for k in range(GATHER_UNROLL):
                r = pl.multiple_of(idx_ref[s * cap + base + k] * ROW_CH, ROW_CH)
                rows.append(r)
                sums.append(acc_s[pl.ds(r, ROW_CH), :] + ye_ref[pl.ds(base + k, ROW_CH, stride=stride), :])
            for r, v in zip(rows, sums):
                acc_s[pl.ds(r, ROW_CH), :] = v
            return carry

        lax.fori_loop(0, cap // GATHER_UNROLL, body, 0)

    @pl.when(s >= N_EXPERTS)
    def _residual():
        r0 = pl.multiple_of((s - N_EXPERTS) * (TOK_TILE * ROW_CH), TOK_TILE * ROW_CH)
        y = jnp.concatenate([acc_s[pl.ds(r0 + j, TOK_TILE, stride=ROW_CH), :] for j in range(ROW_CH)], axis=1)
        o_ref[...] = x_ref[...] + _mod_slice(mod_ref[...], 5) * y


def _combine(st, idx, ye, x, mods, layer, cap):
    stride = _cm_stride(cap)
    n_e = N_EXPERTS

    def tile(s):
        return jnp.maximum(s - n_e, 0)

    grid_spec = pltpu.PrefetchScalarGridSpec(
        num_scalar_prefetch=1,
        grid=(n_e + st.tiles,),
        in_specs=[pl.BlockSpec((None, ROW_CH * stride, LANES), lambda s, idx: (jnp.minimum(s, n_e - 1), 0, 0)),
                  pl.BlockSpec((TOK_TILE, D), lambda s, idx: (tile(s), 0)),
                  pl.BlockSpec((None, None, 1, N_MOD * D), lambda s, idx: (layer, st.mod_row(tile(s)), 0, 0))],
        out_specs=pl.BlockSpec((TOK_TILE, D), lambda s, idx: (tile(s), 0)),
        scratch_shapes=[pltpu.VMEM((st.n_tok * ROW_CH, LANES), F32)],
    )
    return pl.pallas_call(
        functools.partial(_combine_kernel, cap=cap),
        grid_spec=grid_spec,
        out_shape=jax.ShapeDtypeStruct((st.n_tok, D), F32),
        compiler_params=_cparams(("arbitrary",), COMBINE_VMEM_LIMIT_BYTES),
        name="moe_combine",
    )(idx, ye, x, mods)


def _moe(st, x, xnb, aff_chunks, mods, layer, w_gate, w_up, w_down):
    cap = EC_FACTOR * st.n_tok // N_EXPERTS
    idx, gates = _select(aff_chunks, cap)
    idx = idx.reshape(N_EXPERTS * cap)
    ye = _moe_ffn(idx, gates, xnb, w_gate, w_up, w_down, layer, cap)
    return _combine(st, idx, ye, x, mods, layer, cap)


def _head_sumsq(x, bd):
    x2 = x * x
    hi, lo = _split2(x2)
    w = bd.shape[0]
    cols = []
    for c in range(x.shape[1] // w):
        sl = slice(c * w, (c + 1) * w)
        cols.append(_dot(hi[:, sl], bd) + _dot(lo[:, sl], bd))
    return cols[0] if len(cols) == 1 else jnp.concatenate(cols, axis=1)


def _qk_norm(x, gain, bd):
    ms = _head_sumsq(x, bd) * (1.0 / HEAD_DIM)
    return x * lax.rsqrt(ms + EPS) * gain


def _rope(x, cos, sin_dn, sin_up):
    n = x.shape[1]
    q = HEAD_DIM // 4
    return x * cos + pltpu.roll(x, n - q, 1) * sin_dn + pltpu.roll(x, q, 1) * sin_up


def _qkv_ctx_kernel(x_ref, mod_ref, g_ref, w_ref, qg_ref, kg_ref, bd_ref, q_ref, k_ref, v_ref, kc_ref, vc_ref):
    m = mod_ref[...]
    h = _norm_mod(x_ref[...], g_ref[...], _mod_slice(m, 0), _mod_slice(m, 1))
    qkv = _dot(h.astype(BF16), w_ref[...])
    nq, nk = N_HEADS * HEAD_DIM, N_KV * HEAD_DIM
    bd = bd_ref[...]
    q = _qk_norm(qkv[:, :nq], qg_ref[...], bd)
    k = _qk_norm(qkv[:, nq:nq + nk], kg_ref[...], bd)
    v = qkv[:, nq + nk:]
    q_ref[...] = (q * (HEAD_DIM ** -0.5)).astype(BF16)
    k_ref[...] = k.astype(BF16)
    v_ref[...] = v.astype(BF16)
    kc_ref[...] = k
    vc_ref[...] = v


def _qkv_lat_kernel(x_ref, mod_ref, g_ref, w_ref, qg_ref, kg_ref, bd_ref,
                    cos_ref, sdn_ref, sup_ref, q_ref, k_ref, v_ref):
    m = mod_ref[...]
    h = _norm_mod(x_ref[...], g_ref[...], _mod_slice(m, 0), _mod_slice(m, 1))
    qkv = _dot(h.astype(BF16), w_ref[...])
    nq, nk = N_HEADS * HEAD_DIM, N_KV * HEAD_DIM
    bd = bd_ref[...]
    q = _qk_norm(qkv[:, :nq], qg_ref[...], bd)
    k = _qk_norm(qkv[:, nq:nq + nk], kg_ref[...], bd)
    tabs = (cos_ref[...], sdn_ref[...], sup_ref[...])
    q = _rope(q, *(_tile_lanes(t, nq) for t in tabs))
    k = _rope(k, *(_tile_lanes(t, nk) for t in tabs))
    q_ref[...] = (q * (HEAD_DIM ** -0.5)).astype(BF16)
    k_ref[...] = k.astype(BF16)
    v_ref[...] = qkv[:, nq + nk:].astype(BF16)


def _rope_tables(seq, n_heads):
    n_rows = seq // GRID_W
    row = jnp.repeat(jnp.arange(n_rows), GRID_W).astype(F32)
    col = jnp.tile(jnp.arange(GRID_W), n_rows).astype(F32)
    n_freq = HEAD_DIM // 4
    inv = ROPE_BASE ** (-jnp.arange(n_freq, dtype=F32) / n_freq)
    ar, ac = row[:, None] * inv, col[:, None] * inv
    ang = jnp.concatenate([ar, ar, ac, ac], axis=-1)
    cos, sin = jnp.cos(ang), jnp.sin(ang)
    even = ((jnp.arange(HEAD_DIM) // n_freq) % 2 == 0).astype(F32)
    sin_dn = -sin * even
    sin_up = sin * (1.0 - even)
    return tuple(jnp.tile(a, (1, n_heads)) for a in (cos, sin_dn, sin_up))


def _tile_lanes(t, width):
    return jnp.concatenate([t] * (width // t.shape[1]), axis=1)


def _qkv(st, x, mods, layer, g1, w_qkv, q_gain, k_gain, rope):
    nq, nk = N_HEADS * HEAD_DIM, N_KV * HEAD_DIM
    bd = jnp.asarray(np.kron(np.eye(4, dtype=np.float32), np.ones((HEAD_DIM, HEAD_DIM), np.float32))).astype(BF16)
    qg = jnp.tile(q_gain.reshape(1, HEAD_DIM), (1, N_HEADS))
    kg = jnp.tile(k_gain.reshape(1, HEAD_DIM), (1, N_KV))
    base_specs = [_tok_spec(), st.mod_spec(layer), _const_spec((1, D)), _const_spec((D, nq + 2 * nk)),
                  _const_spec((1, nq)), _const_spec((1, nk)), _const_spec((4 * HEAD_DIM, 4 * HEAD_DIM))]
    outs = [jax.ShapeDtypeStruct((st.n_tok, nq), BF16), jax.ShapeDtypeStruct((st.n_tok, nk), BF16),
            jax.ShapeDtypeStruct((st.n_tok, nk), BF16)]
    out_specs = [_tok_spec(nq), _tok_spec(nk), _tok_spec(nk)]
    if rope:
        tps = st.tiles_per_seq
        tabs = _rope_tables(st.seq, LANES // HEAD_DIM)
        tab_specs = [pl.BlockSpec((TOK_TILE, LANES), lambda i: (i % tps, 0))] * 3
        return pl.pallas_call(
            _qkv_lat_kernel, grid=(st.tiles,), in_specs=base_specs + tab_specs, out_specs=out_specs,
            out_shape=outs, compiler_params=_cparams(("arbitrary",), VMEM_LIMIT_BYTES), name="qkv_latent",
        )(x, mods, g1, w_qkv, qg, kg, bd, *tabs)
    outs += [jax.ShapeDtypeStruct((st.n_tok, nk), F32)] * 2
    out_specs += [_tok_spec(nk)] * 2
    return pl.pallas_call(
        _qkv_ctx_kernel, grid=(st.tiles,), in_specs=base_specs, out_specs=out_specs,
        out_shape=outs, compiler_params=_cparams(("arbitrary",), VMEM_LIMIT_BYTES), name="qkv_context",
    )(x, mods, g1, w_qkv, qg, kg, bd)


def _qk(q, k):
    return lax.dot_general(q, k, (((1,), (1,)), ((), ())), preferred_element_type=F32)


def _gqa_attend(q, k, v, sink_ref, ok):
    nq = q.shape[0]
    row = lax.broadcasted_iota(I32, (GQA_G * nq, 1), 0)
    kvs = range(N_KV)
    cols = [slice(kv * HEAD_DIM, (kv + 1) * HEAD_DIM) for kv in kvs]

    def stacked_q(kv):
        h0 = kv * GQA_G
        return jnp.concatenate([q[:, (h0 + g) * HEAD_DIM:(h0 + g + 1) * HEAD_DIM] for g in range(GQA_G)], axis=0)

    def sink_col(kv):
        h0 = kv * GQA_G
        sink = jnp.full((GQA_G * nq, 1), sink_ref[h0], F32)
        for g in range(1, GQA_G):
            sink = jnp.where(row >= g * nq, sink_ref[h0 + g], sink)
        return sink

    ss = [_qk(stacked_q(kv), k[:, cols[kv]]) for kv in kvs]
    if ok is not None:
        ss = [jnp.where(ok, s, -jnp.inf) for s in ss]
    sinks = [sink_col(kv) for kv in kvs]
    mxs = [jnp.maximum(jnp.max(s, axis=1, keepdims=True), sk) for s, sk in zip(ss, sinks)]
    ps = [jnp.exp(s - mx) for s, mx in zip(ss, mxs)]
    dens = [jnp.sum(p, axis=1, keepdims=True) + jnp.exp(sk - mx) for p, sk, mx in zip(ps, sinks, mxs)]
    os = [_dot(p.astype(BF16), v[:, cols[kv]]) / den for kv, (p, den) in enumerate(zip(ps, dens))]
    return jnp.concatenate([o[g * nq:(g + 1) * nq] for o in os for g in range(GQA_G)], axis=1)


def _ctx_attn_kernel(sink_ref, q_ref, k_ref, v_ref, o_ref):
    o_ref[...] = _gqa_attend(q_ref[...], k_ref[...], v_ref[...], sink_ref, None).astype(BF16)


def _ctx_attn(st, q, k, v, sink):
    nq, nk = N_HEADS * HEAD_DIM, N_KV * HEAD_DIM
    seq = st.seq
    return pl.pallas_call(
        _ctx_attn_kernel,
        grid=(st.batch,),
        in_specs=[pl.BlockSpec(memory_space=pltpu.SMEM),
                  pl.BlockSpec((seq, nq), lambda b: (b, 0)),
                  pl.BlockSpec((seq, nk), lambda b: (b, 0)),
                  pl.BlockSpec((seq, nk), lambda b: (b, 0))],
        out_specs=pl.BlockSpec((seq, nq), lambda b: (b, 0)),
        out_shape=jax.ShapeDtypeStruct((st.n_tok, nq), BF16),
        compiler_params=_cparams(("arbitrary",), VMEM_LIMIT_BYTES),
        name="context_attention",
    )(sink, q, k, v)


def _lat_attn_kernel(sink_ref, q_ref, k_ref, v_ref, kc_ref, vc_ref, o_ref, *, n_blocks):
    j = pl.program_id(1)
    w = WINDOW
    jp = jnp.maximum(j - 1, 0)
    jn = jnp.minimum(j + 1, n_blocks - 1)

    def rows(ref, blk):
        return ref[pl.ds(pl.multiple_of(blk * w, w), w), :]

    kcat = jnp.concatenate([rows(k_ref, jp), rows(k_ref, j), rows(k_ref, jn), kc_ref[...].astype(BF16)], axis=0)
    vcat = jnp.concatenate([rows(v_ref, jp), rows(v_ref, j), rows(v_ref, jn), vc_ref[...].astype(BF16)], axis=0)
    n_keys = kcat.shape[0]
    qi = lax.broadcasted_iota(I32, (GQA_G * w, n_keys), 0) & (w - 1)
    ki = lax.broadcasted_iota(I32, (GQA_G * w, n_keys), 1)
    ok = (((ki < w) & (j > 0) & (ki >= qi))
          | ((ki >= w) & (ki < 2 * w))
          | ((ki >= 2 * w) & (ki < 3 * w) & (j < n_blocks - 1) & (ki - 2 * w <= qi))
          | (ki >= 3 * w))
    o_ref[...] = _gqa_attend(q_ref[...], kcat, vcat, sink_ref, ok).astype(BF16)


def _lat_attn(st, q, k, v, k_ctx, v_ctx, sink):
    nq, nk = N_HEADS * HEAD_DIM, N_KV * HEAD_DIM
    seq, past = st.seq, k_ctx.shape[0] // st.batch
    n_blocks = seq // WINDOW
    return pl.pallas_call(
        functools.partial(_lat_attn_kernel, n_blocks=n_blocks),
        grid=(st.batch, n_blocks),
        in_specs=[pl.BlockSpec(memory_space=pltpu.SMEM),
                  pl.BlockSpec((WINDOW, nq), lambda b, j: (b * n_blocks + j, 0)),
                  pl.BlockSpec((seq, nk), lambda b, j: (b, 0)),
                  pl.BlockSpec((seq, nk), lambda b, j: (b, 0)),
                  pl.BlockSpec((past, nk), lambda b, j: (b, 0)),
                  pl.BlockSpec((past, nk), lambda b, j: (b, 0))],
        out_specs=pl.BlockSpec((WINDOW, nq), lambda b, j: (b * n_blocks + j, 0)),
        out_shape=jax.ShapeDtypeStruct((st.n_tok, nq), BF16),
        compiler_params=_cparams(("arbitrary", "arbitrary"), VMEM_LIMIT_BYTES),
        name="latent_attention",
    )(sink, q, k, v, k_ctx, v_ctx)


def _fourier_kernel(x_ref, mod_ref, g_ref, cs_ref, ss_ref, cc_ref, sc_ref, o_ref, *, scale):
    m = mod_ref[...]
    h = _norm_mod(x_ref[...], g_ref[...], _mod_slice(m, 0), _mod_slice(m, 1)).astype(BF16)
    p = _dot(cs_ref[...], h).astype(BF16)
    q = _dot(ss_ref[...], h).astype(BF16)
    gw = FOURIER_GW
    cc, sc = cc_ref[...], sc_ref[...]
    outs = []
    for g in range(FOURIER_GROUPS):
        sl = slice(g * gw, (g + 1) * gw)
        outs.append(_dot(p[:, sl], cc) - _dot(q[:, sl], sc))
    o_ref[...] = (jnp.concatenate(outs, axis=1) * scale).astype(BF16)


def _dft_tables(n):
    k = np.arange(n, dtype=np.int64)
    ang = ((k[:, None] * k[None, :]) % n).astype(np.float64) * (2.0 * math.pi / n)
    return jnp.asarray(np.cos(ang), dtype=F32).astype(BF16), jnp.asarray(np.sin(ang), dtype=F32).astype(BF16)


def _fourier(st, x, mods, layer, g1):
    seq = st.seq
    cs, ss = _dft_tables(seq)
    cc, sc = _dft_tables(FOURIER_GW)
    mod_spec = pl.BlockSpec((None, None, 1, N_MOD * D), lambda b: (layer, 0 if st.shared else 1 + b, 0, 0))
    return pl.pallas_call(
        functools.partial(_fourier_kernel, scale=1.0 / math.sqrt(seq * FOURIER_GW)),
        grid=(st.batch,),
        in_specs=[pl.BlockSpec((seq, D), lambda b: (b, 0)), mod_spec, _const_spec((1, D)),
                  _const_spec((seq, seq)), _const_spec((seq, seq)),
                  _const_spec((FOURIER_GW, FOURIER_GW)), _const_spec((FOURIER_GW, FOURIER_GW))],
        out_specs=pl.BlockSpec((seq, D), lambda b: (b, 0)),
        out_shape=jax.ShapeDtypeStruct((st.n_tok, D), BF16),
        compiler_params=_cparams(("arbitrary",), VMEM_LIMIT_BYTES),
        name="fourier",
    )(x, mods, g1, cs, ss, cc, sc)


def kernel(x_prompt, x_sample, state_rglru, cache_k, cache_v, c, c_ctx, mod_w, mod_b, norm1_g, norm2_g,
           rg_w_in, rg_conv_w, rg_conv_b, rg_w_a, rg_b_a, rg_w_x, rg_b_x, rg_lambda, rg_w_out,
           at_w_qkv, at_q_norm, at_k_norm, at_sink, at_w_o, ft_w, moe_router, moe_w_gate, moe_w_up, moe_w_down):
    depth = mod_w.shape[0]
    batch, seq, _ = x_prompt.shape
    dec_batch, dec_seq, _ = x_sample.shape
    assert 1 + dec_batch <= MOD_ROWS
    streams = (_Stream(batch, seq, True), _Stream(dec_batch, dec_seq, False))
    cond = jnp.concatenate([c_ctx[None, :], c, jnp.zeros((MOD_ROWS - 1 - dec_batch, D), F32)], axis=0)
    mods = _modulation(cond, mod_w, mod_b).reshape(depth, MOD_ROWS, 1, N_MOD * D)

    xs = [x_prompt.reshape(batch * seq, D), x_sample.reshape(dec_batch * dec_seq, D)]
    new_rg, new_k, new_v = [], [], []
    n_mixers = 3
    for layer in range(depth):
        kind, j = layer % n_mixers, layer // n_mixers
        g1 = norm1_g[layer].reshape(1, D)
        g2 = norm2_g[layer].reshape(1, D)
        wr = _router_pieces(moe_router[layer])
        for si, st in enumerate(streams):
            x = xs[si]
            if kind == 0:
                gate, u = _rg_in(st, x, mods, layer, g1, rg_w_in[j].astype(BF16))
                if st.shared:
                    h0 = jnp.zeros((2, st.batch, D), F32)
                else:
                    h0 = jnp.transpose(state_rglru[:, j], (1, 0, 2))
                h, fin = _rg_scan(st, u, h0, rg_conv_w[j], rg_conv_b[j], rg_w_a[j], rg_b_a[j],
                                  rg_w_x[j], rg_b_x[j], rg_lambda[j])
                if st.shared:
                    new_rg.append(jnp.transpose(fin, (1, 0, 2)))
                x1, xnb, aff =_lin_out(st, (h, gate), rg_w_out[j].astype(BF16), x, mods, layer, g2, wr, True)
            elif kind == 1:
                w_qkv = at_w_qkv[j].astype(BF16)
                if st.shared:
                    q, k, v, kc, vc = _qkv(st, x, mods, layer, g1, w_qkv, at_q_norm[j], at_k_norm[j], False)
                    new_k.append(kc.reshape(st.batch, st.seq, N_KV, HEAD_DIM))
                    new_v.append(vc.reshape(st.batch, st.seq, N_KV, HEAD_DIM))
                    a = _ctx_attn(st, q, k, v, at_sink[j])
                else:
                    q, k, v = _qkv(st, x, mods, layer, g1, w_qkv, at_q_norm[j], at_k_norm[j], True)
                    nk = N_KV * HEAD_DIM
                    a = _lat_attn(st, q, k, v, cache_k[:, j].reshape(-1, nk), cache_v[:, j].reshape(-1, nk),
                                  at_sink[j])
                x1, xnb, aff =_lin_out(st, (a,), at_w_o[j].astype(BF16), x, mods, layer, g2, wr, False)
            else:
                a = _fourier(st, x, mods, layer, g1)
                x1, xnb, aff =_lin_out(st, (a,), ft_w[j].astype(BF16), x, mods, layer, g2, wr, False)
            xs[si] = _moe(st, x1, xnb, aff, mods, layer, moe_w_gate, moe_w_up, moe_w_down)
    return (xs[0].reshape(batch, seq, D), xs[1].reshape(dec_batch, dec_seq, D),
            jnp.stack(new_rg, axis=1), jnp.stack(new_k, axis=1), jnp.stack(new_v, axis=1))
```

```python
import functools
import math

import numpy as np
import jax
import jax.numpy as jnp
from jax import lax
from jax.experimental import pallas as pl
from jax.experimental.pallas import tpu as pltpu

F32 = jnp.float32
BF16 = jnp.bfloat16
I32 = jnp.int32
U32 = jnp.uint32

D = 1024
N_MOD = 6
EPS = 1e-6
GRID_W = 64
CONV_W = 4
CONV_LEFT = 2
LRU_C = 8.0
RNN_BLOCKS = 16
RNN_BLOCK = D // RNN_BLOCKS
N_HEADS = 16
N_KV = 4
HEAD_DIM = 64
GQA_G = N_HEADS // N_KV
WINDOW = 128
ROPE_BASE = 10000.0
FOURIER_GROUPS = 4
FOURIER_GW = D // FOURIER_GROUPS
N_EXPERTS = 16
EC_FACTOR = 2

LANES = 128
SUBLANES = 8
VMEM_LIMIT_BYTES = 56 * 1024 * 1024

TOK_TILE = 512
MOD_ROWS = 16
SCAN_CC = 128
SCAN_TC = 128
FF_CHUNK = 512


def _cparams(sem, vmem=None):
    return pltpu.CompilerParams(dimension_semantics=sem, vmem_limit_bytes=vmem)


def _split2(a):
    hi = a.astype(BF16)
    lo = (a - hi.astype(F32)).astype(BF16)
    return hi, lo


def _dot(a, b):
    return jnp.dot(a, b, preferred_element_type=F32)


def _dot3(a, b):
    a_hi, a_lo = _split2(a)
    b_hi, b_lo = _split2(b)
    return _dot(a_hi, b_hi) + _dot(a_hi, b_lo) + _dot(a_lo, b_hi)


def _norm_mod(x, g, shift, scale):
    ms = jnp.mean(x * x, axis=-1, keepdims=True)
    y = x * lax.rsqrt(ms + EPS) * g
    return y * (1.0 + scale) + shift


def _mod_slice(m, k):
    return m[:, k * D:(k + 1) * D]


class _Stream:
    def __init__(self, batch, seq, shared_cond):
        self.batch, self.seq, self.shared = batch, seq, shared_cond
        self.n_tok = batch * seq
        self.tiles = self.n_tok // TOK_TILE
        self.tiles_per_seq = max(1, seq // TOK_TILE)
        self.seqs_per_tile = max(1, TOK_TILE // seq)
        assert shared_cond or seq % TOK_TILE == 0

    def mod_row(self, i):
        return 0 if self.shared else 1 + i // self.tiles_per_seq

    def mod_spec(self, layer):
        return pl.BlockSpec((None, None, 1, N_MOD * D), lambda i: (layer, self.mod_row(i), 0, 0))

    def seq_major_spec(self):
        tps = self.tiles_per_seq
        if self.seq >= TOK_TILE:
            return pl.BlockSpec((TOK_TILE, D), lambda i: (i % tps, i // tps))
        return pl.BlockSpec((self.seq, self.seqs_per_tile * D), lambda i: (0, i))


def _tok_spec(width=D):
    return pl.BlockSpec((TOK_TILE, width), lambda i: (i, 0))


def _const_spec(shape):
    nd = len(shape)
    return pl.BlockSpec(shape, lambda i: (0,) * nd)


def _mod_kernel(c_ref, w_ref, b_ref, o_ref):
    c = c_ref[...]
    c = c * jax.nn.sigmoid(c)
    o_ref[...] = _dot3(c, w_ref[...]) + b_ref[...]


def _modulation(cond, mod_w, mod_b):
    depth = mod_w.shape[0]
    tn = N_MOD * D // 4
    return pl.pallas_call(
        _mod_kernel,
        grid=(depth, N_MOD * D // tn),
        in_specs=[pl.BlockSpec((MOD_ROWS, D), lambda l, n: (0, 0)),
                  pl.BlockSpec((None, D, tn), lambda l, n: (l, 0, n)),
                  pl.BlockSpec((None, 1, tn), lambda l, n: (l, 0, n))],
        out_specs=pl.BlockSpec((None, MOD_ROWS, tn), lambda l, n: (l, 0, n)),
        out_shape=jax.ShapeDtypeStruct((depth, MOD_ROWS, N_MOD * D), F32),
        compiler_params=_cparams(("arbitrary", "arbitrary"), VMEM_LIMIT_BYTES),
        name="modulation",
    )(cond, mod_w, mod_b.reshape(depth, 1, N_MOD * D))


def _seq_major_store(ref, val):
    seq = ref.shape[0]
    for b in range(val.shape[0] // seq):
        ref[:, b * D:(b + 1) * D] = val[b * seq:(b + 1) * seq, :]


def _seq_major_rows(ref, r0, n):
    seq = ref.shape[0]
    b, off = r0 // seq, r0 % seq
    return ref[off:off + n, b * D:(b + 1) * D]


def _rg_in_kernel(x_ref, mod_ref, g_ref, w_ref, gate_ref, u_ref):
    m = mod_ref[...]
    h = _norm_mod(x_ref[...], g_ref[...], _mod_slice(m, 0), _mod_slice(m, 1))
    gu = _dot(h.astype(BF16), w_ref[...])
    _seq_major_store(gate_ref, gu[:, :D].astype(BF16))
    _seq_major_store(u_ref, gu[:, D:])


def _rg_in(st, x, mods, layer, g1, w_in):
    shape = (st.seq, st.batch * D)
    return pl.pallas_call(
        _rg_in_kernel,
        grid=(st.tiles,),
        in_specs=[_tok_spec(), st.mod_spec(layer), _const_spec((1, D)), _const_spec((D, 2 * D))],
        out_specs=[st.seq_major_spec(), st.seq_major_spec()],
        out_shape=[jax.ShapeDtypeStruct(shape, BF16), jax.ShapeDtypeStruct(shape, F32)],
        compiler_params=_cparams(("arbitrary",), VMEM_LIMIT_BYTES),
        name="rg_in",
    )(x, mods, g1, w_in)


def _softplus(z):
    return jnp.maximum(z, 0.0) + jnp.log1p(jnp.exp(-jnp.abs(z)))


def _rg_scan_kernel(u_ref, h0_ref, cw_ref, cb_ref, wg_ref, bg_ref, lam_ref, h_ref, fin_ref,
                    upad, a_f, b_f, a_b, b_b, *, seq):
    cc = u_ref.shape[-1]
    pad_hi = CONV_W - 1 - CONV_LEFT
    upad[0:CONV_LEFT] = jnp.zeros((CONV_LEFT, SUBLANES, cc), F32)
    upad[CONV_LEFT:CONV_LEFT + seq] = u_ref[...]
    upad[CONV_LEFT + seq:CONV_LEFT + seq + pad_hi] = jnp.zeros((pad_hi, SUBLANES, cc), F32)
    k2 = (-0.5 * LRU_C * math.log2(math.e)) * _softplus(-lam_ref[...])
    cw = cw_ref[...]
    rows = SCAN_TC * SUBLANES

    def coef(c, carry):
        t0 = pl.multiple_of(c * SCAN_TC, SCAN_TC)
        uc = cb_ref[...] + cw[0:1] * upad[pl.ds(t0, SCAN_TC)]
        for k in range(1, CONV_W):
            uc = uc + cw[k:k + 1] * upad[pl.ds(t0 + k, SCAN_TC)]
        u2 = uc.reshape(rows, cc)
        gth = jnp.tanh(_dot(u2.astype(BF16), wg_ref[...]) + bg_ref[...])
        hu2 = 0.5 * u2
        for d, (a_s, b_s) in enumerate(((a_f, b_f), (a_b, b_b))):
            r_th = gth[:, (2 * d) * cc:(2 * d + 1) * cc]
            i_th = gth[:, (2 * d + 1) * cc:(2 * d + 2) * cc]
            a = jnp.exp2(k2[d] * r_th + k2[d])
            q = 1.0 - a * a
            root = jnp.where(q > 0.0, q * lax.rsqrt(q), 0.0)
            a_s[pl.ds(t0, SCAN_TC)] = a.reshape(SCAN_TC, SUBLANES, cc)
            b_s[pl.ds(t0, SCAN_TC)] = (root * (i_th * hu2 + hu2)).reshape(SCAN_TC, SUBLANES, cc)
        return carry

    lax.fori_loop(0, seq // SCAN_TC, coef, 0)

    blk = SUBLANES

    def step(i, carry):
        hf, hb = carry
        t0 = pl.multiple_of(i * blk, blk)
        av, bv = a_f[pl.ds(t0, blk)], b_f[pl.ds(t0, blk)]
        outs = []
        for k in range(0, blk, 2):
            a0, b0, a1, b1 = av[k], bv[k], av[k + 1], bv[k + 1]
            outs.append(a0 * hf + b0)
            hf = (a1 * a0) * hf + (a1 * b0 + b1)
            outs.append(hf)
        b_f[pl.ds(t0, blk)] = jnp.stack(outs, axis=0)
        s0 = seq - blk - t0
        cv, dv = a_b[pl.ds(s0, blk)], b_b[pl.ds(s0, blk)]
        outs = [None] * blk
        for k in range(blk - 1, 0, -2):
            c0, d0, c1, d1 = cv[k], dv[k], cv[k - 1], dv[k - 1]
            outs[k] = c0 * hb + d0
            hb = (c1 * c0) * hb + (c1 * d0 + d1)
            outs[k - 1] = hb
        b_b[pl.ds(s0, blk)] = jnp.stack(outs, axis=0)
        return hf, hb

    hf, hb = lax.fori_loop(0, seq // blk, step, (h0_ref[0], h0_ref[1]))
    fin_ref[0] = hf
    fin_ref[1] = hb
    h_ref[...] = b_f[...] + b_b[...]


def _blockdiag_pairs(w):
    per = SCAN_CC // RNN_BLOCK
    w4 = w.reshape(D // SCAN_CC, per, RNN_BLOCK, RNN_BLOCK)
    eye = jnp.eye(per, dtype=w.dtype)
    return jnp.einsum('cipq,ij->cipjq', w4, eye).reshape(D // SCAN_CC, SCAN_CC, SCAN_CC)


def _rg_scan(st, u, h0, conv_w, conv_b, w_a, b_a, w_x, b_x, lam):
    seq, batch = st.seq, st.batch
    n_cc = D // SCAN_CC
    wg = (0.5 * jnp.concatenate([_blockdiag_pairs(w_a[0]), _blockdiag_pairs(w_x[0]),
                                 _blockdiag_pairs(w_a[1]), _blockdiag_pairs(w_x[1])], axis=-1)).astype(BF16)
    bg = 0.5 * jnp.concatenate([b.reshape(n_cc, 1, SCAN_CC) for b in (b_a[0], b_x[0], b_a[1], b_x[1])], axis=-1)
    blk = (seq, SUBLANES, SCAN_CC)
    scr = pltpu.VMEM(blk, F32)
    h, fin = pl.pallas_call(
        functools.partial(_rg_scan_kernel, seq=seq),
        grid=(batch // SUBLANES, n_cc),
        in_specs=[pl.BlockSpec(blk, lambda b, c: (0, b, c)),
                  pl.BlockSpec((2, SUBLANES, SCAN_CC), lambda b, c: (0, b, c)),
                  pl.BlockSpec((CONV_W, 1, SCAN_CC), lambda b, c: (0, 0, c)),
                  pl.BlockSpec((1, 1, SCAN_CC), lambda b, c: (0, 0, c)),
                  pl.BlockSpec((None, SCAN_CC, 4 * SCAN_CC), lambda b, c: (c, 0, 0)),
                  pl.BlockSpec((None, 1, 4 * SCAN_CC), lambda b, c: (c, 0, 0)),
                  pl.BlockSpec((2, 1, SCAN_CC), lambda b, c: (0, 0, c))],
        out_specs=[pl.BlockSpec(blk, lambda b, c: (0, b, c)),
                   pl.BlockSpec((2, SUBLANES, SCAN_CC), lambda b, c: (0, b, c))],
        out_shape=[jax.ShapeDtypeStruct((seq, batch, D), F32),
                   jax.ShapeDtypeStruct((2, batch, D), F32)],
        scratch_shapes=[pltpu.VMEM((seq + CONV_W - 1, SUBLANES, SCAN_CC), F32), scr, scr, scr, scr],
        compiler_params=_cparams(("arbitrary", "arbitrary"), VMEM_LIMIT_BYTES),
        name="rg_scan",
    )(u.reshape(seq, batch, D), h0, conv_w.reshape(CONV_W, 1, D), conv_b.reshape(1, 1, D),
      wg, bg, lam.reshape(2, 1, D))
    return h.reshape(seq, batch * D), fin


LIN_CHUNK = LANES
LIN_CHUNKS = TOK_TILE // LIN_CHUNK


def _chunk_rows(k):
    return slice(k * LIN_CHUNK, (k + 1) * LIN_CHUNK)


def _route_and_pack(ys, m, x_ref, g2_ref, wr_ref, xo_ref, xnb_ref, aff_ref, slab_s):
    n_ch = D // LANES
    ks = range(LIN_CHUNKS)
    xs = [x_ref[_chunk_rows(k), :] + _mod_slice(m, 2) * ys[k] for k in ks]
    for k in ks:
        xo_ref[_chunk_rows(k), :] = xs[k]
    xns = [_norm_mod(x, g2_ref[...], _mod_slice(m, 3), _mod_slice(m, 4)) for x in xs]
    his = [xn.astype(BF16) for xn in xns]
    los = [(xn - hi.astype(F32)).astype(BF16) for xn, hi in zip(xns, his)]
    wr = wr_ref[...]
    l1s = [_qk(wr, hi) for hi in his]
    l2s = [_qk(wr, lo) for lo in los]
    for k in ks:
        for j in range(n_ch):
            slab_s[pl.ds(k * LIN_CHUNK * n_ch + j, LIN_CHUNK, stride=n_ch), :] = xns[k][:, j * LANES:(j + 1) * LANES]
    xnb_ref[...] = slab_s[...].astype(BF16).reshape(xnb_ref.shape)
    e = N_EXPERTS
    for k in ks:
        l1, l2 = l1s[k], l2s[k]
        logit = l1[0:e] + l1[e:2 * e] + l1[2 * e:3 * e] + l2[0:e] + l2[e:2 * e]
        mx = jnp.max(logit, axis=0, keepdims=True)
        ex = jnp.exp(logit - mx)
        aff_ref[k] = ex / jnp.sum(ex, axis=0, keepdims=True)


def _lin_out_kernel(a_ref, w_ref, x_ref, mod_ref, g2_ref, wr_ref, xo_ref, xnb_ref, aff_ref, slab_s):
    ys = [_dot(a_ref[_chunk_rows(k), :], w_ref[...]) for k in range(LIN_CHUNKS)]
    _route_and_pack(ys, mod_ref[...], x_ref, g2_ref, wr_ref, xo_ref, xnb_ref, aff_ref, slab_s)


def _lin_out_gated_kernel(h_ref, gate_ref, w_ref, x_ref, mod_ref, g2_ref, wr_ref, xo_ref, xnb_ref, aff_ref,
                          slab_s):
    def rows(ref, k):
        return _seq_major_rows(ref, k * LIN_CHUNK, LIN_CHUNK)

    acts = [(rows(h_ref, k) * jax.nn.gelu(rows(gate_ref, k).astype(F32))).astype(BF16) for k in range(LIN_CHUNKS)]
    ys = [_dot(a, w_ref[...]) for a in acts]
    _route_and_pack(ys, mod_ref[...], x_ref, g2_ref, wr_ref, xo_ref, xnb_ref, aff_ref, slab_s)


def _router_pieces(w_router):
    hi = w_router.astype(BF16)
    r1 = w_router - hi.astype(F32)
    mid = r1.astype(BF16)
    lo = (r1 - mid.astype(F32)).astype(BF16)
    return jnp.concatenate([hi, mid, lo], axis=1).T


def _lin_out(st, srcs, w, x, mods, layer, g2, wr, gated):
    n_chunks = st.n_tok // LANES
    if gated:
        body, src_specs = _lin_out_gated_kernel, [st.seq_major_spec(), st.seq_major_spec()]
    else:
        body, src_specs = _lin_out_kernel, [_tok_spec()]
    return pl.pallas_call(
        body,
        grid=(st.tiles,),
        in_specs=src_specs + [_const_spec((D, D)), _tok_spec(), st.mod_spec(layer), _const_spec((1, D)),
                              _const_spec((3 * N_EXPERTS, D))],
        out_specs=[_tok_spec(), pl.BlockSpec((TOK_TILE // 2, 2 * (D // LANES), LANES), lambda i: (i, 0, 0)),
                   pl.BlockSpec((TOK_TILE // LANES, N_EXPERTS, LANES), lambda i: (i, 0, 0))],
        out_shape=[jax.ShapeDtypeStruct((st.n_tok, D), F32),
                   jax.ShapeDtypeStruct((st.n_tok // 2, 2 * (D // LANES), LANES), BF16),
                   jax.ShapeDtypeStruct((n_chunks, N_EXPERTS, LANES), F32)],
        scratch_shapes=[pltpu.VMEM((TOK_TILE * (D // LANES), LANES), F32)],
        compiler_params=_cparams(("arbitrary",), VMEM_LIMIT_BYTES),
        name="lin_out_gated" if gated else "lin_out",
    )(*srcs, w, x, mods, g2, wr)


def _select_kernel(a_ref, idx_ref, g_ref, linc_s, cnt_s, crow_s, *, cap):
    n_e, n_ch, _ = a_ref.shape
    assert n_ch & (n_ch - 1) == 0 and n_ch <= LANES
    rows = n_e * n_ch
    a3 = a_ref[...]
    capf = jnp.float32(cap)

    def count(mask3):
        c = jnp.sum(mask3.astype(F32), axis=1, keepdims=True)
        return jnp.sum(c, axis=2, keepdims=True)

    def as_f32(bits):
        return pltpu.bitcast(bits, F32)

    def search(i, thr):
        cand = thr | (jnp.int32(1) << (30 - i))
        return jnp.where(count(a3 >= as_f32(cand)) >= capf, cand, thr)

    thr = lax.fori_loop(0, 31, search, jnp.zeros((n_e, 1, LANES), I32))
    gt3 = a3 >= as_f32(thr + 1)
    eq3 = jnp.logical_and(a3 >= as_f32(thr), jnp.logical_not(gt3))
    need = capf - count(gt3)

    li = lax.broadcasted_iota(I32, (LANES, LANES), 0)
    lj = lax.broadcasted_iota(I32, (LANES, LANES), 1)
    upper = (li <= lj).astype(BF16)
    ones = jnp.ones((LANES, LANES), BF16)
    ri = lax.broadcasted_iota(I32, (rows, rows), 0)
    rj = lax.broadcasted_iota(I32, (rows, rows), 1)
    sh = n_ch.bit_length() - 1
    before = (((ri >> sh) == (rj >> sh)) & (rj < ri)).astype(BF16)

    def prefixes(mask3):
        x = mask3.reshape(rows, LANES).astype(BF16)
        local = _dot(x, upper)
        tot = _dot(x, ones)
        return local, tot, _dot(before, tot.astype(BF16))

    gt_l, gt_t, gt_x = prefixes(gt3)
    eq_l, eq_t, eq_x = prefixes(eq3)
    need_r = jnp.broadcast_to(need, (n_e, n_ch, LANES)).reshape(rows, LANES)
    sel_incl = gt_x + gt_l + jnp.minimum(eq_x + eq_l, need_r)
    sel_x = gt_x + jnp.minimum(eq_x, need_r)
    sel_c = gt_x + gt_t + jnp.minimum(eq_x + eq_t, need_r)
    linc_s[...] = sel_incl - sel_x
    cnt_s[...] = sel_c - sel_x
    c3 = sel_c.reshape(n_e, n_ch, LANES)
    pick = lax.broadcasted_iota(I32, (1, n_ch, LANES), 1) == lax.broadcasted_iota(I32, (1, n_ch, LANES), 2)
    crow = jnp.sum(jnp.where(pick, c3, 0.0), axis=1)
    crow = jnp.where(lax.broadcasted_iota(I32, crow.shape, 1) < n_ch, crow, jnp.float32(2 * cap + n_ch * LANES))
    crow_s[...] = crow

    slot = lax.broadcasted_iota(I32, (cap, LANES), 0).astype(F32)
    lane = lax.broadcasted_iota(I32, (cap, LANES), 1).astype(F32)
    diag = lax.broadcasted_iota(I32, (LANES, LANES), 0) == lax.broadcasted_iota(I32, (LANES, LANES), 1)
    zpad_w = jnp.zeros((LANES - n_ch, 4 * LANES), BF16)
    group = 2
    assert n_e % group == 0

    def operands(e):
        r0 = pl.multiple_of(e * n_ch, n_ch)
        cnt_e = cnt_s[pl.ds(r0, n_ch), :].astype(BF16)
        linc_e = linc_s[pl.ds(r0, n_ch), :].astype(BF16)
        a_e = a_ref[e]
        a_hi = a_e.astype(BF16)
        r1 = a_e - a_hi.astype(F32)
        a_mid = r1.astype(BF16)
        a_lo = (r1 - a_mid.astype(F32)).astype(BF16)
        by_passed = jnp.concatenate([jnp.concatenate([ones[:n_ch], cnt_e], axis=1), zpad_w[:, :2 * LANES]], axis=0)
        by_chunk = jnp.concatenate([jnp.concatenate([linc_e, a_hi, a_mid, a_lo], axis=1), zpad_w], axis=0)
        return by_passed, by_chunk

    def to_row(col):
        pieces = []
        for b in range(cap // LANES):
            blk = col[b * LANES:(b + 1) * LANES, :]
            pieces.append(jnp.sum(jnp.where(diag, blk, 0.0), axis=0, keepdims=True))
        return jnp.concatenate(pieces, axis=1)

    def per_group(i, carry):
        es = [i * group + k for k in range(group)]
        ops = [operands(e) for e in es]
        passed = [(crow_s[pl.ds(e, 1), :] <= slot).astype(BF16) for e in es]
        res1 = [_dot(p, o[0]) for p, o in zip(passed, ops)]
        chunks = [r[:, :LANES] for r in res1]
        ranks = [slot - r[:, LANES:] for r in res1]
        res2 = [_dot((c == lane).astype(BF16), o[1]) for c, o in zip(chunks, ops)]
        withins = [_dot((r2[:, :LANES] <= rk).astype(BF16), ones) for r2, rk in zip(res2, ranks)]
        for e, c, w, r2 in zip(es, chunks, withins, res2):
            tok = c * jnp.float32(LANES) + w
            arow = r2[:, LANES:2 * LANES] + r2[:, 2 * LANES:3 * LANES] + r2[:, 3 * LANES:]
            gate = jnp.sum(jnp.where(lane == w, arow, 0.0), axis=1, keepdims=True)
            idx_ref[pl.ds(e, 1), :] = to_row(tok).astype(I32)
            g_ref[pl.ds(e, 1), :] = to_row(jnp.broadcast_to(gate, (cap, LANES)))
        return carry

    lax.fori_loop(0, n_e // group, per_group, 0)


def _select(aff_chunks, cap):
    n_ch = aff_chunks.shape[0]
    a = jnp.transpose(aff_chunks, (1, 0, 2))
    rows = N_EXPERTS * n_ch
    return pl.pallas_call(
        functools.partial(_select_kernel, cap=cap),
        out_shape=[jax.ShapeDtypeStruct((N_EXPERTS, cap), I32),
                   jax.ShapeDtypeStruct((N_EXPERTS, cap), F32)],
        scratch_shapes=[pltpu.VMEM((rows, LANES), F32), pltpu.VMEM((rows, LANES), F32),
                        pltpu.VMEM((N_EXPERTS, LANES), F32)],
        compiler_params=pltpu.CompilerParams(vmem_limit_bytes=VMEM_LIMIT_BYTES),
        name="select",
    )(a)


ROW_CH = D // LANES
GATHER_UNROLL = 16
FFN_ROWS = 512


def _cm_stride(m):
    return m + SUBLANES


def _moe_ffn_kernel(idx_ref, src_ref, g_ref, wg_ref, wu_ref, wd_ref, o_ref, tile_s, xe_s, *, cap):
    e = pl.program_id(0)
    f = pl.program_id(1)
    stride = _cm_stride(cap)
    tm = min(FFN_ROWS, cap)

    def gather_token(p, n):
        pair = src_ref[n >> 1].astype(F32)
        odd = (jnp.full((ROW_CH, LANES), n, I32) & 1) == 1
        tile_s[pl.ds(p, ROW_CH, stride=stride), :] = jnp.where(odd, pair[ROW_CH:], pair[:ROW_CH])

    @pl.when(jnp.logical_and(e == 0, f == 0))
    def _first_gather():
        def body(gi, carry):
            base = pl.multiple_of(gi * GATHER_UNROLL, GATHER_UNROLL)
            for k in range(GATHER_UNROLL):
                gather_token(base + k, idx_ref[base + k])
            return carry

        lax.fori_loop(0, cap // GATHER_UNROLL, body, 0)

    def ffn(xt):
        hg = _dot(xt, wg_ref[...].astype(BF16))
        hu = _dot(xt, wu_ref[...].astype(BF16))
        h = (hg * jax.nn.sigmoid(hg) * hu).astype(BF16)
        return _dot(h, wd_ref[...].astype(BF16))

    @pl.when(f == 0)
    def _half0():
        for t in range(cap // tm):
            r0 = t * tm
            xt = jnp.concatenate([tile_s[pl.ds(j * stride + r0, tm), :].astype(BF16) for j in range(ROW_CH)],
                                 axis=1)
            xe_s[r0:r0 + tm, :] = xt
            y = ffn(xt)
            for j in range(ROW_CH):
                o_ref[pl.ds(j * stride + r0, tm), :] = y[:, j * LANES:(j + 1) * LANES]

    @pl.when(f == 1)
    def _half1():
        nxt = (e + 1) * cap
        for t in range(cap // tm):
            r0 = t * tm
            def gather_share(q):
                for p in range(r0 + q * tm // 4, r0 + (q + 1) * tm // 4):
                    gather_token(p, idx_ref[nxt + p])

            xt = xe_s[r0:r0 + tm, :]
            gather_share(0)
            hg = _dot(xt, wg_ref[...].astype(BF16))
            gather_share(1)
            hu = _dot(xt, wu_ref[...].astype(BF16))
            h = (hg * jax.nn.sigmoid(hg) * hu).astype(BF16)
            gather_share(2)
            y = _dot(h, wd_ref[...].astype(BF16))
            gather_share(3)
            gt = g_ref[r0:r0 + tm, :]
            for j in range(ROW_CH):
                rows = pl.ds(j * stride + r0, tm)
                o_ref[rows, :] = (o_ref[rows, :] + y[:, j * LANES:(j + 1) * LANES]) * gt
        for j in range(ROW_CH):
            o_ref[pl.ds(j * stride + cap, stride - cap), :] = jnp.zeros((stride - cap, LANES), F32)


def _moe_ffn(idx, gates, xnb, w_gate, w_up, w_down, layer, cap):
    n_tok = 2 * xnb.shape[0]
    ff = w_gate.shape[-1]
    assert ff == 2 * FF_CHUNK
    stride = _cm_stride(cap)
    src = xnb
    idx = jnp.concatenate([idx, jnp.zeros((cap,), I32)])
    grid_spec = pltpu.PrefetchScalarGridSpec(
        num_scalar_prefetch=1,
        grid=(N_EXPERTS, ff // FF_CHUNK),
        in_specs=[pl.BlockSpec((n_tok // 2, 2 * ROW_CH, LANES), lambda e, f, idx: (0, 0, 0),
                               pipeline_mode=pl.Buffered(1)),
                  pl.BlockSpec((None, cap, 1), lambda e, f, idx: (e, 0, 0)),
                  pl.BlockSpec((None, None, D, FF_CHUNK), lambda e, f, idx: (layer, e, 0, f)),
                  pl.BlockSpec((None, None, D, FF_CHUNK), lambda e, f, idx: (layer, e, 0, f)),
                  pl.BlockSpec((None, None, FF_CHUNK, D), lambda e, f, idx: (layer, e, f, 0))],
        out_specs=pl.BlockSpec((None, ROW_CH * stride, LANES), lambda e, f, idx: (e, 0, 0)),
        scratch_shapes=[pltpu.VMEM((ROW_CH * stride, LANES), F32), pltpu.VMEM((cap, D), BF16)],
    )
    return pl.pallas_call(
        functools.partial(_moe_ffn_kernel, cap=cap),
        grid_spec=grid_spec,
        out_shape=jax.ShapeDtypeStruct((N_EXPERTS, ROW_CH * stride, LANES), F32),
        compiler_params=_cparams(("arbitrary", "arbitrary"), VMEM_LIMIT_BYTES),
        name="moe_ffn",
    )(idx, src, gates.reshape(N_EXPERTS, cap, 1), w_gate, w_up, w_down)


COMBINE_VMEM_LIMIT_BYTES = 60 * 1024 * 1024


def _combine_kernel(idx_ref, ye_ref, x_ref, mod_ref, o_ref, acc_s, *, cap):
    s = pl.program_id(0)
    stride = _cm_stride(cap)

    @pl.when(s == 0)
    def _zero():
        acc_s[...] = jnp.zeros(acc_s.shape, F32)

    @pl.when(s < N_EXPERTS)
    def _scatter():
        def body(gi, carry):
            base = pl.multiple_of(gi * GATHER_UNROLL, GATHER_UNROLL)
            rows, sums = [], []
            for k in range(GATHER_UNROLL):
                r = pl.multiple_of(idx_ref[s * cap + base + k] * ROW_CH, ROW_CH)
                rows.append(r)
                sums.append(acc_s[pl.ds(r, ROW_CH), :] + ye_ref[pl.ds(base + k, ROW_CH, stride=stride), :])
            for r, v in zip(rows, sums):
                acc_s[pl.ds(r, ROW_CH), :] = v
            return carry

        lax.fori_loop(0, cap // GATHER_UNROLL, body, 0)

    @pl.when(s >= N_EXPERTS)
    def _residual():
        r0 = pl.multiple_of((s - N_EXPERTS) * (TOK_TILE * ROW_CH), TOK_TILE * ROW_CH)
        y = jnp.concatenate([acc_s[pl.ds(r0 + j, TOK_TILE, stride=ROW_CH), :] for j in range(ROW_CH)], axis=1)
        o_ref[...] = x_ref[...] + _mod_slice(mod_ref[...], 5) * y


def _combine(st, idx, ye, x, mods, layer, cap):
    stride = _cm_stride(cap)
    n_e = N_EXPERTS

    def tile(s):
        return jnp.maximum(s - n_e, 0)

    grid_spec = pltpu.PrefetchScalarGridSpec(
        num_scalar_prefetch=1,
        grid=(n_e + st.tiles,),
        in_specs=[pl.BlockSpec((None, ROW_CH * stride, LANES), lambda s, idx: (jnp.minimum(s, n_e - 1), 0, 0)),
                  pl.BlockSpec((TOK_TILE, D), lambda s, idx: (tile(s), 0)),
                  pl.BlockSpec((None, None, 1, N_MOD * D), lambda s, idx: (layer, st.mod_row(tile(s)), 0, 0))],
        out_specs=pl.BlockSpec((TOK_TILE, D), lambda s, idx: (tile(s), 0)),
        scratch_shapes=[pltpu.VMEM((st.n_tok * ROW_CH, LANES), F32)],
    )
    return pl.pallas_call(
        functools.partial(_combine_kernel, cap=cap),
        grid_spec=grid_spec,
        out_shape=jax.ShapeDtypeStruct((st.n_tok, D), F32),
        compiler_params=_cparams(("arbitrary",), COMBINE_VMEM_LIMIT_BYTES),
        name="moe_combine",
    )(idx, ye, x, mods)


def _moe(st, x, xnb, aff_chunks, mods, layer, w_gate, w_up, w_down):
    cap = EC_FACTOR * st.n_tok // N_EXPERTS
    idx, gates = _select(aff_chunks, cap)
    idx = idx.reshape(N_EXPERTS * cap)
    ye = _moe_ffn(idx, gates, xnb, w_gate, w_up, w_down, layer, cap)
    return _combine(st, idx, ye, x, mods, layer, cap)


def _head_sumsq(x, bd):
    x2 = x * x
    hi, lo = _split2(x2)
    w = bd.shape[0]
    cols = []
    for c in range(x.shape[1] // w):
        sl = slice(c * w, (c + 1) * w)
        cols.append(_dot(hi[:, sl], bd) + _dot(lo[:, sl], bd))
    return cols[0] if len(cols) == 1 else jnp.concatenate(cols, axis=1)


def _qk_norm(x, gain, bd):
    ms = _head_sumsq(x, bd) * (1.0 / HEAD_DIM)
    return x * lax.rsqrt(ms + EPS) * gain


def _rope(x, cos, sin_dn, sin_up):
    n = x.shape[1]
    q = HEAD_DIM // 4
    return x * cos + pltpu.roll(x, n - q, 1) * sin_dn + pltpu.roll(x, q, 1) * sin_up


def _qkv_ctx_kernel(x_ref, mod_ref, g_ref, w_ref, qg_ref, kg_ref, bd_ref, q_ref, k_ref, v_ref, kc_ref, vc_ref):
    m = mod_ref[...]
    h = _norm_mod(x_ref[...], g_ref[...], _mod_slice(m, 0), _mod_slice(m, 1))
    qkv = _dot(h.astype(BF16), w_ref[...])
    nq, nk = N_HEADS * HEAD_DIM, N_KV * HEAD_DIM
    bd = bd_ref[...]
    q = _qk_norm(qkv[:, :nq], qg_ref[...], bd)
    k = _qk_norm(qkv[:, nq:nq + nk], kg_ref[...], bd)
    v = qkv[:, nq + nk:]
    q_ref[...] = (q * (HEAD_DIM ** -0.5)).astype(BF16)
    k_ref[...] = k.astype(BF16)
    v_ref[...] = v.astype(BF16)
    kc_ref[...] = k
    vc_ref[...] = v


def _qkv_lat_kernel(x_ref, mod_ref, g_ref, w_ref, qg_ref, kg_ref, bd_ref,
                    cos_ref, sdn_ref, sup_ref, q_ref, k_ref, v_ref):
    m = mod_ref[...]
    h = _norm_mod(x_ref[...], g_ref[...], _mod_slice(m, 0), _mod_slice(m, 1))
    qkv = _dot(h.astype(BF16), w_ref[...])
    nq, nk = N_HEADS * HEAD_DIM, N_KV * HEAD_DIM
    bd = bd_ref[...]
    q = _qk_norm(qkv[:, :nq], qg_ref[...], bd)
    k = _qk_norm(qkv[:, nq:nq + nk], kg_ref[...], bd)
    tabs = (cos_ref[...], sdn_ref[...], sup_ref[...])
    q = _rope(q, *(_tile_lanes(t, nq) for t in tabs))
    k = _rope(k, *(_tile_lanes(t, nk) for t in tabs))
    q_ref[...] = (q * (HEAD_DIM ** -0.5)).astype(BF16)
    k_ref[...] = k.astype(BF16)
    v_ref[...] = qkv[:, nq + nk:].astype(BF16)


def _rope_tables(seq, n_heads):
    n_rows = seq // GRID_W
    row = jnp.repeat(jnp.arange(n_rows), GRID_W).astype(F32)
    col = jnp.tile(jnp.arange(GRID_W), n_rows).astype(F32)
    n_freq = HEAD_DIM // 4
    inv = ROPE_BASE ** (-jnp.arange(n_freq, dtype=F32) / n_freq)
    ar, ac = row[:, None] * inv, col[:, None] * inv
    ang = jnp.concatenate([ar, ar, ac, ac], axis=-1)
    cos, sin = jnp.cos(ang), jnp.sin(ang)
    even = ((jnp.arange(HEAD_DIM) // n_freq) % 2 == 0).astype(F32)
    sin_dn = -sin * even
    sin_up = sin * (1.0 - even)
    return tuple(jnp.tile(a, (1, n_heads)) for a in (cos, sin_dn, sin_up))


def _tile_lanes(t, width):
    return jnp.concatenate([t] * (width // t.shape[1]), axis=1)


def _qkv(st, x, mods, layer, g1, w_qkv, q_gain, k_gain, rope):
    nq, nk = N_HEADS * HEAD_DIM, N_KV * HEAD_DIM
    bd = jnp.asarray(np.kron(np.eye(4, dtype=np.float32), np.ones((HEAD_DIM, HEAD_DIM), np.float32))).astype(BF16)
    qg = jnp.tile(q_gain.reshape(1, HEAD_DIM), (1, N_HEADS))
    kg = jnp.tile(k_gain.reshape(1, HEAD_DIM), (1, N_KV))
    base_specs = [_tok_spec(), st.mod_spec(layer), _const_spec((1, D)), _const_spec((D, nq + 2 * nk)),
                  _const_spec((1, nq)), _const_spec((1, nk)), _const_spec((4 * HEAD_DIM, 4 * HEAD_DIM))]
    outs = [jax.ShapeDtypeStruct((st.n_tok, nq), BF16), jax.ShapeDtypeStruct((st.n_tok, nk), BF16),
            jax.ShapeDtypeStruct((st.n_tok, nk), BF16)]
    out_specs = [_tok_spec(nq), _tok_spec(nk), _tok_spec(nk)]
    if rope:
        tps = st.tiles_per_seq
        tabs = _rope_tables(st.seq, LANES // HEAD_DIM)
        tab_specs = [pl.BlockSpec((TOK_TILE, LANES), lambda i: (i % tps, 0))] * 3
        return pl.pallas_call(
            _qkv_lat_kernel, grid=(st.tiles,), in_specs=base_specs + tab_specs, out_specs=out_specs,
            out_shape=outs, compiler_params=_cparams(("arbitrary",), VMEM_LIMIT_BYTES), name="qkv_latent",
        )(x, mods, g1, w_qkv, qg, kg, bd, *tabs)
    outs += [jax.ShapeDtypeStruct((st.n_tok, nk), F32)] * 2
    out_specs += [_tok_spec(nk)] * 2
    return pl.pallas_call(
        _qkv_ctx_kernel, grid=(st.tiles,), in_specs=base_specs, out_specs=out_specs,
        out_shape=outs, compiler_params=_cparams(("arbitrary",), VMEM_LIMIT_BYTES), name="qkv_context",
    )(x, mods, g1, w_qkv, qg, kg, bd)


def _qk(q, k):
    return lax.dot_general(q, k, (((1,), (1,)), ((), ())), preferred_element_type=F32)


def _gqa_attend(q, k, v, sink_ref, ok):
    nq = q.shape[0]
    row = lax.broadcasted_iota(I32, (GQA_G * nq, 1), 0)
    kvs = range(N_KV)
    cols = [slice(kv * HEAD_DIM, (kv + 1) * HEAD_DIM) for kv in kvs]

    def stacked_q(kv):
        h0 = kv * GQA_G
        return jnp.concatenate([q[:, (h0 + g) * HEAD_DIM:(h0 + g + 1) * HEAD_DIM] for g in range(GQA_G)], axis=0)

    def sink_col(kv):
        h0 = kv * GQA_G
        sink = jnp.full((GQA_G * nq, 1), sink_ref[h0], F32)
        for g in range(1, GQA_G):
            sink = jnp.where(row >= g * nq, sink_ref[h0 + g], sink)
        return sink

    ss = [_qk(stacked_q(kv), k[:, cols[kv]]) for kv in kvs]
    if ok is not None:
        ss = [jnp.where(ok, s, -jnp.inf) for s in ss]
    sinks = [sink_col(kv) for kv in kvs]
    mxs = [jnp.maximum(jnp.max(s, axis=1, keepdims=True), sk) for s, sk in zip(ss, sinks)]
    ps = [jnp.exp(s - mx) for s, mx in zip(ss, mxs)]
    dens = [jnp.sum(p, axis=1, keepdims=True) + jnp.exp(sk - mx) for p, sk, mx in zip(ps, sinks, mxs)]
    os = [_dot(p.astype(BF16), v[:, cols[kv]]) / den for kv, (p, den) in enumerate(zip(ps, dens))]
    return jnp.concatenate([o[g * nq:(g + 1) * nq] for o in os for g in range(GQA_G)], axis=1)


def _ctx_attn_kernel(sink_ref, q_ref, k_ref, v_ref, o_ref):
    o_ref[...] = _gqa_attend(q_ref[...], k_ref[...], v_ref[...], sink_ref, None).astype(BF16)


def _ctx_attn(st, q, k, v, sink):
    nq, nk = N_HEADS * HEAD_DIM, N_KV * HEAD_DIM
    seq = st.seq
    return pl.pallas_call(
        _ctx_attn_kernel,
        grid=(st.batch,),
        in_specs=[pl.BlockSpec(memory_space=pltpu.SMEM),
                  pl.BlockSpec((seq, nq), lambda b: (b, 0)),
                  pl.BlockSpec((seq, nk), lambda b: (b, 0)),
                  pl.BlockSpec((seq, nk), lambda b: (b, 0))],
        out_specs=pl.BlockSpec((seq, nq), lambda b: (b, 0)),
        out_shape=jax.ShapeDtypeStruct((st.n_tok, nq), BF16),
        compiler_params=_cparams(("arbitrary",), VMEM_LIMIT_BYTES),
        name="context_attention",
    )(sink, q, k, v)


def _lat_attn_kernel(sink_ref, q_ref, k_ref, v_ref, kc_ref, vc_ref, o_ref, *, n_blocks):
    j = pl.program_id(1)
    w = WINDOW
    jp = jnp.maximum(j - 1, 0)
    jn = jnp.minimum(j + 1, n_blocks - 1)

    def rows(ref, blk):
        return ref[pl.ds(pl.multiple_of(blk * w, w), w), :]

    kcat = jnp.concatenate([rows(k_ref, jp), rows(k_ref, j), rows(k_ref, jn), kc_ref[...].astype(BF16)], axis=0)
    vcat = jnp.concatenate([rows(v_ref, jp), rows(v_ref, j), rows(v_ref, jn), vc_ref[...].astype(BF16)], axis=0)
    n_keys = kcat.shape[0]
    qi = lax.broadcasted_iota(I32, (GQA_G * w, n_keys), 0) & (w - 1)
    ki = lax.broadcasted_iota(I32, (GQA_G * w, n_keys), 1)
    ok = (((ki < w) & (j > 0) & (ki >= qi))
          | ((ki >= w) & (ki < 2 * w))
          | ((ki >= 2 * w) & (ki < 3 * w) & (j < n_blocks - 1) & (ki - 2 * w <= qi))
          | (ki >= 3 * w))
    o_ref[...] = _gqa_attend(q_ref[...], kcat, vcat, sink_ref, ok).astype(BF16)


def _lat_attn(st, q, k, v, k_ctx, v_ctx, sink):
    nq, nk = N_HEADS * HEAD_DIM, N_KV * HEAD_DIM
    seq, past = st.seq, k_ctx.shape[0] // st.batch
    n_blocks = seq // WINDOW
    return pl.pallas_call(
        functools.partial(_lat_attn_kernel, n_blocks=n_blocks),
        grid=(st.batch, n_blocks),
        in_specs=[pl.BlockSpec(memory_space=pltpu.SMEM),
                  pl.BlockSpec((WINDOW, nq), lambda b, j: (b * n_blocks + j, 0)),
                  pl.BlockSpec((seq, nk), lambda b, j: (b, 0)),
                  pl.BlockSpec((seq, nk), lambda b, j: (b, 0)),
                  pl.BlockSpec((past, nk), lambda b, j: (b, 0)),
                  pl.BlockSpec((past, nk), lambda b, j: (b, 0))],
        out_specs=pl.BlockSpec((WINDOW, nq), lambda b, j: (b * n_blocks + j, 0)),
        out_shape=jax.ShapeDtypeStruct((st.n_tok, nq), BF16),
        compiler_params=_cparams(("arbitrary", "arbitrary"), VMEM_LIMIT_BYTES),
        name="latent_attention",
    )(sink, q, k, v, k_ctx, v_ctx)


def _fourier_kernel(x_ref, mod_ref, g_ref, cs_ref, ss_ref, cc_ref, sc_ref, o_ref, *, scale):
    m = mod_ref[...]
    h = _norm_mod(x_ref[...], g_ref[...], _mod_slice(m, 0), _mod_slice(m, 1)).astype(BF16)
    p = _dot(cs_ref[...], h).astype(BF16)
    q = _dot(ss_ref[...], h).astype(BF16)
    gw = FOURIER_GW
    cc, sc = cc_ref[...], sc_ref[...]
    outs = []
    for g in range(FOURIER_GROUPS):
        sl = slice(g * gw, (g + 1) * gw)
        outs.append(_dot(p[:, sl], cc) - _dot(q[:, sl], sc))
    o_ref[...] = (jnp.concatenate(outs, axis=1) * scale).astype(BF16)


def _dft_tables(n):
    k = np.arange(n, dtype=np.int64)
    ang = ((k[:, None] * k[None, :]) % n).astype(np.float64) * (2.0 * math.pi / n)
    return jnp.asarray(np.cos(ang), dtype=F32).astype(BF16), jnp.asarray(np.sin(ang), dtype=F32).astype(BF16)


def _fourier(st, x, mods, layer, g1):
    seq = st.seq
    cs, ss = _dft_tables(seq)
    cc, sc = _dft_tables(FOURIER_GW)
    mod_spec = pl.BlockSpec((None, None, 1, N_MOD * D), lambda b: (layer, 0 if st.shared else 1 + b, 0, 0))
    return pl.pallas_call(
        functools.partial(_fourier_kernel, scale=1.0 / math.sqrt(seq * FOURIER_GW)),
        grid=(st.batch,),
        in_specs=[pl.BlockSpec((seq, D), lambda b: (b, 0)), mod_spec, _const_spec((1, D)),
                  _const_spec((seq, seq)), _const_spec((seq, seq)),
                  _const_spec((FOURIER_GW, FOURIER_GW)), _const_spec((FOURIER_GW, FOURIER_GW))],
        out_specs=pl.BlockSpec((seq, D), lambda b: (b, 0)),
        out_shape=jax.ShapeDtypeStruct((st.n_tok, D), BF16),
        compiler_params=_cparams(("arbitrary",), VMEM_LIMIT_BYTES),
        name="fourier",
    )(x, mods, g1, cs, ss, cc, sc)


def kernel(x_prompt, x_sample, state_rglru, cache_k, cache_v, c, c_ctx, mod_w, mod_b, norm1_g, norm2_g,
           rg_w_in, rg_conv_w, rg_conv_b, rg_w_a, rg_b_a, rg_w_x, rg_b_x, rg_lambda, rg_w_out,
           at_w_qkv, at_q_norm, at_k_norm, at_sink, at_w_o, ft_w, moe_router, moe_w_gate, moe_w_up, moe_w_down):
    depth = mod_w.shape[0]
    batch, seq, _ = x_prompt.shape
    dec_batch, dec_seq, _ = x_sample.shape
    assert 1 + dec_batch <= MOD_ROWS
    streams = (_Stream(batch, seq, True), _Stream(dec_batch, dec_seq, False))
    cond = jnp.concatenate([c_ctx[None, :], c, jnp.zeros((MOD_ROWS - 1 - dec_batch, D), F32)], axis=0)
    mods = _modulation(cond, mod_w, mod_b).reshape(depth, MOD_ROWS, 1, N_MOD * D)

    xs = [x_prompt.reshape(batch * seq, D), x_sample.reshape(dec_batch * dec_seq, D)]
    new_rg, new_k, new_v = [], [], []
    n_mixers = 3
    for layer in range(depth):
        kind, j = layer % n_mixers, layer // n_mixers
        g1 = norm1_g[layer].reshape(1, D)
        g2 = norm2_g[layer].reshape(1, D)
        wr = _router_pieces(moe_router[layer])
        for si, st in enumerate(streams):
            x = xs[si]
            if kind == 0:
                gate, u = _rg_in(st, x, mods, layer, g1, rg_w_in[j].astype(BF16))
                if st.shared:
                    h0 = jnp.zeros((2, st.batch, D), F32)
                else:
                    h0 = jnp.transpose(state_rglru[:, j], (1, 0, 2))
                h, fin = _rg_scan(st, u, h0, rg_conv_w[j], rg_conv_b[j], rg_w_a[j], rg_b_a[j],
                                  rg_w_x[j], rg_b_x[j], rg_lambda[j])
                if st.shared:
                    new_rg.append(jnp.transpose(fin, (1, 0, 2)))
                x1, xnb, aff =_lin_out(st, (h, gate), rg_w_out[j].astype(BF16), x, mods, layer, g2, wr, True)
            elif kind == 1:
                w_qkv = at_w_qkv[j].astype(BF16)
                if st.shared:
                    q, k, v, kc, vc = _qkv(st, x, mods, layer, g1, w_qkv, at_q_norm[j], at_k_norm[j], False)
                    new_k.append(kc.reshape(st.batch, st.seq, N_KV, HEAD_DIM))
                    new_v.append(vc.reshape(st.batch, st.seq, N_KV, HEAD_DIM))
                    a = _ctx_attn(st, q, k, v, at_sink[j])
                else:
                    q, k, v = _qkv(st, x, mods, layer, g1, w_qkv, at_q_norm[j], at_k_norm[j], True)
                    nk = N_KV * HEAD_DIM
                    a = _lat_attn(st, q, k, v, cache_k[:, j].reshape(-1, nk), cache_v[:, j].reshape(-1, nk),
                                  at_sink[j])
                x1, xnb, aff =_lin_out(st, (a,), at_w_o[j].astype(BF16), x, mods, layer, g2, wr, False)
            else:
                a = _fourier(st, x, mods, layer, g1)
                x1, xnb, aff =_lin_out(st, (a,), ft_w[j].astype(BF16), x, mods, layer, g2, wr, False)
            xs[si] = _moe(st, x1, xnb, aff, mods, layer, moe_w_gate, moe_w_up, moe_w_down)
    return (xs[0].reshape(batch, seq, D), xs[1].reshape(dec_batch, dec_seq, D),
            jnp.stack(new_rg, axis=1), jnp.stack(new_k, axis=1), jnp.stack(new_v, axis=1))
```

```python
import functools
import math

import numpy as np
import jax
import jax.numpy as jnp
from jax import lax
from jax.experimental import pallas as pl
from jax.experimental.pallas import tpu as pltpu

F32 = jnp.float32
BF16 = jnp.bfloat16
I32 = jnp.int32
U32 = jnp.uint32

D = 1024
N_MOD = 6
EPS = 1e-6
GRID_W = 64
CONV_W = 4
CONV_LEFT = 2
LRU_C = 8.0
RNN_BLOCKS = 16
RNN_BLOCK = D // RNN_BLOCKS
N_HEADS = 16
N_KV = 4
HEAD_DIM = 64
GQA_G = N_HEADS // N_KV
WINDOW = 128
ROPE_BASE = 10000.0
FOURIER_GROUPS = 4
FOURIER_GW = D // FOURIER_GROUPS
N_EXPERTS = 16
EC_FACTOR = 2

LANES = 128
SUBLANES = 8
VMEM_LIMIT_BYTES = 56 * 1024 * 1024

TOK_TILE = 1024
MOD_ROWS = 16
SCAN_CC = 128
SCAN_TC = 128
FF_CHUNK = 512


def _cparams(sem, vmem=None):
    return pltpu.CompilerParams(dimension_semantics=sem, vmem_limit_bytes=vmem)


def _split2(a):
    hi = a.astype(BF16)
    lo = (a - hi.astype(F32)).astype(BF16)
    return hi, lo


def _dot(a, b):
    return jnp.dot(a, b, preferred_element_type=F32)


def _dot3(a, b):
    a_hi, a_lo = _split2(a)
    b_hi, b_lo = _split2(b)
    return _dot(a_hi, b_hi) + _dot(a_hi, b_lo) + _dot(a_lo, b_hi)


def _norm_mod(x, g, shift, scale):
    ms = jnp.mean(x * x, axis=-1, keepdims=True)
    y = x * lax.rsqrt(ms + EPS) * g
    return y * (1.0 + scale) + shift


def _mod_slice(m, k):
    return m[:, k * D:(k + 1) * D]


class _Stream:
    def __init__(self, batch, seq, shared_cond):
        self.batch, self.seq, self.shared = batch, seq, shared_cond
        self.n_tok = batch * seq
        self.tiles = self.n_tok // TOK_TILE
        self.tiles_per_seq = max(1, seq // TOK_TILE)
        self.seqs_per_tile = max(1, TOK_TILE // seq)
        assert shared_cond or seq % TOK_TILE == 0

    def mod_row(self, i):
        return 0 if self.shared else 1 + i // self.tiles_per_seq

    def mod_spec(self, layer):
        return pl.BlockSpec((None, None, 1, N_MOD * D), lambda i: (layer, self.mod_row(i), 0, 0))

    def seq_major_spec(self):
        tps = self.tiles_per_seq
        if self.seq >= TOK_TILE:
            return pl.BlockSpec((TOK_TILE, D), lambda i: (i % tps, i // tps))
        return pl.BlockSpec((self.seq, self.seqs_per_tile * D), lambda i: (0, i))


def _tok_spec(width=D):
    return pl.BlockSpec((TOK_TILE, width), lambda i: (i, 0))


def _const_spec(shape):
    nd = len(shape)
    return pl.BlockSpec(shape, lambda i: (0,) * nd)


def _mod_kernel(c_ref, w_ref, b_ref, o_ref):
    c = c_ref[...]
    c = c * jax.nn.sigmoid(c)
    o_ref[...] = _dot3(c, w_ref[...]) + b_ref[...]


def _modulation(cond, mod_w, mod_b):
    depth = mod_w.shape[0]
    tn = N_MOD * D // 4
    return pl.pallas_call(
        _mod_kernel,
        grid=(depth, N_MOD * D // tn),
        in_specs=[pl.BlockSpec((MOD_ROWS, D), lambda l, n: (0, 0)),
                  pl.BlockSpec((None, D, tn), lambda l, n: (l, 0, n)),
                  pl.BlockSpec((None, 1, tn), lambda l, n: (l, 0, n))],
        out_specs=pl.BlockSpec((None, MOD_ROWS, tn), lambda l, n: (l, 0, n)),
        out_shape=jax.ShapeDtypeStruct((depth, MOD_ROWS, N_MOD * D), F32),
        compiler_params=_cparams(("arbitrary", "arbitrary"), VMEM_LIMIT_BYTES),
        name="modulation",
    )(cond, mod_w, mod_b.reshape(depth, 1, N_MOD * D))


def _seq_major_store(ref, val):
    seq = ref.shape[0]
    for b in range(val.shape[0] // seq):
        ref[:, b * D:(b + 1) * D] = val[b * seq:(b + 1) * seq, :]


def _seq_major_rows(ref, r0, n):
    seq = ref.shape[0]
    b, off = r0 // seq, r0 % seq
    return ref[off:off + n, b * D:(b + 1) * D]


def _rg_in_kernel(x_ref, mod_ref, g_ref, w_ref, gate_ref, u_ref):
    m = mod_ref[...]
    h = _norm_mod(x_ref[...], g_ref[...], _mod_slice(m, 0), _mod_slice(m, 1))
    gu = _dot(h.astype(BF16), w_ref[...])
    _seq_major_store(gate_ref, gu[:, :D].astype(BF16))
    _seq_major_store(u_ref, gu[:, D:])


def _rg_in(st, x, mods, layer, g1, w_in):
    shape = (st.seq, st.batch * D)
    return pl.pallas_call(
        _rg_in_kernel,
        grid=(st.tiles,),
        in_specs=[_tok_spec(), st.mod_spec(layer), _const_spec((1, D)), _const_spec((D, 2 * D))],
        out_specs=[st.seq_major_spec(), st.seq_major_spec()],
        out_shape=[jax.ShapeDtypeStruct(shape, BF16), jax.ShapeDtypeStruct(shape, F32)],
        compiler_params=_cparams(("arbitrary",), VMEM_LIMIT_BYTES),
        name="rg_in",
    )(x, mods, g1, w_in)


def _softplus(z):
    return jnp.maximum(z, 0.0) + jnp.log1p(jnp.exp(-jnp.abs(z)))


def _rg_scan_kernel(u_ref, h0_ref, cw_ref, cb_ref, wg_ref, bg_ref, lam_ref, h_ref, fin_ref,
                    upad, a_f, b_f, a_b, b_b, *, seq):
    cc = u_ref.shape[-1]
    pad_hi = CONV_W - 1 - CONV_LEFT
    upad[0:CONV_LEFT] = jnp.zeros((CONV_LEFT, SUBLANES, cc), F32)
    upad[CONV_LEFT:CONV_LEFT + seq] = u_ref[...]
    upad[CONV_LEFT + seq:CONV_LEFT + seq + pad_hi] = jnp.zeros((pad_hi, SUBLANES, cc), F32)
    k2 = (-0.5 * LRU_C * math.log2(math.e)) * _softplus(-lam_ref[...])
    cw = cw_ref[...]
    rows = SCAN_TC * SUBLANES

    def coef(c, carry):
        t0 = pl.multiple_of(c * SCAN_TC, SCAN_TC)
        uc = cb_ref[...] + cw[0:1] * upad[pl.ds(t0, SCAN_TC)]
        for k in range(1, CONV_W):
            uc = uc + cw[k:k + 1] * upad[pl.ds(t0 + k, SCAN_TC)]
        u2 = uc.reshape(rows, cc)
        gth = jnp.tanh(_dot(u2.astype(BF16), wg_ref[...]) + bg_ref[...])
        hu2 = 0.5 * u2
        for d, (a_s, b_s) in enumerate(((a_f, b_f), (a_b, b_b))):
            r_th = gth[:, (2 * d) * cc:(2 * d + 1) * cc]
            i_th = gth[:, (2 * d + 1) * cc:(2 * d + 2) * cc]
            a = jnp.exp2(k2[d] * r_th + k2[d])
            q = 1.0 - a * a
            root = jnp.where(q > 0.0, q * lax.rsqrt(q), 0.0)
            a_s[pl.ds(t0, SCAN_TC)] = a.reshape(SCAN_TC, SUBLANES, cc)
            b_s[pl.ds(t0, SCAN_TC)] = (root * (i_th * hu2 + hu2)).reshape(SCAN_TC, SUBLANES, cc)
        return carry

    lax.fori_loop(0, seq // SCAN_TC, coef, 0)

    blk = SUBLANES

    def step(i, carry):
        hf, hb = carry
        t0 = pl.multiple_of(i * blk, blk)
        av, bv = a_f[pl.ds(t0, blk)], b_f[pl.ds(t0, blk)]
        outs = []
        for k in range(0, blk, 2):
            a0, b0, a1, b1 = av[k], bv[k], av[k + 1], bv[k + 1]
            outs.append(a0 * hf + b0)
            hf = (a1 * a0) * hf + (a1 * b0 + b1)
            outs.append(hf)
        b_f[pl.ds(t0, blk)] = jnp.stack(outs, axis=0)
        s0 = seq - blk - t0
        cv, dv = a_b[pl.ds(s0, blk)], b_b[pl.ds(s0, blk)]
        outs = [None] * blk
        for k in range(blk - 1, 0, -2):
            c0, d0, c1, d1 = cv[k], dv[k], cv[k - 1], dv[k - 1]
            outs[k] = c0 * hb + d0
            hb = (c1 * c0) * hb + (c1 * d0 + d1)
            outs[k - 1] = hb
        b_b[pl.ds(s0, blk)] = jnp.stack(outs, axis=0)
        return hf, hb

    hf, hb = lax.fori_loop(0, seq // blk, step, (h0_ref[0], h0_ref[1]))
    fin_ref[0] = hf
    fin_ref[1] = hb
    h_ref[...] = b_f[...] + b_b[...]


def _blockdiag_pairs(w):
    per = SCAN_CC // RNN_BLOCK
    w4 = w.reshape(D // SCAN_CC, per, RNN_BLOCK, RNN_BLOCK)
    eye = jnp.eye(per, dtype=w.dtype)
    return jnp.einsum('cipq,ij->cipjq', w4, eye).reshape(D // SCAN_CC, SCAN_CC, SCAN_CC)


def _rg_scan(st, u, h0, conv_w, conv_b, w_a, b_a, w_x, b_x, lam):
    seq, batch = st.seq, st.batch
    n_cc = D // SCAN_CC
    wg = (0.5 * jnp.concatenate([_blockdiag_pairs(w_a[0]), _blockdiag_pairs(w_x[0]),
                                 _blockdiag_pairs(w_a[1]), _blockdiag_pairs(w_x[1])], axis=-1)).astype(BF16)
    bg = 0.5 * jnp.concatenate([b.reshape(n_cc, 1, SCAN_CC) for b in (b_a[0], b_x[0], b_a[1], b_x[1])], axis=-1)
    blk = (seq, SUBLANES, SCAN_CC)
    scr = pltpu.VMEM(blk, F32)
    h, fin = pl.pallas_call(
        functools.partial(_rg_scan_kernel, seq=seq),
        grid=(batch // SUBLANES, n_cc),
        in_specs=[pl.BlockSpec(blk, lambda b, c: (0, b, c)),
                  pl.BlockSpec((2, SUBLANES, SCAN_CC), lambda b, c: (0, b, c)),
                  pl.BlockSpec((CONV_W, 1, SCAN_CC), lambda b, c: (0, 0, c)),
                  pl.BlockSpec((1, 1, SCAN_CC), lambda b, c: (0, 0, c)),
                  pl.BlockSpec((None, SCAN_CC, 4 * SCAN_CC), lambda b, c: (c, 0, 0)),
                  pl.BlockSpec((None, 1, 4 * SCAN_CC), lambda b, c: (c, 0, 0)),
                  pl.BlockSpec((2, 1, SCAN_CC), lambda b, c: (0, 0, c))],
        out_specs=[pl.BlockSpec(blk, lambda b, c: (0, b, c)),
                   pl.BlockSpec((2, SUBLANES, SCAN_CC), lambda b, c: (0, b, c))],
        out_shape=[jax.ShapeDtypeStruct((seq, batch, D), F32),
                   jax.ShapeDtypeStruct((2, batch, D), F32)],
        scratch_shapes=[pltpu.VMEM((seq + CONV_W - 1, SUBLANES, SCAN_CC), F32), scr, scr, scr, scr],
        compiler_params=_cparams(("arbitrary", "arbitrary"), VMEM_LIMIT_BYTES),
        name="rg_scan",
    )(u.reshape(seq, batch, D), h0, conv_w.reshape(CONV_W, 1, D), conv_b.reshape(1, 1, D),
      wg, bg, lam.reshape(2, 1, D))
    return h.reshape(seq, batch * D), fin


LIN_CHUNK = LANES
LIN_CHUNKS = TOK_TILE // LIN_CHUNK


def _chunk_rows(k):
    return slice(k * LIN_CHUNK, (k + 1) * LIN_CHUNK)


def _route_and_pack(ys, m, x_ref, g2_ref, wr_ref, xo_ref, xnb_ref, aff_ref, slab_s):
    n_ch = D // LANES
    ks = range(LIN_CHUNKS)
    xs = [x_ref[_chunk_rows(k), :] + _mod_slice(m, 2) * ys[k] for k in ks]
    for k in ks:
        xo_ref[_chunk_rows(k), :] = xs[k]
    xns = [_norm_mod(x, g2_ref[...], _mod_slice(m, 3), _mod_slice(m, 4)) for x in xs]
    his = [xn.astype(BF16) for xn in xns]
    los = [(xn - hi.astype(F32)).astype(BF16) for xn, hi in zip(xns, his)]
    wr = wr_ref[...]
    l1s = [_qk(wr, hi) for hi in his]
    l2s = [_qk(wr, lo) for lo in los]
    for k in ks:
        for j in range(n_ch):
            slab_s[pl.ds(k * LIN_CHUNK * n_ch + j, LIN_CHUNK, stride=n_ch), :] = xns[k][:, j * LANES:(j + 1) * LANES]
    xnb_ref[...] = slab_s[...].astype(BF16).reshape(xnb_ref.shape)
    e = N_EXPERTS
    for k in ks:
        l1, l2 = l1s[k], l2s[k]
        logit = l1[0:e] + l1[e:2 * e] + l1[2 * e:3 * e] + l2[0:e] + l2[e:2 * e]
        mx = jnp.max(logit, axis=0, keepdims=True)
        ex = jnp.exp(logit - mx)
        aff_ref[k] = ex / jnp.sum(ex, axis=0, keepdims=True)


def _lin_out_kernel(a_ref, w_ref, x_ref, mod_ref, g2_ref, wr_ref, xo_ref, xnb_ref, aff_ref, slab_s):
    ys = [_dot(a_ref[_chunk_rows(k), :], w_ref[...]) for k in range(LIN_CHUNKS)]
    _route_and_pack(ys, mod_ref[...], x_ref, g2_ref, wr_ref, xo_ref, xnb_ref, aff_ref, slab_s)


def _lin_out_gated_kernel(h_ref, gate_ref, w_ref, x_ref, mod_ref, g2_ref, wr_ref, xo_ref, xnb_ref, aff_ref,
                          slab_s):
    def rows(ref, k):
        return _seq_major_rows(ref, k * LIN_CHUNK, LIN_CHUNK)

    acts = [(rows(h_ref, k) * jax.nn.gelu(rows(gate_ref, k).astype(F32))).astype(BF16) for k in range(LIN_CHUNKS)]
    ys = [_dot(a, w_ref[...]) for a in acts]
    _route_and_pack(ys, mod_ref[...], x_ref, g2_ref, wr_ref, xo_ref, xnb_ref, aff_ref, slab_s)


def _router_pieces(w_router):
    hi = w_router.astype(BF16)
    r1 = w_router - hi.astype(F32)
    mid = r1.astype(BF16)
    lo = (r1 - mid.astype(F32)).astype(BF16)
    return jnp.concatenate([hi, mid, lo], axis=1).T


def _lin_out(st, srcs, w, x, mods, layer, g2, wr, gated):
    n_chunks = st.n_tok // LANES
    if gated:
        body, src_specs = _lin_out_gated_kernel, [st.seq_major_spec(), st.seq_major_spec()]
    else:
        body, src_specs = _lin_out_kernel, [_tok_spec()]
    return pl.pallas_call(
        body,
        grid=(st.tiles,),
        in_specs=src_specs + [_const_spec((D, D)), _tok_spec(), st.mod_spec(layer), _const_spec((1, D)),
                              _const_spec((3 * N_EXPERTS, D))],
        out_specs=[_tok_spec(), pl.BlockSpec((TOK_TILE // 2, 2 * (D // LANES), LANES), lambda i: (i, 0, 0)),
                   pl.BlockSpec((TOK_TILE // LANES, N_EXPERTS, LANES), lambda i: (i, 0, 0))],
        out_shape=[jax.ShapeDtypeStruct((st.n_tok, D), F32),
                   jax.ShapeDtypeStruct((st.n_tok // 2, 2 * (D // LANES), LANES), BF16),
                   jax.ShapeDtypeStruct((n_chunks, N_EXPERTS, LANES), F32)],
        scratch_shapes=[pltpu.VMEM((TOK_TILE * (D // LANES), LANES), F32)],
        compiler_params=_cparams(("arbitrary",), VMEM_LIMIT_BYTES),
        name="lin_out_gated" if gated else "lin_out",
    )(*srcs, w, x, mods, g2, wr)


def _select_kernel(a_ref, idx_ref, g_ref, linc_s, cnt_s, crow_s, *, cap):
    n_e, n_ch, _ = a_ref.shape
    assert n_ch & (n_ch - 1) == 0 and n_ch <= LANES
    rows = n_e * n_ch
    a3 = a_ref[...]
    capf = jnp.float32(cap)

    def count(mask3):
        c = jnp.sum(mask3.astype(F32), axis=1, keepdims=True)
        return jnp.sum(c, axis=2, keepdims=True)

    def as_f32(bits):
        return pltpu.bitcast(bits, F32)

    def search(i, thr):
        cand = thr | (jnp.int32(1) << (30 - i))
        return jnp.where(count(a3 >= as_f32(cand)) >= capf, cand, thr)

    thr = lax.fori_loop(0, 31, search, jnp.zeros((n_e, 1, LANES), I32))
    gt3 = a3 >= as_f32(thr + 1)
    eq3 = jnp.logical_and(a3 >= as_f32(thr), jnp.logical_not(gt3))
    need = capf - count(gt3)

    li = lax.broadcasted_iota(I32, (LANES, LANES), 0)
    lj = lax.broadcasted_iota(I32, (LANES, LANES), 1)
    upper = (li <= lj).astype(BF16)
    ones = jnp.ones((LANES, LANES), BF16)
    ri = lax.broadcasted_iota(I32, (rows, rows), 0)
    rj = lax.broadcasted_iota(I32, (rows, rows), 1)
    sh = n_ch.bit_length() - 1
    before = (((ri >> sh) == (rj >> sh)) & (rj < ri)).astype(BF16)

    def prefixes(mask3):
        x = mask3.reshape(rows, LANES).astype(BF16)
        local = _dot(x, upper)
        tot = _dot(x, ones)
        return local, tot, _dot(before, tot.astype(BF16))

    gt_l, gt_t, gt_x = prefixes(gt3)
    eq_l, eq_t, eq_x = prefixes(eq3)
    need_r = jnp.broadcast_to(need, (n_e, n_ch, LANES)).reshape(rows, LANES)
    sel_incl = gt_x + gt_l + jnp.minimum(eq_x + eq_l, need_r)
    sel_x = gt_x + jnp.minimum(eq_x, need_r)
    sel_c = gt_x + gt_t + jnp.minimum(eq_x + eq_t, need_r)
    linc_s[...] = sel_incl - sel_x
    cnt_s[...] = sel_c - sel_x
    c3 = sel_c.reshape(n_e, n_ch, LANES)
    pick = lax.broadcasted_iota(I32, (1, n_ch, LANES), 1) == lax.broadcasted_iota(I32, (1, n_ch, LANES), 2)
    crow = jnp.sum(jnp.where(pick, c3, 0.0), axis=1)
    crow = jnp.where(lax.broadcasted_iota(I32, crow.shape, 1) < n_ch, crow, jnp.float32(2 * cap + n_ch * LANES))
    crow_s[...] = crow

    slot = lax.broadcasted_iota(I32, (cap, LANES), 0).astype(F32)
    lane = lax.broadcasted_iota(I32, (cap, LANES), 1).astype(F32)
    diag = lax.broadcasted_iota(I32, (LANES, LANES), 0) == lax.broadcasted_iota(I32, (LANES, LANES), 1)
    zpad_w = jnp.zeros((LANES - n_ch, 4 * LANES), BF16)
    group = 2
    assert n_e % group == 0

    def operands(e):
        r0 = pl.multiple_of(e * n_ch, n_ch)
        cnt_e = cnt_s[pl.ds(r0, n_ch), :].astype(BF16)
        linc_e = linc_s[pl.ds(r0, n_ch), :].astype(BF16)
        a_e = a_ref[e]
        a_hi = a_e.astype(BF16)
        r1 = a_e - a_hi.astype(F32)
        a_mid = r1.astype(BF16)
        a_lo = (r1 - a_mid.astype(F32)).astype(BF16)
        by_passed = jnp.concatenate([jnp.concatenate([ones[:n_ch], cnt_e], axis=1), zpad_w[:, :2 * LANES]], axis=0)
        by_chunk = jnp.concatenate([jnp.concatenate([linc_e, a_hi, a_mid, a_lo], axis=1), zpad_w], axis=0)
        return by_passed, by_chunk

    def to_row(col):
        pieces = []
        for b in range(cap // LANES):
            blk = col[b * LANES:(b + 1) * LANES, :]
            pieces.append(jnp.sum(jnp.where(diag, blk, 0.0), axis=0, keepdims=True))
        return jnp.concatenate(pieces, axis=1)

    def per_group(i, carry):
        es = [i * group + k for k in range(group)]
        ops = [operands(e) for e in es]
        passed = [(crow_s[pl.ds(e, 1), :] <= slot).astype(BF16) for e in es]
        res1 = [_dot(p, o[0]) for p, o in zip(passed, ops)]
        chunks = [r[:, :LANES] for r in res1]
        ranks = [slot - r[:, LANES:] for r in res1]
        res2 = [_dot((c == lane).astype(BF16), o[1]) for c, o in zip(chunks, ops)]
        withins = [_dot((r2[:, :LANES] <= rk).astype(BF16), ones) for r2, rk in zip(res2, ranks)]
        for e, c, w, r2 in zip(es, chunks, withins, res2):
            tok = c * jnp.float32(LANES) + w
            arow = r2[:, LANES:2 * LANES] + r2[:, 2 * LANES:3 * LANES] + r2[:, 3 * LANES:]
            gate = jnp.sum(jnp.where(lane == w, arow, 0.0), axis=1, keepdims=True)
            idx_ref[pl.ds(e, 1), :] = to_row(tok).astype(I32)
            g_ref[pl.ds(e, 1), :] = to_row(jnp.broadcast_to(gate, (cap, LANES)))
        return carry

    lax.fori_loop(0, n_e // group, per_group, 0)


def _select(aff_chunks, cap):
    n_ch = aff_chunks.shape[0]
    a = jnp.transpose(aff_chunks, (1, 0, 2))
    rows = N_EXPERTS * n_ch
    return pl.pallas_call(
        functools.partial(_select_kernel, cap=cap),
        out_shape=[jax.ShapeDtypeStruct((N_EXPERTS, cap), I32),
                   jax.ShapeDtypeStruct((N_EXPERTS, cap), F32)],
        scratch_shapes=[pltpu.VMEM((rows, LANES), F32), pltpu.VMEM((rows, LANES), F32),
                        pltpu.VMEM((N_EXPERTS, LANES), F32)],
        compiler_params=pltpu.CompilerParams(vmem_limit_bytes=VMEM_LIMIT_BYTES),
        name="select",
    )(a)


ROW_CH = D // LANES
GATHER_UNROLL = 16
FFN_ROWS = 512


def _cm_stride(m):
    return m + SUBLANES


def _moe_ffn_kernel(idx_ref, src_ref, g_ref, wg_ref, wu_ref, wd_ref, o_ref, tile_s, xe_s, *, cap):
    e = pl.program_id(0)
    f = pl.program_id(1)
    stride = _cm_stride(cap)
    tm = min(FFN_ROWS, cap)

    def gather_token(p, n):
        pair = src_ref[n >> 1].astype(F32)
        odd = (jnp.full((ROW_CH, LANES), n, I32) & 1) == 1
        tile_s[pl.ds(p, ROW_CH, stride=stride), :] = jnp.where(odd, pair[ROW_CH:], pair[:ROW_CH])

    @pl.when(jnp.logical_and(e == 0, f == 0))
    def _first_gather():
        def body(gi, carry):
            base = pl.multiple_of(gi * GATHER_UNROLL, GATHER_UNROLL)
            for k in range(GATHER_UNROLL):
                gather_token(base + k, idx_ref[base + k])
            return carry

        lax.fori_loop(0, cap // GATHER_UNROLL, body, 0)

    def ffn(xt):
        hg = _dot(xt, wg_ref[...].astype(BF16))
        hu = _dot(xt, wu_ref[...].astype(BF16))
        h = (hg * jax.nn.sigmoid(hg) * hu).astype(BF16)
        return _dot(h, wd_ref[...].astype(BF16))

    @pl.when(f == 0)
    def _half0():
        for t in range(cap // tm):
            r0 = t * tm
            xt = jnp.concatenate([tile_s[pl.ds(j * stride + r0, tm), :].astype(BF16) for j in range(ROW_CH)],
                                 axis=1)
            xe_s[r0:r0 + tm, :] = xt
            y = ffn(xt)
            for j in range(ROW_CH):
                o_ref[pl.ds(j * stride + r0, tm), :] = y[:, j * LANES:(j + 1) * LANES]

    @pl.when(f == 1)
    def _half1():
        nxt = (e + 1) * cap
        for t in range(cap // tm):
            r0 = t * tm
            def gather_share(q):
                for p in range(r0 + q * tm // 4, r0 + (q + 1) * tm // 4):
                    gather_token(p, idx_ref[nxt + p])

            xt = xe_s[r0:r0 + tm, :]
            gather_share(0)
            hg = _dot(xt, wg_ref[...].astype(BF16))
            gather_share(1)
            hu = _dot(xt, wu_ref[...].astype(BF16))
            h = (hg * jax.nn.sigmoid(hg) * hu).astype(BF16)
            gather_share(2)
            y = _dot(h, wd_ref[...].astype(BF16))
            gather_share(3)
            gt = g_ref[r0:r0 + tm, :]
            for j in range(ROW_CH):
                rows = pl.ds(j * stride + r0, tm)
                o_ref[rows, :] = (o_ref[rows, :] + y[:, j * LANES:(j + 1) * LANES]) * gt
        for j in range(ROW_CH):
            o_ref[pl.ds(j * stride + cap, stride - cap), :] = jnp.zeros((stride - cap, LANES), F32)


def _moe_ffn(idx, gates, xnb, w_gate, w_up, w_down, layer, cap):
    n_tok = 2 * xnb.shape[0]
    ff = w_gate.shape[-1]
    assert ff == 2 * FF_CHUNK
    stride = _cm_stride(cap)
    src = xnb
    idx = jnp.concatenate([idx, jnp.zeros((cap,), I32)])
    grid_spec = pltpu.PrefetchScalarGridSpec(
        num_scalar_prefetch=1,
        grid=(N_EXPERTS, ff // FF_CHUNK),
        in_specs=[pl.BlockSpec((n_tok // 2, 2 * ROW_CH, LANES), lambda e, f, idx: (0, 0, 0),
                               pipeline_mode=pl.Buffered(1)),
                  pl.BlockSpec((None, cap, 1), lambda e, f, idx: (e, 0, 0)),
                  pl.BlockSpec((None, None, D, FF_CHUNK), lambda e, f, idx: (layer, e, 0, f)),
                  pl.BlockSpec((None, None, D, FF_CHUNK), lambda e, f, idx: (layer, e, 0, f)),
                  pl.BlockSpec((None, None, FF_CHUNK, D), lambda e, f, idx: (layer, e, f, 0))],
        out_specs=pl.BlockSpec((None, ROW_CH * stride, LANES), lambda e, f, idx: (e, 0, 0)),
        scratch_shapes=[pltpu.VMEM((ROW_CH * stride, LANES), F32), pltpu.VMEM((cap, D), BF16)],
    )
    return pl.pallas_call(
        functools.partial(_moe_ffn_kernel, cap=cap),
        grid_spec=grid_spec,
        out_shape=jax.ShapeDtypeStruct((N_EXPERTS, ROW_CH * stride, LANES), F32),
        compiler_params=_cparams(("arbitrary", "arbitrary"), VMEM_LIMIT_BYTES),
        name="moe_ffn",
    )(idx, src, gates.reshape(N_EXPERTS, cap, 1), w_gate, w_up, w_down)


COMBINE_VMEM_LIMIT_BYTES = 60 * 1024 * 1024


def _combine_kernel(idx_ref, ye_ref, x_ref, mod_ref, o_ref, acc_s, *, cap):
    s = pl.program_id(0)
    stride = _cm_stride(cap)

    @pl.when(s == 0)
    def _zero():
        acc_s[...] = jnp.zeros(acc_s.shape, F32)

    @pl.when(s < N_EXPERTS)
    def _scatter():
        def body(gi, carry):
            base = pl.multiple_of(gi * GATHER_UNROLL, GATHER_UNROLL)
            rows, sums = [], []
            for k in range(GATHER_UNROLL):
                r = pl.multiple_of(idx_ref[s * cap + base + k] * ROW_CH, ROW_CH)
                rows.append(r)
                sums.append(acc_s[pl.ds(r, ROW_CH), :] + ye_ref[pl.ds(base + k, ROW_CH, stride=stride), :])
            for r, v in zip(rows, sums):
                acc_s[pl.ds(r, ROW_CH), :] = v
            return carry

        lax.fori_loop(0, cap // GATHER_UNROLL, body, 0)

    @pl.when(s >= N_EXPERTS)
    def _residual():
        r0 = pl.multiple_of((s - N_EXPERTS) * (TOK_TILE * ROW_CH), TOK_TILE * ROW_CH)
        y = jnp.concatenate([acc_s[pl.ds(r0 + j, TOK_TILE, stride=ROW_CH), :] for j in range(ROW_CH)], axis=1)
        o_ref[...] = x_ref[...] + _mod_slice(mod_ref[...], 5) * y


def _combine(st, idx, ye, x, mods, layer, cap):
    stride = _cm_stride(cap)
    n_e = N_EXPERTS

    def tile(s):
        return jnp.maximum(s - n_e, 0)

    grid_spec = pltpu.PrefetchScalarGridSpec(
        num_scalar_prefetch=1,
        grid=(n_e + st.tiles,),
        in_specs=[pl.BlockSpec((None, ROW_CH * stride, LANES), lambda s, idx: (jnp.minimum(s, n_e - 1), 0, 0)),
                  pl.BlockSpec((TOK_TILE, D), lambda s, idx: (tile(s), 0)),
                  pl.BlockSpec((None, None, 1, N_MOD * D), lambda s, idx: (layer, st.mod_row(tile(s)), 0, 0))],
        out_specs=pl.BlockSpec((TOK_TILE, D), lambda s, idx: (tile(s), 0)),
        scratch_shapes=[pltpu.VMEM((st.n_tok * ROW_CH, LANES), F32)],
    )
    return pl.pallas_call(
        functools.partial(_combine_kernel, cap=cap),
        grid_spec=grid_spec,
        out_shape=jax.ShapeDtypeStruct((st.n_tok, D), F32),
        compiler_params=_cparams(("arbitrary",), COMBINE_VMEM_LIMIT_BYTES),
        name="moe_combine",
    )(idx, ye, x, mods)


def _moe(st, x, xnb, aff_chunks, mods, layer, w_gate, w_up, w_down):
    cap = EC_FACTOR * st.n_tok // N_EXPERTS
    idx, gates = _select(aff_chunks, cap)
    idx = idx.reshape(N_EXPERTS * cap)
    ye = _moe_ffn(idx, gates, xnb, w_gate, w_up, w_down, layer, cap)
    return _combine(st, idx, ye, x, mods, layer, cap)


def _head_sumsq(x, bd):
    x2 = x * x
    hi, lo = _split2(x2)
    w = bd.shape[0]
    cols = []
    for c in range(x.shape[1] // w):
        sl = slice(c * w, (c + 1) * w)
        cols.append(_dot(hi[:, sl], bd) + _dot(lo[:, sl], bd))
    return cols[0] if len(cols) == 1 else jnp.concatenate(cols, axis=1)


def _qk_norm(x, gain, bd):
    ms = _head_sumsq(x, bd) * (1.0 / HEAD_DIM)
    return x * lax.rsqrt(ms + EPS) * gain


def _rope(x, cos, sin_dn, sin_up):
    n = x.shape[1]
    q = HEAD_DIM // 4
    return x * cos + pltpu.roll(x, n - q, 1) * sin_dn + pltpu.roll(x, q, 1) * sin_up


def _qkv_ctx_kernel(x_ref, mod_ref, g_ref, w_ref, qg_ref, kg_ref, bd_ref, q_ref, k_ref, v_ref, kc_ref, vc_ref):
    m = mod_ref[...]
    h = _norm_mod(x_ref[...], g_ref[...], _mod_slice(m, 0), _mod_slice(m, 1))
    qkv = _dot(h.astype(BF16), w_ref[...])
    nq, nk = N_HEADS * HEAD_DIM, N_KV * HEAD_DIM
    bd = bd_ref[...]
    q = _qk_norm(qkv[:, :nq], qg_ref[...], bd)
    k = _qk_norm(qkv[:, nq:nq + nk], kg_ref[...], bd)
    v = qkv[:, nq + nk:]
    q_ref[...] = (q * (HEAD_DIM ** -0.5)).astype(BF16)
    k_ref[...] = k.astype(BF16)
    v_ref[...] = v.astype(BF16)
    kc_ref[...] = k
    vc_ref[...] = v


def _qkv_lat_kernel(x_ref, mod_ref, g_ref, w_ref, qg_ref, kg_ref, bd_ref,
                    cos_ref, sdn_ref, sup_ref, q_ref, k_ref, v_ref):
    m = mod_ref[...]
    h = _norm_mod(x_ref[...], g_ref[...], _mod_slice(m, 0), _mod_slice(m, 1))
    qkv = _dot(h.astype(BF16), w_ref[...])
    nq, nk = N_HEADS * HEAD_DIM, N_KV * HEAD_DIM
    bd = bd_ref[...]
    q = _qk_norm(qkv[:, :nq], qg_ref[...], bd)
    k = _qk_norm(qkv[:, nq:nq + nk], kg_ref[...], bd)
    tabs = (cos_ref[...], sdn_ref[...], sup_ref[...])
    q = _rope(q, *(_tile_lanes(t, nq) for t in tabs))
    k = _rope(k, *(_tile_lanes(t, nk) for t in tabs))
    q_ref[...] = (q * (HEAD_DIM ** -0.5)).astype(BF16)
    k_ref[...] = k.astype(BF16)
    v_ref[...] = qkv[:, nq + nk:].astype(BF16)


def _rope_tables(seq, n_heads):
    n_rows = seq // GRID_W
    row = jnp.repeat(jnp.arange(n_rows), GRID_W).astype(F32)
    col = jnp.tile(jnp.arange(GRID_W), n_rows).astype(F32)
    n_freq = HEAD_DIM // 4
    inv = ROPE_BASE ** (-jnp.arange(n_freq, dtype=F32) / n_freq)
    ar, ac = row[:, None] * inv, col[:, None] * inv
    ang = jnp.concatenate([ar, ar, ac, ac], axis=-1)
    cos, sin = jnp.cos(ang), jnp.sin(ang)
    even = ((jnp.arange(HEAD_DIM) // n_freq) % 2 == 0).astype(F32)
    sin_dn = -sin * even
    sin_up = sin * (1.0 - even)
    return tuple(jnp.tile(a, (1, n_heads)) for a in (cos, sin_dn, sin_up))


def _tile_lanes(t, width):
    return jnp.concatenate([t] * (width // t.shape[1]), axis=1)


def _qkv(st, x, mods, layer, g1, w_qkv, q_gain, k_gain, rope):
    nq, nk = N_HEADS * HEAD_DIM, N_KV * HEAD_DIM
    bd = jnp.asarray(np.kron(np.eye(4, dtype=np.float32), np.ones((HEAD_DIM, HEAD_DIM), np.float32))).astype(BF16)
    qg = jnp.tile(q_gain.reshape(1, HEAD_DIM), (1, N_HEADS))
    kg = jnp.tile(k_gain.reshape(1, HEAD_DIM), (1, N_KV))
    base_specs = [_tok_spec(), st.mod_spec(layer), _const_spec((1, D)), _const_spec((D, nq + 2 * nk)),
                  _const_spec((1, nq)), _const_spec((1, nk)), _const_spec((4 * HEAD_DIM, 4 * HEAD_DIM))]
    outs = [jax.ShapeDtypeStruct((st.n_tok, nq), BF16), jax.ShapeDtypeStruct((st.n_tok, nk), BF16),
            jax.ShapeDtypeStruct((st.n_tok, nk), BF16)]
    out_specs = [_tok_spec(nq), _tok_spec(nk), _tok_spec(nk)]
    if rope:
        tps = st.tiles_per_seq
        tabs = _rope_tables(st.seq, LANES // HEAD_DIM)
        tab_specs = [pl.BlockSpec((TOK_TILE, LANES), lambda i: (i % tps, 0))] * 3
        return pl.pallas_call(
            _qkv_lat_kernel, grid=(st.tiles,), in_specs=base_specs + tab_specs, out_specs=out_specs,
            out_shape=outs, compiler_params=_cparams(("arbitrary",), VMEM_LIMIT_BYTES), name="qkv_latent",
        )(x, mods, g1, w_qkv, qg, kg, bd, *tabs)
    outs += [jax.ShapeDtypeStruct((st.n_tok, nk), F32)] * 2
    out_specs += [_tok_spec(nk)] * 2
    return pl.pallas_call(
        _qkv_ctx_kernel, grid=(st.tiles,), in_specs=base_specs, out_specs=out_specs,
        out_shape=outs, compiler_params=_cparams(("arbitrary",), VMEM_LIMIT_BYTES), name="qkv_context",
    )(x, mods, g1, w_qkv, qg, kg, bd)


def _qk(q, k):
    return lax.dot_general(q, k, (((1,), (1,)), ((), ())), preferred_element_type=F32)


def _gqa_attend(q, k, v, sink_ref, ok):
    nq = q.shape[0]
    row = lax.broadcasted_iota(I32, (GQA_G * nq, 1), 0)
    kvs = range(N_KV)
    cols = [slice(kv * HEAD_DIM, (kv + 1) * HEAD_DIM) for kv in kvs]

    def stacked_q(kv):
        h0 = kv * GQA_G
        return jnp.concatenate([q[:, (h0 + g) * HEAD_DIM:(h0 + g + 1) * HEAD_DIM] for g in range(GQA_G)], axis=0)

    def sink_col(kv):
        h0 = kv * GQA_G
        sink = jnp.full((GQA_G * nq, 1), sink_ref[h0], F32)
        for g in range(1, GQA_G):
            sink = jnp.where(row >= g * nq, sink_ref[h0 + g], sink)
        return sink

    ss = [_qk(stacked_q(kv), k[:, cols[kv]]) for kv in kvs]
    if ok is not None:
        ss = [jnp.where(ok, s, -jnp.inf) for s in ss]
    sinks = [sink_col(kv) for kv in kvs]
    mxs = [jnp.maximum(jnp.max(s, axis=1, keepdims=True), sk) for s, sk in zip(ss, sinks)]
    ps = [jnp.exp(s - mx) for s, mx in zip(ss, mxs)]
    dens = [jnp.sum(p, axis=1, keepdims=True) + jnp.exp(sk - mx) for p, sk, mx in zip(ps, sinks, mxs)]
    os = [_dot(p.astype(BF16), v[:, cols[kv]]) / den for kv, (p, den) in enumerate(zip(ps, dens))]
    return jnp.concatenate([o[g * nq:(g + 1) * nq] for o in os for g in range(GQA_G)], axis=1)


def _ctx_attn_kernel(sink_ref, q_ref, k_ref, v_ref, o_ref):
    o_ref[...] = _gqa_attend(q_ref[...], k_ref[...], v_ref[...], sink_ref, None).astype(BF16)


def _ctx_attn(st, q, k, v, sink):
    nq, nk = N_HEADS * HEAD_DIM, N_KV * HEAD_DIM
    seq = st.seq
    return pl.pallas_call(
        _ctx_attn_kernel,
        grid=(st.batch,),
        in_specs=[pl.BlockSpec(memory_space=pltpu.SMEM),
                  pl.BlockSpec((seq, nq), lambda b: (b, 0)),
                  pl.BlockSpec((seq, nk), lambda b: (b, 0)),
                  pl.BlockSpec((seq, nk), lambda b: (b, 0))],
        out_specs=pl.BlockSpec((seq, nq), lambda b: (b, 0)),
        out_shape=jax.ShapeDtypeStruct((st.n_tok, nq), BF16),
        compiler_params=_cparams(("arbitrary",), VMEM_LIMIT_BYTES),
        name="context_attention",
    )(sink, q, k, v)


def _lat_attn_kernel(sink_ref, q_ref, k_ref, v_ref, kc_ref, vc_ref, o_ref, *, n_blocks):
    j = pl.program_id(1)
    w = WINDOW
    jp = jnp.maximum(j - 1, 0)
    jn = jnp.minimum(j + 1, n_blocks - 1)

    def rows(ref, blk):
        return ref[pl.ds(pl.multiple_of(blk * w, w), w), :]

    kcat = jnp.concatenate([rows(k_ref, jp), rows(k_ref, j), rows(k_ref, jn), kc_ref[...].astype(BF16)], axis=0)
    vcat = jnp.concatenate([rows(v_ref, jp), rows(v_ref, j), rows(v_ref, jn), vc_ref[...].astype(BF16)], axis=0)
    n_keys = kcat.shape[0]
    qi = lax.broadcasted_iota(I32, (GQA_G * w, n_keys), 0) & (w - 1)
    ki = lax.broadcasted_iota(I32, (GQA_G * w, n_keys), 1)
    ok = (((ki < w) & (j > 0) & (ki >= qi))
          | ((ki >= w) & (ki < 2 * w))
          | ((ki >= 2 * w) & (ki < 3 * w) & (j < n_blocks - 1) & (ki - 2 * w <= qi))
          | (ki >= 3 * w))
    o_ref[...] = _gqa_attend(q_ref[...], kcat, vcat, sink_ref, ok).astype(BF16)


def _lat_attn(st, q, k, v, k_ctx, v_ctx, sink):
    nq, nk = N_HEADS * HEAD_DIM, N_KV * HEAD_DIM
    seq, past = st.seq, k_ctx.shape[0] // st.batch
    n_blocks = seq // WINDOW
    return pl.pallas_call(
        functools.partial(_lat_attn_kernel, n_blocks=n_blocks),
        grid=(st.batch, n_blocks),
        in_specs=[pl.BlockSpec(memory_space=pltpu.SMEM),
                  pl.BlockSpec((WINDOW, nq), lambda b, j: (b * n_blocks + j, 0)),
                  pl.BlockSpec((seq, nk), lambda b, j: (b, 0)),
                  pl.BlockSpec((seq, nk), lambda b, j: (b, 0)),
                  pl.BlockSpec((past, nk), lambda b, j: (b, 0)),
                  pl.BlockSpec((past, nk), lambda b, j: (b, 0))],
        out_specs=pl.BlockSpec((WINDOW, nq), lambda b, j: (b * n_blocks + j, 0)),
        out_shape=jax.ShapeDtypeStruct((st.n_tok, nq), BF16),
        compiler_params=_cparams(("arbitrary", "arbitrary"), VMEM_LIMIT_BYTES),
        name="latent_attention",
    )(sink, q, k, v, k_ctx, v_ctx)


def _fourier_kernel(x_ref, mod_ref, g_ref, cs_ref, ss_ref, cc_ref, sc_ref, o_ref, *, scale):
    m = mod_ref[...]
    h = _norm_mod(x_ref[...], g_ref[...], _mod_slice(m, 0), _mod_slice(m, 1)).astype(BF16)
    p = _dot(cs_ref[...], h).astype(BF16)
    q = _dot(ss_ref[...], h).astype(BF16)
    gw = FOURIER_GW
    cc, sc = cc_ref[...], sc_ref[...]
    outs = []
    for g in range(FOURIER_GROUPS):
        sl = slice(g * gw, (g + 1) * gw)
        outs.append(_dot(p[:, sl], cc) - _dot(q[:, sl], sc))
    o_ref[...] = (jnp.concatenate(outs, axis=1) * scale).astype(BF16)


def _dft_tables(n):
    k = np.arange(n, dtype=np.int64)
    ang = ((k[:, None] * k[None, :]) % n).astype(np.float64) * (2.0 * math.pi / n)
    return jnp.asarray(np.cos(ang), dtype=F32).astype(BF16), jnp.asarray(np.sin(ang), dtype=F32).astype(BF16)


def _fourier(st, x, mods, layer, g1):
    seq = st.seq
    cs, ss = _dft_tables(seq)
    cc, sc = _dft_tables(FOURIER_GW)
    mod_spec = pl.BlockSpec((None, None, 1, N_MOD * D), lambda b: (layer, 0 if st.shared else 1 + b, 0, 0))
    return pl.pallas_call(
        functools.partial(_fourier_kernel, scale=1.0 / math.sqrt(seq * FOURIER_GW)),
        grid=(st.batch,),
        in_specs=[pl.BlockSpec((seq, D), lambda b: (b, 0)), mod_spec, _const_spec((1, D)),
                  _const_spec((seq, seq)), _const_spec((seq, seq)),
                  _const_spec((FOURIER_GW, FOURIER_GW)), _const_spec((FOURIER_GW, FOURIER_GW))],
        out_specs=pl.BlockSpec((seq, D), lambda b: (b, 0)),
        out_shape=jax.ShapeDtypeStruct((st.n_tok, D), BF16),
        compiler_params=_cparams(("arbitrary",), VMEM_LIMIT_BYTES),
        name="fourier",
    )(x, mods, g1, cs, ss, cc, sc)


def kernel(x_prompt, x_sample, state_rglru, cache_k, cache_v, c, c_ctx, mod_w, mod_b, norm1_g, norm2_g,
           rg_w_in, rg_conv_w, rg_conv_b, rg_w_a, rg_b_a, rg_w_x, rg_b_x, rg_lambda, rg_w_out,
           at_w_qkv, at_q_norm, at_k_norm, at_sink, at_w_o, ft_w, moe_router, moe_w_gate, moe_w_up, moe_w_down):
    depth = mod_w.shape[0]
    batch, seq, _ = x_prompt.shape
    dec_batch, dec_seq, _ = x_sample.shape
    assert 1 + dec_batch <= MOD_ROWS
    streams = (_Stream(batch, seq, True), _Stream(dec_batch, dec_seq, False))
    cond = jnp.concatenate([c_ctx[None, :], c, jnp.zeros((MOD_ROWS - 1 - dec_batch, D), F32)], axis=0)
    mods = _modulation(cond, mod_w, mod_b).reshape(depth, MOD_ROWS, 1, N_MOD * D)

    xs = [x_prompt.reshape(batch * seq, D), x_sample.reshape(dec_batch * dec_seq, D)]
    new_rg, new_k, new_v = [], [], []
    n_mixers = 3
    for layer in range(depth):
        kind, j = layer % n_mixers, layer // n_mixers
        g1 = norm1_g[layer].reshape(1, D)
        g2 = norm2_g[layer].reshape(1, D)
        wr = _router_pieces(moe_router[layer])
        for si, st in enumerate(streams):
            x = xs[si]
            if kind == 0:
                gate, u = _rg_in(st, x, mods, layer, g1, rg_w_in[j].astype(BF16))
                if st.shared:
                    h0 = jnp.zeros((2, st.batch, D), F32)
                else:
                    h0 = jnp.transpose(state_rglru[:, j], (1, 0, 2))
                h, fin = _rg_scan(st, u, h0, rg_conv_w[j], rg_conv_b[j], rg_w_a[j], rg_b_a[j],
                                  rg_w_x[j], rg_b_x[j], rg_lambda[j])
                if st.shared:
                    new_rg.append(jnp.transpose(fin, (1, 0, 2)))
                x1, xnb, aff =_lin_out(st, (h, gate), rg_w_out[j].astype(BF16), x, mods, layer, g2, wr, True)
            elif kind == 1:
                w_qkv = at_w_qkv[j].astype(BF16)
                if st.shared:
                    q, k, v, kc, vc = _qkv(st, x, mods, layer, g1, w_qkv, at_q_norm[j], at_k_norm[j], False)
                    new_k.append(kc.reshape(st.batch, st.seq, N_KV, HEAD_DIM))
                    new_v.append(vc.reshape(st.batch, st.seq, N_KV, HEAD_DIM))
                    a = _ctx_attn(st, q, k, v, at_sink[j])
                else:
                    q, k, v = _qkv(st, x, mods, layer, g1, w_qkv, at_q_norm[j], at_k_norm[j], True)
                    nk = N_KV * HEAD_DIM
                    a = _lat_attn(st, q, k, v, cache_k[:, j].reshape(-1, nk), cache_v[:, j].reshape(-1, nk),
                                  at_sink[j])
                x1, xnb, aff =_lin_out(st, (a,), at_w_o[j].astype(BF16), x, mods, layer, g2, wr, False)
            else:
                a = _fourier(st, x, mods, layer, g1)
                x1, xnb, aff =_lin_out(st, (a,), ft_w[j].astype(BF16), x, mods, layer, g2, wr, False)
            xs[si] = _moe(st, x1, xnb, aff, mods, layer, moe_w_gate, moe_w_up, moe_w_down)
    return (xs[0].reshape(batch, seq, D), xs[1].reshape(dec_batch, dec_seq, D),
            jnp.stack(new_rg, axis=1), jnp.stack(new_k, axis=1), jnp.stack(new_v, axis=1))
```
